```python
import math
import jax, jax.numpy as jnp
from jax import lax
import numpy as np

D_MODEL = 1024
BATCH = 2
SEQ = 8192
DEPTH = 4

GRID_W = 64
CTX_LEN = 256
N_MIXERS = 2
N_ATTN_LAYERS = (DEPTH + 1) // 2
N_HYENA_LAYERS = DEPTH // 2
N_HEADS = 16
N_KV_HEADS = 4
HEAD_DIM = D_MODEL // N_HEADS
KV_GROUP = N_HEADS // N_KV_HEADS
ROPE_THETA = 10000.0
Q_BLOCK = 128
D_FF = 256 * ((8 * D_MODEL // 3 + 255) // 256)
HYENA_ORDER = 2
HYENA_DIRS = 2
SHORT_CONV = 3
FILTER_BANDS = 16
FILTER_EMB = 1 + 2 * FILTER_BANDS
FILTER_HIDDEN = 64
DECAY_TARGET = 1e-2
FAST_DECAY_PCT = 0.3
SLOW_DECAY_PCT = 1.5
N_MOD = 9
EPS = 1e-6

kernel_name = "hybrid_gqa_hyena_macaron_dit"


def rms_norm(x, w):
    xf = x.astype(jnp.float32)
    y = xf * lax.rsqrt(jnp.mean(xf * xf, axis=-1, keepdims=True) + EPS)
    return (y * w.astype(jnp.float32)).astype(x.dtype)


def modulate(x, g, shift, scale):
    return rms_norm(x, g) * (1 + scale) + shift


def swiglu(h, w_gu, w_down):
    g, u = jnp.split(h @ w_gu, 2, axis=-1)
    return (jax.nn.silu(g) * u) @ w_down


def ffn_sublayer(s, mod, k, norm_g, w_gu, w_down):
    h = modulate(s, norm_g, mod[3 * k], mod[3 * k + 1])
    return s + 0.5 * mod[3 * k + 2] * swiglu(h, w_gu, w_down)


def grid_positions(n):
    n_rows = n // GRID_W
    rows = jnp.repeat(jnp.arange(n_rows, dtype=jnp.int32), GRID_W)
    cols = jnp.tile(jnp.arange(GRID_W, dtype=jnp.int32), n_rows)
    return rows, cols


def rope_axis(x, pos):
    half = x.shape[-1] // 2
    freqs = ROPE_THETA ** (-jnp.arange(half, dtype=jnp.float32) / half)
    ang = pos.astype(jnp.float32)[:, None] * freqs
    cos = jnp.cos(ang)[:, None, :].astype(x.dtype)
    sin = jnp.sin(ang)[:, None, :].astype(x.dtype)
    x1, x2 = x[..., :half], x[..., half:]
    return jnp.concatenate([x1 * cos - x2 * sin, x1 * sin + x2 * cos], axis=-1)


def rope_2d(x, rows, cols):
    h = x.shape[-1] // 2
    return jnp.concatenate([rope_axis(x[..., :h], rows), rope_axis(x[..., h:], cols)], axis=-1)


def qkv_heads(h, w_qkv, q_norm, k_norm):
    B, L, _ = h.shape
    nq, nk = N_HEADS * HEAD_DIM, N_KV_HEADS * HEAD_DIM
    qkv = h @ w_qkv
    q = qkv[..., :nq].reshape(B, L, N_HEADS, HEAD_DIM)
    k = qkv[..., nq:nq + nk].reshape(B, L, N_KV_HEADS, HEAD_DIM)
    v = qkv[..., nq + nk:].reshape(B, L, N_KV_HEADS, HEAD_DIM)
    return rms_norm(q, q_norm), rms_norm(k, k_norm), v


def gqa(q, k, v):
    B, Lq = q.shape[0], q.shape[1]
    q = q.reshape(B, Lq, N_KV_HEADS, KV_GROUP, HEAD_DIM) * (HEAD_DIM ** -0.5)
    s = jnp.einsum('bqkgd,bnkd->bkgqn', q, k, preferred_element_type=jnp.float32)
    p = jax.nn.softmax(s, axis=-1).astype(v.dtype)
    o = jnp.einsum('bkgqn,bnkd->bqkgd', p, v)
    return o.reshape(B, Lq, N_HEADS * HEAD_DIM)


def attention_mixer(h_lat, h_ctx, w_qkv, w_o, q_norm, k_norm, with_ctx_out):
    B, S, _ = h_lat.shape
    rows, cols = grid_positions(S)
    q_l, k_l, v_l = qkv_heads(h_lat, w_qkv, q_norm, k_norm)
    q_l = rope_2d(q_l, rows, cols)
    k_l = rope_2d(k_l, rows, cols)
    q_c, k_c, v_c = qkv_heads(h_ctx, w_qkv, q_norm, k_norm)
    k_all = jnp.concatenate([k_c, k_l], axis=1)
    v_all = jnp.concatenate([v_c, v_l], axis=1)
    n_blk = S // Q_BLOCK
    q_blocks = q_l.reshape(B, n_blk, Q_BLOCK, N_HEADS, HEAD_DIM).swapaxes(0, 1)
    o_blocks = lax.map(lambda qb: gqa(qb, k_all, v_all), q_blocks)
    o_lat = o_blocks.swapaxes(0, 1).reshape(B, S, N_HEADS * HEAD_DIM) @ w_o
    o_ctx = gqa(q_c, k_c, v_c) @ w_o if with_ctx_out else None
    return o_lat, o_ctx


def hyena_filters(L, f_w1, f_b1, f_w2, f_b2, f_w3, f_b3, f_wout, f_freq):
    f32 = jnp.float32
    t = jnp.linspace(0.0, 1.0, L, dtype=f32)[:, None]
    w = 2.0 * math.pi * jnp.arange(L, dtype=f32)[:, None] / L
    bands = jnp.linspace(1e-4, FILTER_BANDS - 1, FILTER_BANDS, dtype=f32)
    feats = jnp.concatenate([t, jnp.cos(bands * w), -jnp.sin(bands * w)], axis=-1)
    a = f_freq.astype(f32)
    hid = jnp.sin(a * (feats @ f_w1.astype(f32) + f_b1.astype(f32)))
    hid = jnp.sin(a * (hid @ f_w2.astype(f32) + f_b2.astype(f32)))
    hid = jnp.sin(a * (hid @ f_w3.astype(f32) + f_b3.astype(f32)))
    filt = (hid @ f_wout.astype(f32)).reshape(L, HYENA_ORDER, HYENA_DIRS, D_MODEL)
    min_decay = math.log(DECAY_TARGET) / SLOW_DECAY_PCT
    max_decay = math.log(DECAY_TARGET) / FAST_DECAY_PCT
    deltas = jnp.abs(jnp.linspace(min_decay, max_decay, D_MODEL, dtype=f32))
    decay = jnp.exp(-t * deltas)
    return filt * decay[:, None, None, :]


def bidir_filter_taps(filt_o):
    fwd, bwd = filt_o[:, 0], filt_o[:, 1]
    zero = jnp.zeros((1, fwd.shape[-1]), fwd.dtype)
    return jnp.concatenate([fwd[:1] + bwd[:1], fwd[1:], zero, bwd[:0:-1]], axis=0)


def fft_conv(z, taps, bias):
    L = z.shape[1]
    zf = z.astype(jnp.float32)
    zq = jnp.fft.rfft(zf, n=2 * L, axis=1)
    tq = jnp.fft.rfft(taps, n=2 * L, axis=0)[None]
    y = jnp.fft.irfft(zq * tq, n=2 * L, axis=1)[:, :L]
    return (y + zf * bias.astype(jnp.float32)).astype(z.dtype)


def short_conv(u, w, b):
    up = jnp.pad(u, ((0, 0), (1, 1), (0, 0)))
    return up[:, :-2] * w[0] + up[:, 1:-1] * w[1] + up[:, 2:] * w[2] + b


def hyena_mixer(h, w_in, b_in, conv_w, conv_b, f_w1, f_b1, f_w2, f_b2, f_w3, f_b3, f_wout, f_freq, f_bias, w_out, b_out):
    L = h.shape[1]
    u = short_conv(h @ w_in + b_in, conv_w, conv_b)
    v, x1, x2 = jnp.split(u, 3, axis=-1)
    filt = hyena_filters(L, f_w1, f_b1, f_w2, f_b2, f_w3, f_b3, f_wout, f_freq)
    z = x1 * fft_conv(v, bidir_filter_taps(filt[:, 0]), f_bias[0])
    y = x2 * fft_conv(z, bidir_filter_taps(filt[:, 1]), f_bias[1])
    return y @ w_out + b_out


def setup_inputs(seed: int = 0) -> dict:
    key = jax.random.key(seed)
    ks = iter(jax.random.split(key, 32))
    f32 = jnp.float32

    def nrm(shape, scale):
        return jax.random.normal(next(ks), shape, f32) * scale

    D, F = D_MODEL, D_FF
    QKV = (N_HEADS + 2 * N_KV_HEADS) * HEAD_DIM
    nA, nH = N_ATTN_LAYERS, N_HYENA_LAYERS
    FH = FILTER_HIDDEN
    return {
        "x": nrm((BATCH, SEQ, D), 1.0),
        "c": nrm((BATCH, D), 1.0),
        "ctx": nrm((BATCH, CTX_LEN, D), 1.0),
        "c_ctx": nrm((D,), 1.0),
        "w_mod": nrm((DEPTH, D, N_MOD * D), 0.5 * D ** -0.5),
        "b_mod": nrm((DEPTH, N_MOD * D), 0.02),
        "norm_w": 1.0 + nrm((DEPTH, 3, D), 0.05),
        "ffn_w_gate_up": nrm((DEPTH, 2, D, 2 * F), D ** -0.5),
        "ffn_w_down": nrm((DEPTH, 2, F, D), F ** -0.5),
        "attn_w_qkv": nrm((nA, D, QKV), D ** -0.5),
        "attn_w_o": nrm((nA, N_HEADS * HEAD_DIM, D), (N_HEADS * HEAD_DIM) ** -0.5),
        "attn_q_norm": 1.0 + nrm((nA, HEAD_DIM), 0.05),
        "attn_k_norm": 1.0 + nrm((nA, HEAD_DIM), 0.05),
        "hy_w_in": nrm((nH, D, 3 * D), D ** -0.5),
        "hy_b_in": nrm((nH, 3 * D), 0.02),
        "hy_conv_w": nrm((nH, SHORT_CONV, 3 * D), SHORT_CONV ** -0.5),
        "hy_conv_b": nrm((nH, 3 * D), 0.02),
        "hy_f_w1": nrm((nH, FILTER_EMB, FH), FILTER_EMB ** -0.5),
        "hy_f_b1": nrm((nH, FH), 0.1),
        "hy_f_w2": nrm((nH, FH, FH), FH ** -0.5),
        "hy_f_b2": nrm((nH, FH), 0.1),
        "hy_f_w3": nrm((nH, FH, FH), FH ** -0.5),
        "hy_f_b3": nrm((nH, FH), 0.1),
        "hy_f_wout": nrm((nH, FH, HYENA_ORDER * HYENA_DIRS * D), 0.03 * FH ** -0.5),
        "hy_f_freq": 1.0 + nrm((nH, FH), 0.05),
        "hy_f_bias": nrm((nH, HYENA_ORDER, D), 0.5),
        "hy_w_out": nrm((nH, D, D), D ** -0.5),
        "hy_b_out": nrm((nH, D), 0.02),
    }


def reference(x, c, ctx, c_ctx, w_mod, b_mod, norm_w, ffn_w_gate_up, ffn_w_down,
              attn_w_qkv, attn_w_o, attn_q_norm, attn_k_norm,
              hy_w_in, hy_b_in, hy_conv_w, hy_conv_b, hy_f_w1, hy_f_b1, hy_f_w2, hy_f_b2,
              hy_f_w3, hy_f_b3, hy_f_wout, hy_f_freq, hy_f_bias, hy_w_out, hy_b_out):
    B, D = x.shape[0], x.shape[-1]
    for l in range(DEPTH):
        mod_x = (jax.nn.silu(c) @ w_mod[l] + b_mod[l]).reshape(B, N_MOD, 1, D).swapaxes(0, 1)
        mod_c = (jax.nn.silu(c_ctx) @ w_mod[l] + b_mod[l]).reshape(N_MOD, 1, 1, D)
        is_attn = (l % N_MIXERS) == 0
        ctx_out = l < DEPTH - 1
        ctx_live = ctx_out or is_attn

        x = ffn_sublayer(x, mod_x, 0, norm_w[l, 0], ffn_w_gate_up[l, 0], ffn_w_down[l, 0])
        if ctx_live:
            ctx = ffn_sublayer(ctx, mod_c, 0, norm_w[l, 0], ffn_w_gate_up[l, 0], ffn_w_down[l, 0])

        h_x = modulate(x, norm_w[l, 1], mod_x[3], mod_x[4])
        if is_attn:
            a = l // N_MIXERS
            h_c = modulate(ctx, norm_w[l, 1], mod_c[3], mod_c[4])
            o_x, o_c = attention_mixer(h_x, h_c, attn_w_qkv[a], attn_w_o[a],
                                       attn_q_norm[a], attn_k_norm[a], ctx_out)
        else:
            j = l // N_MIXERS
            hp = (hy_w_in[j], hy_b_in[j], hy_conv_w[j], hy_conv_b[j], hy_f_w1[j], hy_f_b1[j],
                  hy_f_w2[j], hy_f_b2[j], hy_f_w3[j], hy_f_b3[j], hy_f_wout[j], hy_f_freq[j],
                  hy_f_bias[j], hy_w_out[j], hy_b_out[j])
            o_x = hyena_mixer(h_x, *hp)
            o_c = hyena_mixer(modulate(ctx, norm_w[l, 1], mod_c[3], mod_c[4]), *hp) if ctx_out else None
        x = x + mod_x[5] * o_x
        if ctx_out:
            ctx = ctx + mod_c[5] * o_c

        x = ffn_sublayer(x, mod_x, 2, norm_w[l, 2], ffn_w_gate_up[l, 1], ffn_w_down[l, 1])
        if ctx_out:
            ctx = ffn_sublayer(ctx, mod_c, 2, norm_w[l, 2], ffn_w_gate_up[l, 1], ffn_w_down[l, 1])
    return x
```

```python
import functools
import math

import jax
import jax.numpy as jnp
import numpy as np
from jax import lax
from jax.experimental import pallas as pl
from jax.experimental.pallas import tpu as pltpu

F32 = jnp.float32
BF16 = jnp.bfloat16

N_MOD = 9
N_HEADS = 16
N_KV_HEADS = 4
HEAD_DIM = 64
KV_GROUP = N_HEADS // N_KV_HEADS
GRID_W = 64
ROPE_THETA = 10000.0
EPS = 1e-6
FILTER_BANDS = 16
DECAY_TARGET = 1e-2
FAST_DECAY_PCT = 0.3
SLOW_DECAY_PCT = 1.5

LANES = 128
HALO = 16
FFT_N2 = 128
VMEM_LIMIT = 56 * 1024 * 1024

FFN_TM = 512
PRE_TM = 256
FFN_FC = 256
ATT_TQ = 128
ATT_TK = 512
FFT_NB = 8
FFT_KB = 4
FFT_CT = 512


def _cparams(sem):
    return pltpu.CompilerParams(dimension_semantics=sem, vmem_limit_bytes=VMEM_LIMIT)


def _const_spec(shape):
    nd = len(shape)
    return pl.BlockSpec(shape, lambda *_: (0,) * nd, pipeline_mode=pl.Buffered(1))


def _modulated(s, g, shift, scale):
    ms = jnp.mean(s * s, axis=-1, keepdims=True)
    return (s * lax.rsqrt(ms + EPS) * g) * (1.0 + scale) + shift


def _mod_kernel(c_ref, w_ref, b_ref, o_ref):
    c = c_ref[...]
    a = (c * jax.nn.sigmoid(c)).astype(BF16)
    o_ref[0] = jnp.dot(a, w_ref[0].astype(BF16), preferred_element_type=F32) + b_ref[0]


def _mod_vectors(cc, w_mod, b_mod):
    depth, d, n = w_mod.shape
    tn = n // 6
    return pl.pallas_call(
        _mod_kernel,
        grid=(depth, n // tn),
        in_specs=[pl.BlockSpec((8, d), lambda l, j: (0, 0)),
                  pl.BlockSpec((1, d, tn), lambda l, j: (l, 0, j)),
                  pl.BlockSpec((1, 1, tn), lambda l, j: (l, 0, j))],
        out_specs=pl.BlockSpec((1, 8, tn), lambda l, j: (l, 0, j)),
        out_shape=jax.ShapeDtypeStruct((depth, 8, n), F32),
        compiler_params=_cparams(("arbitrary", "arbitrary")),
        name="mod_vectors",
    )(cc, w_mod, b_mod.reshape(depth, 1, n))


def _mod_blockspec(d, tiles_per_row, fixed_row):
    if fixed_row is None:
        return pl.BlockSpec((None, N_MOD, d), lambda i: (i // tiles_per_row, 0, 0))
    return pl.BlockSpec((None, N_MOD, d), lambda i: (fixed_row, 0, 0))


def _ffn_kernel(k, n_chunks, s_ref, mod_ref, g_ref, wgu_ref, wd_ref, o_ref, h_ref, acc_ref):
    s = s_ref[...]
    shift = mod_ref[3 * k:3 * k + 1, :]
    scale = mod_ref[3 * k + 1:3 * k + 2, :]
    gate = mod_ref[3 * k + 2:3 * k + 3, :]
    h_ref[...] = _modulated(s, g_ref[...], shift, scale).astype(BF16)
    acc_ref[...] = jnp.zeros_like(acc_ref)
    fc = wd_ref.shape[1]

    def body(c, carry):
        gu = jnp.dot(h_ref[...], wgu_ref[c], preferred_element_type=F32)
        g = gu[:, :fc]
        u = gu[:, fc:]
        a = (g * jax.nn.sigmoid(g) * u).astype(BF16)
        acc_ref[...] += jnp.dot(a, wd_ref[c], preferred_element_type=F32)
        return carry

    lax.fori_loop(0, n_chunks, body, 0)
    o_ref[...] = s + 0.5 * gate * acc_ref[...]


def _ffn(s, mod, k, g, wgu, wd, tiles_per_row, fixed_row):
    n, d = s.shape
    tm = min(FFN_TM, n)
    n_chunks, _, fc2 = wgu.shape
    return pl.pallas_call(
        functools.partial(_ffn_kernel, k, n_chunks),
        grid=(n // tm,),
        in_specs=[pl.BlockSpec((tm, d), lambda i: (i, 0)),
                  _mod_blockspec(d, tiles_per_row, fixed_row),
                  _const_spec((1, d)),
                  _const_spec(wgu.shape),
                  _const_spec(wd.shape)],
        out_specs=pl.BlockSpec((tm, d), lambda i: (i, 0)),
        out_shape=jax.ShapeDtypeStruct((n, d), F32),
        scratch_shapes=[pltpu.VMEM((tm, d), BF16), pltpu.VMEM((tm, d), F32)],
        compiler_params=_cparams(("arbitrary",)),
        name="ffn",
    )(s, mod, g.reshape(1, d), wgu, wd)


def _prep_ffn_weights(w_gate_up, w_down):
    d, f2 = w_gate_up.shape
    f = f2 // 2
    nc = f // FFN_FC
    wg = w_gate_up[:, :f].reshape(d, nc, FFN_FC)
    wu = w_gate_up[:, f:].reshape(d, nc, FFN_FC)
    wgu = jnp.concatenate([wg, wu], axis=-1).transpose(1, 0, 2).astype(BF16)
    wd = w_down.reshape(nc, FFN_FC, d).astype(BF16)
    return wgu, wd


def _proj_kernel(a_ref, s_ref, mod_ref, w_ref, b_ref, o_ref):
    gate = mod_ref[5:6, :]
    y = jnp.dot(a_ref[...].astype(BF16), w_ref[...], preferred_element_type=F32) + b_ref[...]
    o_ref[...] = s_ref[...] + gate * y


def _proj_residual(a, s, mod, w, b, tiles_per_row, fixed_row):
    n, d = s.shape
    tm = min(FFN_TM, n)
    return pl.pallas_call(
        _proj_kernel,
        grid=(n // tm,),
        in_specs=[pl.BlockSpec((tm, a.shape[1]), lambda i: (i, 0)),
                  pl.BlockSpec((tm, d), lambda i: (i, 0)),
                  _mod_blockspec(d, tiles_per_row, fixed_row),
                  _const_spec(w.shape),
                  _const_spec((1, d))],
        out_specs=pl.BlockSpec((tm, d), lambda i: (i, 0)),
        out_shape=jax.ShapeDtypeStruct((n, d), F32),
        compiler_params=_cparams(("arbitrary",)),
        name="proj_residual",
    )(a, s, mod, w, b.reshape(1, d))


def _head_norm(x, p, pt, w):
    sq = x * x
    hi = sq.astype(BF16)
    lo = (sq - hi.astype(F32)).astype(BF16)
    ss = jnp.dot(hi, p, preferred_element_type=F32) + jnp.dot(lo, p, preferred_element_type=F32)
    r = lax.rsqrt(ss * (1.0 / HEAD_DIM) + EPS)
    rh = r.astype(BF16)
    rl = (r - rh.astype(F32)).astype(BF16)
    rb = jnp.dot(rh, pt, preferred_element_type=F32) + jnp.dot(rl, pt, preferred_element_type=F32)
    return x * rb * w


def _rope_block(xb, cos, sin):
    lane = lax.broadcasted_iota(jnp.int32, xb.shape, 1)
    fwd = pltpu.roll(xb, 16, axis=1)
    bwd = pltpu.roll(xb, LANES - 16, axis=1)
    partner = jnp.where((lane % 32) < 16, bwd, fwd)
    return xb * cos + partner * sin


def _attn_pre_kernel(rope, s_ref, mod_ref, g_ref, w_ref, qn_ref, kn_ref, p_ref, pt_ref,
                     cos_ref, sin_ref, q_ref, kt_ref, v_ref):
    s = s_ref[...]
    tm = s.shape[0]
    nq = N_HEADS * HEAD_DIM
    nk = N_KV_HEADS * HEAD_DIM
    h = _modulated(s, g_ref[...], mod_ref[3:4, :], mod_ref[4:5, :]).astype(BF16)
    qkv = jnp.dot(h, w_ref[...], preferred_element_type=F32)
    q = _head_norm(qkv[:, :nq], p_ref[...], pt_ref[...], qn_ref[...])
    k = _head_norm(qkv[:, nq:nq + nk], p_ref[:nk, :], pt_ref[:, :nk], kn_ref[...])
    v = qkv[:, nq + nk:]
    cos = cos_ref[...]
    sin = sin_ref[...]
    for j in range(nq // LANES):
        qb = q[:, j * LANES:(j + 1) * LANES]
        if rope:
            qb = _rope_block(qb, cos, sin)
        q_ref[:, j * LANES:(j + 1) * LANES] = qb.astype(BF16)
    kblocks = []
    for j in range(nk // LANES):
        kb = k[:, j * LANES:(j + 1) * LANES]
        if rope:
            kb = _rope_block(kb, cos, sin)
        kblocks.append(kb)
    kt = jnp.concatenate(kblocks, axis=1).T.astype(BF16)
    for hh in range(N_KV_HEADS):
        kt_ref[hh] = kt[hh * HEAD_DIM:(hh + 1) * HEAD_DIM, :]
    lane = lax.broadcasted_iota(jnp.int32, (tm, LANES), 1)
    ones_col = jnp.where(lane == HEAD_DIM, 1.0, 0.0)
    for hh in range(N_KV_HEADS):
        vb = v[:, (hh // 2) * LANES:(hh // 2 + 1) * LANES]
        if hh % 2 == 1:
            vb = pltpu.roll(vb, HEAD_DIM, axis=1)
        v_ref[hh] = jnp.where(lane < HEAD_DIM, vb, ones_col).astype(BF16)


def _attn_pre(s, mod, g, w_qkv, qn, kn, p, pt, cos, sin, seq_len, fixed_row, rope):
    n, d = s.shape
    tm = PRE_TM
    tps = seq_len // tm
    n_seq = n // seq_len
    nq = N_HEADS * HEAD_DIM
    return pl.pallas_call(
        functools.partial(_attn_pre_kernel, rope),
        grid=(n // tm,),
        in_specs=[pl.BlockSpec((tm, d), lambda i: (i, 0)),
                  _mod_blockspec(d, tps, fixed_row),
                  _const_spec((1, d)),
                  _const_spec(w_qkv.shape),
                  _const_spec((1, nq)),
                  _const_spec((1, N_KV_HEADS * HEAD_DIM)),
                  _const_spec(p.shape),
                  _const_spec(pt.shape),
                  pl.BlockSpec((tm, LANES), lambda i: (i % tps, 0)),
                  pl.BlockSpec((tm, LANES), lambda i: (i % tps, 0))],
        out_specs=[pl.BlockSpec((tm, nq), lambda i: (i, 0)),
                   pl.BlockSpec((None, N_KV_HEADS, HEAD_DIM, tm), lambda i: (i // tps, 0, 0, i % tps)),
                   pl.BlockSpec((None, N_KV_HEADS, tm, LANES), lambda i: (i // tps, 0, i % tps, 0))],
        out_shape=[jax.ShapeDtypeStruct((n, nq), BF16),
                   jax.ShapeDtypeStruct((n_seq, N_KV_HEADS, HEAD_DIM, seq_len), BF16),
                   jax.ShapeDtypeStruct((n_seq, N_KV_HEADS, seq_len, LANES), BF16)],
        compiler_params=_cparams(("arbitrary",)),
        name="attn_pre",
    )(s, mod, g.reshape(1, d), w_qkv, qn, kn, p, pt, cos, sin)


def _rope_tables(seq_len):
    half = HEAD_DIM // 4
    t = np.arange(seq_len)
    pos = np.stack([t // GRID_W, t % GRID_W], axis=1).astype(np.float32)
    freqs = (ROPE_THETA ** (-np.arange(half, dtype=np.float32) / half)).astype(np.float32)
    lane = np.arange(LANES)
    axis = (lane % HEAD_DIM) // (HEAD_DIM // 2)
    e = lane % (HEAD_DIM // 2)
    ang = pos[:, axis] * freqs[e % half][None, :]
    sign = np.where(e < half, -1.0, 1.0).astype(np.float32)
    return jnp.asarray(np.cos(ang), F32), jnp.asarray(np.sin(ang) * sign[None, :], F32)


def _head_sum_matrices():
    lane = np.arange(N_HEADS * HEAD_DIM)
    p = (lane[:, None] // HEAD_DIM == np.arange(LANES)[None, :]).astype(np.float32)
    return jnp.asarray(p, BF16), jnp.asarray(p.T, BF16)


def _attn_kernel(n_lat_chunks, tk, q_ref, ktc_ref, vc_ref, *rest):
    if n_lat_chunks:
        ktl_ref, vl_ref, o_ref = rest
    else:
        (o_ref,) = rest
    q = q_ref[...]
    tq = q.shape[0]
    q4 = jnp.concatenate([q[:, g * HEAD_DIM:(g + 1) * HEAD_DIM] for g in range(KV_GROUP)], axis=0)

    def step(kt, v, m, acc):
        s = jnp.dot(q4, kt, preferred_element_type=F32)
        m_new = jnp.maximum(m, jnp.max(s, axis=-1, keepdims=True))
        p = jnp.exp(s - m_new)
        alpha = jnp.exp(m - m_new)
        acc = alpha * acc + jnp.dot(p.astype(BF16), v, preferred_element_type=F32)
        return m_new, acc

    m0 = jnp.full((KV_GROUP * tq, 1), -1e30, F32)
    acc0 = jnp.zeros((KV_GROUP * tq, LANES), F32)
    m, acc = step(ktc_ref[...], vc_ref[...], m0, acc0)
    if n_lat_chunks:
        def body(c, carry):
            off = pl.multiple_of(c * tk, tk)
            return step(ktl_ref[:, pl.ds(off, tk)], vl_ref[pl.ds(off, tk), :], *carry)
        m, acc = lax.fori_loop(0, n_lat_chunks, body, (m, acc))
    o = acc[:, :HEAD_DIM] / acc[:, HEAD_DIM:HEAD_DIM + 1]
    o_ref[...] = jnp.concatenate([o[g * tq:(g + 1) * tq, :] for g in range(KV_GROUP)], axis=1).astype(BF16)


def _attention(q, ktc, vc, ktl, vl, lq):
    n, nq = q.shape
    b = n // lq
    tq = ATT_TQ
    nqt = lq // tq
    gw = KV_GROUP * HEAD_DIM
    lc = ktc.shape[-1]
    in_specs = [pl.BlockSpec((tq, gw), lambda bb, j, i: (bb * nqt + i, j)),
                pl.BlockSpec((None, None, HEAD_DIM, lc), lambda bb, j, i: (bb, j, 0, 0)),
                pl.BlockSpec((None, None, lc, LANES), lambda bb, j, i: (bb, j, 0, 0))]
    args = [q, ktc, vc]
    n_lat = 0
    if ktl is not None:
        ll = ktl.shape[-1]
        n_lat = ll // ATT_TK
        in_specs += [pl.BlockSpec((None, None, HEAD_DIM, ll), lambda bb, j, i: (bb, j, 0, 0)),
                     pl.BlockSpec((None, None, ll, LANES), lambda bb, j, i: (bb, j, 0, 0))]
        args += [ktl, vl]
    return pl.pallas_call(
        functools.partial(_attn_kernel, n_lat, ATT_TK),
        grid=(b, N_KV_HEADS, nqt),
        in_specs=in_specs,
        out_specs=pl.BlockSpec((tq, gw), lambda bb, j, i: (bb * nqt + i, j)),
        out_shape=jax.ShapeDtypeStruct((n, nq), BF16),
        compiler_params=_cparams(("arbitrary", "arbitrary", "arbitrary")),
        name="attention",
    )(*args)


def _hyena_pre_kernel(tps, n_col, s_ref, prev_ref, next_ref, mod_ref, g_ref, w_ref, b_ref,
                      cw_ref, cb_ref, o_ref, h_ref, pre_ref):
    i = pl.program_id(0)
    tm = s_ref.shape[0]
    hb = HALO
    g = g_ref[...]
    shift = mod_ref[3:4, :]
    scale = mod_ref[4:5, :]
    h_ref[0:hb, :] = _modulated(prev_ref[...], g, shift, scale).astype(BF16)
    h_ref[hb:hb + tm, :] = _modulated(s_ref[...], g, shift, scale).astype(BF16)
    h_ref[hb + tm:, :] = _modulated(next_ref[...], g, shift, scale).astype(BF16)
    row = lax.broadcasted_iota(jnp.int32, (tm, 1), 0)
    drop_up = jnp.logical_and(row == 0, i % tps == 0)
    drop_dn = jnp.logical_and(row == tm - 1, i % tps == tps - 1)
    ct = w_ref.shape[1] // n_col
    for c in range(n_col):
        cols = slice(c * ct, (c + 1) * ct)
        pre_ref[...] = jnp.dot(h_ref[...], w_ref[:, cols], preferred_element_type=F32) + b_ref[:, cols]
        up = jnp.where(drop_up, 0.0, pre_ref[hb - 1:hb - 1 + tm, :])
        mid = pre_ref[hb:hb + tm, :]
        dn = jnp.where(drop_dn, 0.0, pre_ref[hb + 1:hb + 1 + tm, :])
        o_ref[:, cols] = (up * cw_ref[0:1, cols] + mid * cw_ref[1:2, cols]
                          + dn * cw_ref[2:3, cols] + cb_ref[:, cols])


def _hyena_pre(s, mod, g, w_in, b_in, conv_w, conv_b, seq_len, fixed_row):
    n, d = s.shape
    tm = PRE_TM
    tps = seq_len // tm
    n3 = w_in.shape[1]
    hb = HALO
    nblk = n // hb
    n_col = 6
    return pl.pallas_call(
        functools.partial(_hyena_pre_kernel, tps, n_col),
        grid=(n // tm,),
        in_specs=[pl.BlockSpec((tm, d), lambda i: (i, 0)),
                  pl.BlockSpec((hb, d), lambda i: (jnp.maximum(i * (tm // hb) - 1, 0), 0)),
                  pl.BlockSpec((hb, d), lambda i: (jnp.minimum((i + 1) * (tm // hb), nblk - 1), 0)),
                  _mod_blockspec(d, tps, fixed_row),
                  _const_spec((1, d)),
                  _const_spec(w_in.shape),
                  _const_spec((1, n3)),
                  _const_spec((3, n3)),
                  _const_spec((1, n3))],
        out_specs=pl.BlockSpec((tm, n3), lambda i: (i, 0)),
        out_shape=jax.ShapeDtypeStruct((n, n3), F32),
        scratch_shapes=[pltpu.VMEM((tm + 2 * hb, d), BF16),
                        pltpu.VMEM((tm + 2 * hb, n3 // n_col), F32)],
        compiler_params=_cparams(("arbitrary",)),
        name="hyena_pre",
    )(s, s, s, mod, g.reshape(1, d), w_in, b_in.reshape(1, n3), conv_w, conv_b.reshape(1, n3))


def _filter_kernel(n_feat, d, feat_ref, w1_ref, b1_ref, w2_ref, b2_ref, w3_ref, b3_ref,
                   wo_ref, a_ref, delta_ref, o_ref):
    hp = lax.Precision.HIGHEST
    feats = feat_ref[...]
    a = a_ref[...]
    hid = jnp.sin(a * (jnp.dot(feats, w1_ref[...], precision=hp, preferred_element_type=F32) + b1_ref[...]))
    hid = jnp.sin(a * (jnp.dot(hid, w2_ref[...], precision=hp, preferred_element_type=F32) + b2_ref[...]))
    hid = jnp.sin(a * (jnp.dot(hid, w3_ref[...], precision=hp, preferred_element_type=F32) + b3_ref[...]))
    t = feats[:, 0:1]
    use_fwd = feats[:, n_feat:n_feat + 1]
    use_bwd = feats[:, n_feat + 1:n_feat + 2]
    decay = jnp.exp(-t * delta_ref[...])
    for o in range(2):
        fwd = jnp.dot(hid, wo_ref[:, (2 * o) * d:(2 * o + 1) * d], precision=hp, preferred_element_type=F32)
        bwd = jnp.dot(hid, wo_ref[:, (2 * o + 1) * d:(2 * o + 2) * d], precision=hp, preferred_element_type=F32)
        o_ref[o] = decay * (use_fwd * fwd + use_bwd * bwd)


def _filter_features(seq_len):
    l = seq_len
    n_feat = 1 + 2 * FILTER_BANDS
    n = np.arange(2 * l)
    pos = np.where(n < l, n, 2 * l - n)
    pos = np.where(n == l, 0, pos)
    t = np.linspace(0.0, 1.0, l, dtype=np.float32)[pos]
    w = (2.0 * math.pi * pos.astype(np.float32) / l).astype(np.float32)
    bands = np.linspace(1e-4, FILTER_BANDS - 1, FILTER_BANDS, dtype=np.float32)
    bw = (bands[None, :] * w[:, None]).astype(np.float32)
    feats = np.zeros((2 * l, LANES), np.float32)
    feats[:, 0] = t
    feats[:, 1:1 + FILTER_BANDS] = np.cos(bw)
    feats[:, 1 + FILTER_BANDS:n_feat] = -np.sin(bw)
    feats[:, n_feat] = (n < l)
    feats[:, n_feat + 1] = np.logical_or(n > l, n == 0)
    return jnp.asarray(feats, F32), n_feat


def _filter_taps(seq_len, f_w1, f_b1, f_w2, f_b2, f_w3, f_b3, f_wout, f_freq):
    feats, n_feat = _filter_features(seq_len)
    fh = f_w1.shape[1]
    d = f_wout.shape[1] // 4
    w1p = jnp.zeros((LANES, fh), F32).at[:n_feat].set(f_w1)
    min_decay = math.log(DECAY_TARGET) / SLOW_DECAY_PCT
    max_decay = math.log(DECAY_TARGET) / FAST_DECAY_PCT
    deltas = jnp.abs(jnp.linspace(min_decay, max_decay, d, dtype=F32)).reshape(1, d)
    n = 2 * seq_len
    tm = min(512, n)
    return pl.pallas_call(
        functools.partial(_filter_kernel, n_feat, d),
        grid=(n // tm,),
        in_specs=[pl.BlockSpec((tm, LANES), lambda i: (i, 0)),
                  _const_spec((LANES, fh)), _const_spec((1, fh)),
                  _const_spec((fh, fh)), _const_spec((1, fh)),
                  _const_spec((fh, fh)), _const_spec((1, fh)),
                  _const_spec(f_wout.shape), _const_spec((1, fh)), _const_spec((1, d))],
        out_specs=pl.BlockSpec((2, tm, d), lambda i: (0, i, 0)),
        out_shape=jax.ShapeDtypeStruct((2, n, d), F32),
        compiler_params=_cparams(("arbitrary",)),
        name="filter_taps",
    )(feats, w1p, f_b1.reshape(1, fh), f_w2, f_b2.reshape(1, fh), f_w3, f_b3.reshape(1, fh),
      f_wout, f_freq.reshape(1, fh), deltas)


def _dft_tables(seq_len):
    n_fft = 2 * seq_len
    n1 = n_fft // FFT_N2
    k1 = jnp.arange(n1, dtype=jnp.int32)[None, :, None]
    n2 = jnp.arange(FFT_N2, dtype=jnp.int32)[:, None, None]
    nn1 = jnp.arange(n1, dtype=jnp.int32)[None, None, :]
    ang = (2.0 * math.pi / n_fft) * ((k1 * (FFT_N2 * nn1 + n2)) % n_fft).astype(F32)
    c = jnp.cos(ang)
    s = jnp.sin(ang)
    h = n1 // 2
    g_first = jnp.concatenate([jnp.concatenate([c[:, :, :h], s[:, :, :h]], axis=2),
                               jnp.concatenate([-s[:, :, :h], c[:, :, :h]], axis=2)], axis=1)
    g_last = jnp.swapaxes(g_first, 1, 2) * (1.0 / n_fft)
    g_taps = jnp.concatenate([c, -s], axis=1)
    k2 = jnp.arange(FFT_N2, dtype=jnp.int32)
    ang2 = (2.0 * math.pi / FFT_N2) * ((k2[:, None] * k2[None, :]) % FFT_N2).astype(F32)
    c2, s2 = jnp.cos(ang2), jnp.sin(ang2)
    g_mid = jnp.concatenate([jnp.concatenate([c2, s2], axis=1),
                             jnp.concatenate([-s2, c2], axis=1)], axis=0)
    return (g_first.astype(BF16), g_last.astype(BF16), g_taps.astype(BF16),
            g_mid.astype(BF16), g_mid.T.astype(BF16))


def _fft_first_kernel(x_ref, g_ref, o_ref):
    for j in range(FFT_NB):
        o_ref[:, j, :] = jnp.dot(g_ref[j], x_ref[:, j, :].astype(BF16), preferred_element_type=F32)


def _fft_first(x3, col_block, d, g):
    r = x3.shape[0]
    rows_out = g.shape[1]
    return pl.pallas_call(
        _fft_first_kernel,
        grid=(FFT_N2 // FFT_NB, d // FFT_CT),
        in_specs=[pl.BlockSpec((r, FFT_NB, FFT_CT), lambda i, c: (0, i, col_block * (d // FFT_CT) + c)),
                  pl.BlockSpec((FFT_NB, rows_out, r), lambda i, c: (i, 0, 0))],
        out_specs=pl.BlockSpec((rows_out, FFT_NB, FFT_CT), lambda i, c: (0, i, c)),
        out_shape=jax.ShapeDtypeStruct((rows_out, FFT_N2, d), F32),
        compiler_params=_cparams(("arbitrary", "arbitrary")),
        name="fft_first",
    )(x3, g)


def _fft_mid_kernel(with_filter, t_ref, g_ref, *rest):
    if with_filter:
        h_ref, gi_ref, o_ref = rest
    else:
        (o_ref,) = rest
    half = FFT_N2
    for kk in range(FFT_KB):
        x = jnp.concatenate([t_ref[0, kk], t_ref[1, kk]], axis=0).astype(BF16)
        y = jnp.dot(g_ref[...], x, preferred_element_type=F32)
        if with_filter:
            yr, yi = y[:half], y[half:]
            hr, hi = h_ref[0, kk], h_ref[1, kk]
            z = jnp.concatenate([yr * hr - yi * hi, yr * hi + yi * hr], axis=0).astype(BF16)
            y = jnp.dot(gi_ref[...], z, preferred_element_type=F32)
        o_ref[0, kk] = y[:half]
        o_ref[1, kk] = y[half:]


def _fft_mid(t, g_mid, h=None, g_mid_inv=None):
    rows, _, d = t.shape
    n1 = rows // 2
    t4 = t.reshape(2, n1, FFT_N2, d)
    blk = pl.BlockSpec((2, FFT_KB, FFT_N2, d), lambda i: (0, i, 0, 0))
    in_specs = [blk, _const_spec(g_mid.shape)]
    args = [t4, g_mid]
    if h is not None:
        in_specs += [blk, _const_spec(g_mid_inv.shape)]
        args += [h.reshape(2, n1, FFT_N2, d), g_mid_inv]
    out = pl.pallas_call(
        functools.partial(_fft_mid_kernel, h is not None),
        grid=(n1 // FFT_KB,),
        in_specs=in_specs,
        out_specs=blk,
        out_shape=jax.ShapeDtypeStruct((2, n1, FFT_N2, d), F32),
        compiler_params=_cparams(("arbitrary",)),
        name="fft_mid",
    )(*args)
    return out.reshape(rows, FFT_N2, d)


def _fft_last_kernel(b_ref, g_ref, z_ref, gate_ref, bias_ref, o_ref):
    bias = bias_ref[...]
    for j in range(FFT_NB):
        y = jnp.dot(g_ref[j], b_ref[:, j, :].astype(BF16), preferred_element_type=F32)
        o_ref[:, j, :] = gate_ref[:, j, :] * (y + z_ref[:, j, :] * bias)


def _fft_last(b, g, z3, z_col, gate3, gate_col, bias):
    rows_in, _, d = b.shape
    r = g.shape[1]
    nc = d // FFT_CT
    return pl.pallas_call(
        _fft_last_kernel,
        grid=(FFT_N2 // FFT_NB, nc),
        in_specs=[pl.BlockSpec((rows_in, FFT_NB, FFT_CT), lambda i, c: (0, i, c)),
                  pl.BlockSpec((FFT_NB, r, rows_in), lambda i, c: (i, 0, 0)),
                  pl.BlockSpec((r, FFT_NB, FFT_CT), lambda i, c: (0, i, z_col * nc + c)),
                  pl.BlockSpec((r, FFT_NB, FFT_CT), lambda i, c: (0, i, gate_col * nc + c)),
                  pl.BlockSpec((1, FFT_CT), lambda i, c: (0, c))],
        out_specs=pl.BlockSpec((r, FFT_NB, FFT_CT), lambda i, c: (0, i, c)),
        out_shape=jax.ShapeDtypeStruct((r, FFT_N2, d), F32),
        compiler_params=_cparams(("arbitrary", "arbitrary")),
        name="fft_last",
    )(b, g, z3, gate3, bias.reshape(1, d))


def _hyena_long_convs(u, taps, f_bias, seq_len):
    n, d3 = u.shape
    d = d3 // 3
    g_first, g_last, g_taps, g_mid, g_mid_inv = _dft_tables(seq_len)
    n1 = 2 * seq_len // FFT_N2
    u3 = u.reshape(n1, FFT_N2, d3)
    z3, z_col = u3, 0
    for o in range(2):
        spec = _fft_mid(_fft_first(taps[o].reshape(n1, FFT_N2, d), 0, d, g_taps), g_mid)
        a = _fft_first(z3, z_col, d, g_first)
        bq = _fft_mid(a, g_mid, spec, g_mid_inv)
        z3 = _fft_last(bq, g_last, z3, z_col, u3, 1 + o, f_bias[o])
        z_col = 0
    return z3.reshape(n, d)


def _small_conv_kernel(u_v_ref, u_x1_ref, u_x2_ref, taps_ref, gf_ref, gt_ref, gi_ref, bias_ref, o_ref):
    nf = gf_ref.shape[0] // 2
    z = u_v_ref[...]
    gates = (u_x1_ref, u_x2_ref)
    for o in range(2):
        spec = jnp.dot(gt_ref[...], taps_ref[o].astype(BF16), preferred_element_type=F32)
        zq = jnp.dot(gf_ref[...], z.astype(BF16), preferred_element_type=F32)
        zr, zi = zq[:nf], zq[nf:]
        hr, hi = spec[:nf], spec[nf:]
        prod = jnp.concatenate([zr * hr - zi * hi, zr * hi + zi * hr], axis=0).astype(BF16)
        y = jnp.dot(gi_ref[...], prod, preferred_element_type=F32)
        z = gates[o][...] * (y + z * bias_ref[o:o + 1, :])
    o_ref[...] = z


def _small_dft_tables(seq_len):
    n_fft = 2 * seq_len
    k = jnp.arange(n_fft, dtype=jnp.int32)
    ang = (2.0 * math.pi / n_fft) * ((k[:, None] * k[None, :]) % n_fft).astype(F32)
    c, s = jnp.cos(ang), jnp.sin(ang)
    cl, sl = c[:, :seq_len], s[:, :seq_len]
    g_fwd = jnp.concatenate([jnp.concatenate([cl, sl], axis=1),
                             jnp.concatenate([-sl, cl], axis=1)], axis=0)
    g_taps = jnp.concatenate([c, -s], axis=0)
    g_inv = g_fwd.T * (1.0 / n_fft)
    return g_fwd.astype(BF16), g_taps.astype(BF16), g_inv.astype(BF16)


def _hyena_small_convs(u, taps, f_bias, seq_len):
    n, d3 = u.shape
    d = d3 // 3
    ct = 256
    nc = d // ct
    g_fwd, g_taps, g_inv = _small_dft_tables(seq_len)
    return pl.pallas_call(
        _small_conv_kernel,
        grid=(nc,),
        in_specs=[pl.BlockSpec((n, ct), lambda c: (0, c)),
                  pl.BlockSpec((n, ct), lambda c: (0, nc + c)),
                  pl.BlockSpec((n, ct), lambda c: (0, 2 * nc + c)),
                  pl.BlockSpec((2, 2 * seq_len, ct), lambda c: (0, 0, c)),
                  _const_spec(g_fwd.shape), _const_spec(g_taps.shape), _const_spec(g_inv.shape),
                  pl.BlockSpec((2, ct), lambda c: (0, c))],
        out_specs=pl.BlockSpec((n, ct), lambda c: (0, c)),
        out_shape=jax.ShapeDtypeStruct((n, d), F32),
        compiler_params=_cparams(("arbitrary",)),
        name="small_conv",
    )(u, u, u, taps, g_fwd, g_taps, g_inv, f_bias)


def kernel(x, c, ctx, c_ctx, w_mod, b_mod, norm_w, ffn_w_gate_up, ffn_w_down, attn_w_qkv, attn_w_o,
           attn_q_norm, attn_k_norm, hy_w_in, hy_b_in, hy_conv_w, hy_conv_b, hy_f_w1, hy_f_b1,
           hy_f_w2, hy_f_b2, hy_f_w3, hy_f_b3, hy_f_wout, hy_f_freq, hy_f_bias, hy_w_out, hy_b_out):
    bsz, seq, d = x.shape
    ctx_len = ctx.shape[1]
    depth = w_mod.shape[0]
    assert bsz == 2, "the long convolution packs exactly two batches into one complex sequence"
    xs = x.reshape(bsz * seq, d)
    cs = ctx.reshape(bsz * ctx_len, d)

    cc = jnp.zeros((8, d), F32).at[:bsz].set(c).at[bsz].set(c_ctx)
    mod_all = _mod_vectors(cc, w_mod, b_mod).reshape(depth, 8, N_MOD, d)
    ctx_row = bsz

    x_tiles = seq // FFN_TM
    p_sum, p_bcast = _head_sum_matrices()
    cos_t, sin_t = _rope_tables(seq)
    zero_bias = jnp.zeros((d,), F32)

    for l in range(depth):
        mod = mod_all[l]
        is_attn = (l % 2) == 0
        ctx_out = l < depth - 1
        ctx_live = ctx_out or is_attn
        wgu0, wd0 = _prep_ffn_weights(ffn_w_gate_up[l, 0], ffn_w_down[l, 0])
        wgu1, wd1 = _prep_ffn_weights(ffn_w_gate_up[l, 1], ffn_w_down[l, 1])

        xs = _ffn(xs, mod, 0, norm_w[l, 0], wgu0, wd0, x_tiles, None)
        if ctx_live:
            cs = _ffn(cs, mod, 0, norm_w[l, 0], wgu0, wd0, 1, ctx_row)

        if is_attn:
            a = l // 2
            w_qkv = attn_w_qkv[a].astype(BF16)
            w_o = attn_w_o[a].astype(BF16)
            qn = (jnp.tile(attn_q_norm[a], N_HEADS) * (HEAD_DIM ** -0.5)).reshape(1, -1)
            kn = jnp.tile(attn_k_norm[a], N_KV_HEADS).reshape(1, -1)
            q_l, kt_l, v_l = _attn_pre(xs, mod, norm_w[l, 1], w_qkv, qn, kn, p_sum, p_bcast,
                                       cos_t, sin_t, seq, None, True)
            q_c, kt_c, v_c = _attn_pre(cs, mod, norm_w[l, 1], w_qkv, qn, kn, p_sum, p_bcast,
                                       cos_t, sin_t, ctx_len, ctx_row, False)
            o_l = _attention(q_l, kt_c, v_c, kt_l, v_l, seq)
            xs = _proj_residual(o_l, xs, mod, w_o, zero_bias, x_tiles, None)
            if ctx_out:
                o_c = _attention(q_c, kt_c, v_c, None, None, ctx_len)
                cs = _proj_residual(o_c, cs, mod, w_o, zero_bias, 1, ctx_row)
        else:
            j = l // 2
            w_in = hy_w_in[j].astype(BF16)
            w_out = hy_w_out[j].astype(BF16)
            fargs = (hy_f_w1[j], hy_f_b1[j], hy_f_w2[j], hy_f_b2[j], hy_f_w3[j], hy_f_b3[j],
                     hy_f_wout[j], hy_f_freq[j])
            u_l = _hyena_pre(xs, mod, norm_w[l, 1], w_in, hy_b_in[j], hy_conv_w[j], hy_conv_b[j], seq, None)
            y_l = _hyena_long_convs(u_l, _filter_taps(seq, *fargs), hy_f_bias[j], seq)
            xs_new = _proj_residual(y_l, xs, mod, w_out, hy_b_out[j], x_tiles, None)
            if ctx_out:
                u_c = _hyena_pre(cs, mod, norm_w[l, 1], w_in, hy_b_in[j], hy_conv_w[j], hy_conv_b[j],
                                 ctx_len, ctx_row)
                y_c = _hyena_small_convs(u_c, _filter_taps(ctx_len, *fargs), hy_f_bias[j], ctx_len)
                cs = _proj_residual(y_c, cs, mod, w_out, hy_b_out[j], 1, ctx_row)
            xs = xs_new

        xs = _ffn(xs, mod, 2, norm_w[l, 2], wgu1, wd1, x_tiles, None)
        if ctx_out:
            cs = _ffn(cs, mod, 2, norm_w[l, 2], wgu1, wd1, 1, ctx_row)
    return xs.reshape(bsz, seq, d)
```

```python
import functools
import math

import jax
import jax.numpy as jnp
import numpy as np
from jax import lax
from jax.experimental import pallas as pl
from jax.experimental.pallas import tpu as pltpu

F32 = jnp.float32
BF16 = jnp.bfloat16

N_MOD = 9
N_HEADS = 16
N_KV_HEADS = 4
HEAD_DIM = 64
KV_GROUP = N_HEADS // N_KV_HEADS
GRID_W = 64
ROPE_THETA = 10000.0
EPS = 1e-6
FILTER_BANDS = 16
DECAY_TARGET = 1e-2
FAST_DECAY_PCT = 0.3
SLOW_DECAY_PCT = 1.5

LANES = 128
HALO = 16
FFT_N2 = 128
VMEM_LIMIT = 56 * 1024 * 1024

FFN_TM = 512
PRE_TM = 256
FFN_FC = 256
ATT_TQ = 256
ATT_TK = 512
FFT_NB = 8
FFT_KB = 4
FFT_CT = 512


def _cparams(sem):
    return pltpu.CompilerParams(dimension_semantics=sem, vmem_limit_bytes=VMEM_LIMIT)


def _const_spec(shape):
    nd = len(shape)
    return pl.BlockSpec(shape, lambda *_: (0,) * nd, pipeline_mode=pl.Buffered(1))


def _modulated(s, g, shift, scale):
    ms = jnp.mean(s * s, axis=-1, keepdims=True)
    return (s * lax.rsqrt(ms + EPS) * g) * (1.0 + scale) + shift


def _mod_kernel(c_ref, w_ref, b_ref, o_ref):
    c = c_ref[...]
    a = (c * jax.nn.sigmoid(c)).astype(BF16)
    o_ref[0] = jnp.dot(a, w_ref[0].astype(BF16), preferred_element_type=F32) + b_ref[0]


def _mod_vectors(cc, w_mod, b_mod):
    depth, d, n = w_mod.shape
    tn = n // 6
    return pl.pallas_call(
        _mod_kernel,
        grid=(depth, n // tn),
        in_specs=[pl.BlockSpec((8, d), lambda l, j: (0, 0)),
                  pl.BlockSpec((1, d, tn), lambda l, j: (l, 0, j)),
                  pl.BlockSpec((1, 1, tn), lambda l, j: (l, 0, j))],
        out_specs=pl.BlockSpec((1, 8, tn), lambda l, j: (l, 0, j)),
        out_shape=jax.ShapeDtypeStruct((depth, 8, n), F32),
        compiler_params=_cparams(("arbitrary", "arbitrary")),
        name="mod_vectors",
    )(cc, w_mod, b_mod.reshape(depth, 1, n))


def _mod_blockspec(d, tiles_per_row, fixed_row):
    if fixed_row is None:
        return pl.BlockSpec((None, N_MOD, d), lambda i: (i // tiles_per_row, 0, 0))
    return pl.BlockSpec((None, N_MOD, d), lambda i: (fixed_row, 0, 0))


def _ffn_kernel(k, n_chunks, s_ref, mod_ref, g_ref, wgu_ref, wd_ref, o_ref, h_ref, acc_ref):
    s = s_ref[...]
    shift = mod_ref[3 * k:3 * k + 1, :]
    scale = mod_ref[3 * k + 1:3 * k + 2, :]
    gate = mod_ref[3 * k + 2:3 * k + 3, :]
    h_ref[...] = _modulated(s, g_ref[...], shift, scale).astype(BF16)
    acc_ref[...] = jnp.zeros_like(acc_ref)
    fc = wd_ref.shape[1]

    def body(c, carry):
        gu = jnp.dot(h_ref[...], wgu_ref[c], preferred_element_type=F32)
        g = gu[:, :fc]
        u = gu[:, fc:]
        a = (g * jax.nn.sigmoid(g) * u).astype(BF16)
        acc_ref[...] += jnp.dot(a, wd_ref[c], preferred_element_type=F32)
        return carry

    lax.fori_loop(0, n_chunks, body, 0)
    o_ref[...] = s + 0.5 * gate * acc_ref[...]


def _ffn(s, mod, k, g, wgu, wd, tiles_per_row, fixed_row):
    n, d = s.shape
    tm = min(FFN_TM, n)
    n_chunks, _, fc2 = wgu.shape
    return pl.pallas_call(
        functools.partial(_ffn_kernel, k, n_chunks),
        grid=(n // tm,),
        in_specs=[pl.BlockSpec((tm, d), lambda i: (i, 0)),
                  _mod_blockspec(d, tiles_per_row, fixed_row),
                  _const_spec((1, d)),
                  _const_spec(wgu.shape),
                  _const_spec(wd.shape)],
        out_specs=pl.BlockSpec((tm, d), lambda i: (i, 0)),
        out_shape=jax.ShapeDtypeStruct((n, d), F32),
        scratch_shapes=[pltpu.VMEM((tm, d), BF16), pltpu.VMEM((tm, d), F32)],
        compiler_params=_cparams(("arbitrary",)),
        name="ffn",
    )(s, mod, g.reshape(1, d), wgu, wd)


def _prep_ffn_weights(w_gate_up, w_down):
    d, f2 = w_gate_up.shape
    f = f2 // 2
    nc = f // FFN_FC
    wg = w_gate_up[:, :f].reshape(d, nc, FFN_FC)
    wu = w_gate_up[:, f:].reshape(d, nc, FFN_FC)
    wgu = jnp.concatenate([wg, wu], axis=-1).transpose(1, 0, 2).astype(BF16)
    wd = w_down.reshape(nc, FFN_FC, d).astype(BF16)
    return wgu, wd


def _proj_kernel(a_ref, s_ref, mod_ref, w_ref, b_ref, o_ref):
    gate = mod_ref[5:6, :]
    y = jnp.dot(a_ref[...].astype(BF16), w_ref[...], preferred_element_type=F32) + b_ref[...]
    o_ref[...] = s_ref[...] + gate * y


def _proj_residual(a, s, mod, w, b, tiles_per_row, fixed_row):
    n, d = s.shape
    tm = min(FFN_TM, n)
    return pl.pallas_call(
        _proj_kernel,
        grid=(n // tm,),
        in_specs=[pl.BlockSpec((tm, a.shape[1]), lambda i: (i, 0)),
                  pl.BlockSpec((tm, d), lambda i: (i, 0)),
                  _mod_blockspec(d, tiles_per_row, fixed_row),
                  _const_spec(w.shape),
                  _const_spec((1, d))],
        out_specs=pl.BlockSpec((tm, d), lambda i: (i, 0)),
        out_shape=jax.ShapeDtypeStruct((n, d), F32),
        compiler_params=_cparams(("arbitrary",)),
        name="proj_residual",
    )(a, s, mod, w, b.reshape(1, d))


def _head_norm(x, p, pt, w):
    sq = x * x
    hi = sq.astype(BF16)
    lo = (sq - hi.astype(F32)).astype(BF16)
    ss = jnp.dot(hi, p, preferred_element_type=F32) + jnp.dot(lo, p, preferred_element_type=F32)
    r = lax.rsqrt(ss * (1.0 / HEAD_DIM) + EPS)
    rh = r.astype(BF16)
    rl = (r - rh.astype(F32)).astype(BF16)
    rb = jnp.dot(rh, pt, preferred_element_type=F32) + jnp.dot(rl, pt, preferred_element_type=F32)
    return x * rb * w


def _rope_block(xb, cos, sin):
    lane = lax.broadcasted_iota(jnp.int32, xb.shape, 1)
    fwd = pltpu.roll(xb, 16, axis=1)
    bwd = pltpu.roll(xb, LANES - 16, axis=1)
    partner = jnp.where((lane % 32) < 16, bwd, fwd)
    return xb * cos + partner * sin


def _attn_pre_kernel(rope, s_ref, mod_ref, g_ref, w_ref, qn_ref, kn_ref, p_ref, pt_ref,
                     cos_ref, sin_ref, q_ref, kt_ref, v_ref):
    s = s_ref[...]
    tm = s.shape[0]
    nq = N_HEADS * HEAD_DIM
    nk = N_KV_HEADS * HEAD_DIM
    h = _modulated(s, g_ref[...], mod_ref[3:4, :], mod_ref[4:5, :]).astype(BF16)
    qkv = jnp.dot(h, w_ref[...], preferred_element_type=F32)
    q = _head_norm(qkv[:, :nq], p_ref[...], pt_ref[...], qn_ref[...])
    k = _head_norm(qkv[:, nq:nq + nk], p_ref[:nk, :], pt_ref[:, :nk], kn_ref[...])
    v = qkv[:, nq + nk:]
    cos = cos_ref[...]
    sin = sin_ref[...]
    for j in range(nq // LANES):
        qb = q[:, j * LANES:(j + 1) * LANES]
        if rope:
            qb = _rope_block(qb, cos, sin)
        q_ref[:, j * LANES:(j + 1) * LANES] = qb.astype(BF16)
    kblocks = []
    for j in range(nk // LANES):
        kb = k[:, j * LANES:(j + 1) * LANES]
        if rope:
            kb = _rope_block(kb, cos, sin)
        kblocks.append(kb)
    kt = jnp.concatenate(kblocks, axis=1).T.astype(BF16)
    for hh in range(N_KV_HEADS):
        kt_ref[hh] = kt[hh * HEAD_DIM:(hh + 1) * HEAD_DIM, :]
    lane = lax.broadcasted_iota(jnp.int32, (tm, LANES), 1)
    ones_col = jnp.where(lane == HEAD_DIM, 1.0, 0.0)
    for hh in range(N_KV_HEADS):
        vb = v[:, (hh // 2) * LANES:(hh // 2 + 1) * LANES]
        if hh % 2 == 1:
            vb = pltpu.roll(vb, HEAD_DIM, axis=1)
        v_ref[hh] = jnp.where(lane < HEAD_DIM, vb, ones_col).astype(BF16)


def _attn_pre(s, mod, g, w_qkv, qn, kn, p, pt, cos, sin, seq_len, fixed_row, rope):
    n, d = s.shape
    tm = PRE_TM
    tps = seq_len // tm
    n_seq = n // seq_len
    nq = N_HEADS * HEAD_DIM
    return pl.pallas_call(
        functools.partial(_attn_pre_kernel, rope),
        grid=(n // tm,),
        in_specs=[pl.BlockSpec((tm, d), lambda i: (i, 0)),
                  _mod_blockspec(d, tps, fixed_row),
                  _const_spec((1, d)),
                  _const_spec(w_qkv.shape),
                  _const_spec((1, nq)),
                  _const_spec((1, N_KV_HEADS * HEAD_DIM)),
                  _const_spec(p.shape),
                  _const_spec(pt.shape),
                  pl.BlockSpec((tm, LANES), lambda i: (i % tps, 0)),
                  pl.BlockSpec((tm, LANES), lambda i: (i % tps, 0))],
        out_specs=[pl.BlockSpec((tm, nq), lambda i: (i, 0)),
                   pl.BlockSpec((None, N_KV_HEADS, HEAD_DIM, tm), lambda i: (i // tps, 0, 0, i % tps)),
                   pl.BlockSpec((None, N_KV_HEADS, tm, LANES), lambda i: (i // tps, 0, i % tps, 0))],
        out_shape=[jax.ShapeDtypeStruct((n, nq), BF16),
                   jax.ShapeDtypeStruct((n_seq, N_KV_HEADS, HEAD_DIM, seq_len), BF16),
                   jax.ShapeDtypeStruct((n_seq, N_KV_HEADS, seq_len, LANES), BF16)],
        compiler_params=_cparams(("arbitrary",)),
        name="attn_pre",
    )(s, mod, g.reshape(1, d), w_qkv, qn, kn, p, pt, cos, sin)


def _rope_tables(seq_len):
    half = HEAD_DIM // 4
    t = np.arange(seq_len)
    pos = np.stack([t // GRID_W, t % GRID_W], axis=1).astype(np.float32)
    freqs = (ROPE_THETA ** (-np.arange(half, dtype=np.float32) / half)).astype(np.float32)
    lane = np.arange(LANES)
    axis = (lane % HEAD_DIM) // (HEAD_DIM // 2)
    e = lane % (HEAD_DIM // 2)
    ang = pos[:, axis] * freqs[e % half][None, :]
    sign = np.where(e < half, -1.0, 1.0).astype(np.float32)
    return jnp.asarray(np.cos(ang), F32), jnp.asarray(np.sin(ang) * sign[None, :], F32)


def _head_sum_matrices():
    lane = np.arange(N_HEADS * HEAD_DIM)
    p = (lane[:, None] // HEAD_DIM == np.arange(LANES)[None, :]).astype(np.float32)
    return jnp.asarray(p, BF16), jnp.asarray(p.T, BF16)


def _attn_kernel(n_lat_chunks, tk, q_ref, ktc_ref, vc_ref, *rest):
    if n_lat_chunks:
        ktl_ref, vl_ref, o_ref = rest
    else:
        (o_ref,) = rest
    q = q_ref[...]
    tq = q.shape[0]
    q4 = jnp.concatenate([q[:, g * HEAD_DIM:(g + 1) * HEAD_DIM] for g in range(KV_GROUP)], axis=0)

    def step(kt, v, m, acc):
        s = jnp.dot(q4, kt, preferred_element_type=F32)
        m_new = jnp.maximum(m, jnp.max(s, axis=-1, keepdims=True))
        p = jnp.exp2(s - m_new)
        alpha = jnp.exp2(m - m_new)
        acc = alpha * acc + jnp.dot(p.astype(BF16), v, preferred_element_type=F32)
        return m_new, acc

    m0 = jnp.full((KV_GROUP * tq, 1), -1e30, F32)
    acc0 = jnp.zeros((KV_GROUP * tq, LANES), F32)
    m, acc = step(ktc_ref[...], vc_ref[...], m0, acc0)
    if n_lat_chunks:
        def body(c, carry):
            off = pl.multiple_of(c * tk, tk)
            return step(ktl_ref[:, pl.ds(off, tk)], vl_ref[pl.ds(off, tk), :], *carry)
        m, acc = lax.fori_loop(0, n_lat_chunks, body, (m, acc), unroll=8)
    o = acc[:, :HEAD_DIM] / acc[:, HEAD_DIM:HEAD_DIM + 1]
    o_ref[...] = jnp.concatenate([o[g * tq:(g + 1) * tq, :] for g in range(KV_GROUP)], axis=1).astype(BF16)


def _attention(q, ktc, vc, ktl, vl, lq):
    n, nq = q.shape
    b = n // lq
    tq = ATT_TQ
    nqt = lq // tq
    gw = KV_GROUP * HEAD_DIM
    lc = ktc.shape[-1]
    in_specs = [pl.BlockSpec((tq, gw), lambda bb, j, i: (bb * nqt + i, j)),
                pl.BlockSpec((None, None, HEAD_DIM, lc), lambda bb, j, i: (bb, j, 0, 0)),
                pl.BlockSpec((None, None, lc, LANES), lambda bb, j, i: (bb, j, 0, 0))]
    args = [q, ktc, vc]
    n_lat = 0
    if ktl is not None:
        ll = ktl.shape[-1]
        n_lat = ll // ATT_TK
        in_specs += [pl.BlockSpec((None, None, HEAD_DIM, ll), lambda bb, j, i: (bb, j, 0, 0)),
                     pl.BlockSpec((None, None, ll, LANES), lambda bb, j, i: (bb, j, 0, 0))]
        args += [ktl, vl]
    return pl.pallas_call(
        functools.partial(_attn_kernel, n_lat, ATT_TK),
        grid=(b, N_KV_HEADS, nqt),
        in_specs=in_specs,
        out_specs=pl.BlockSpec((tq, gw), lambda bb, j, i: (bb * nqt + i, j)),
        out_shape=jax.ShapeDtypeStruct((n, nq), BF16),
        compiler_params=_cparams(("arbitrary", "arbitrary", "arbitrary")),
        name="attention",
    )(*args)


def _hyena_pre_kernel(tps, n_col, s_ref, prev_ref, next_ref, mod_ref, g_ref, w_ref, b_ref,
                      cw_ref, cb_ref, o_ref, h_ref, pre_ref):
    i = pl.program_id(0)
    tm = s_ref.shape[0]
    hb = HALO
    g = g_ref[...]
    shift = mod_ref[3:4, :]
    scale = mod_ref[4:5, :]
    h_ref[0:hb, :] = _modulated(prev_ref[...], g, shift, scale).astype(BF16)
    h_ref[hb:hb + tm, :] = _modulated(s_ref[...], g, shift, scale).astype(BF16)
    h_ref[hb + tm:, :] = _modulated(next_ref[...], g, shift, scale).astype(BF16)
    row = lax.broadcasted_iota(jnp.int32, (tm, 1), 0)
    drop_up = jnp.logical_and(row == 0, i % tps == 0)
    drop_dn = jnp.logical_and(row == tm - 1, i % tps == tps - 1)
    ct = w_ref.shape[1] // n_col
    for c in range(n_col):
        cols = slice(c * ct, (c + 1) * ct)
        pre_ref[...] = jnp.dot(h_ref[...], w_ref[:, cols], preferred_element_type=F32) + b_ref[:, cols]
        up = jnp.where(drop_up, 0.0, pre_ref[hb - 1:hb - 1 + tm, :])
        mid = pre_ref[hb:hb + tm, :]
        dn = jnp.where(drop_dn, 0.0, pre_ref[hb + 1:hb + 1 + tm, :])
        o_ref[:, cols] = (up * cw_ref[0:1, cols] + mid * cw_ref[1:2, cols]
                          + dn * cw_ref[2:3, cols] + cb_ref[:, cols])


def _hyena_pre(s, mod, g, w_in, b_in, conv_w, conv_b, seq_len, fixed_row):
    n, d = s.shape
    tm = PRE_TM
    tps = seq_len // tm
    n3 = w_in.shape[1]
    hb = HALO
    nblk = n // hb
    n_col = 6
    return pl.pallas_call(
        functools.partial(_hyena_pre_kernel, tps, n_col),
        grid=(n // tm,),
        in_specs=[pl.BlockSpec((tm, d), lambda i: (i, 0)),
                  pl.BlockSpec((hb, d), lambda i: (jnp.maximum(i * (tm // hb) - 1, 0), 0)),
                  pl.BlockSpec((hb, d), lambda i: (jnp.minimum((i + 1) * (tm // hb), nblk - 1), 0)),
                  _mod_blockspec(d, tps, fixed_row),
                  _const_spec((1, d)),
                  _const_spec(w_in.shape),
                  _const_spec((1, n3)),
                  _const_spec((3, n3)),
                  _const_spec((1, n3))],
        out_specs=pl.BlockSpec((tm, n3), lambda i: (i, 0)),
        out_shape=jax.ShapeDtypeStruct((n, n3), F32),
        scratch_shapes=[pltpu.VMEM((tm + 2 * hb, d), BF16),
                        pltpu.VMEM((tm + 2 * hb, n3 // n_col), F32)],
        compiler_params=_cparams(("arbitrary",)),
        name="hyena_pre",
    )(s, s, s, mod, g.reshape(1, d), w_in, b_in.reshape(1, n3), conv_w, conv_b.reshape(1, n3))


def _filter_kernel(n_feat, d, feat_ref, w1_ref, b1_ref, w2_ref, b2_ref, w3_ref, b3_ref,
                   wo_ref, a_ref, delta_ref, o_ref):
    hp = lax.Precision.HIGHEST
    feats = feat_ref[...]
    a = a_ref[...]
    hid = jnp.sin(a * (jnp.dot(feats, w1_ref[...], precision=hp, preferred_element_type=F32) + b1_ref[...]))
    hid = jnp.sin(a * (jnp.dot(hid, w2_ref[...], precision=hp, preferred_element_type=F32) + b2_ref[...]))
    hid = jnp.sin(a * (jnp.dot(hid, w3_ref[...], precision=hp, preferred_element_type=F32) + b3_ref[...]))
    t = feats[:, 0:1]
    use_fwd = feats[:, n_feat:n_feat + 1]
    use_bwd = feats[:, n_feat + 1:n_feat + 2]
    decay = jnp.exp(-t * delta_ref[...])
    for o in range(2):
        fwd = jnp.dot(hid, wo_ref[:, (2 * o) * d:(2 * o + 1) * d], precision=hp, preferred_element_type=F32)
        bwd = jnp.dot(hid, wo_ref[:, (2 * o + 1) * d:(2 * o + 2) * d], precision=hp, preferred_element_type=F32)
        o_ref[o] = decay * (use_fwd * fwd + use_bwd * bwd)


def _filter_features(seq_len):
    l = seq_len
    n_feat = 1 + 2 * FILTER_BANDS
    n = np.arange(2 * l)
    pos = np.where(n < l, n, 2 * l - n)
    pos = np.where(n == l, 0, pos)
    t = np.linspace(0.0, 1.0, l, dtype=np.float32)[pos]
    w = (2.0 * math.pi * pos.astype(np.float32) / l).astype(np.float32)
    bands = np.linspace(1e-4, FILTER_BANDS - 1, FILTER_BANDS, dtype=np.float32)
    bw = (bands[None, :] * w[:, None]).astype(np.float32)
    feats = np.zeros((2 * l, LANES), np.float32)
    feats[:, 0] = t
    feats[:, 1:1 + FILTER_BANDS] = np.cos(bw)
    feats[:, 1 + FILTER_BANDS:n_feat] = -np.sin(bw)
    feats[:, n_feat] = (n < l)
    feats[:, n_feat + 1] = np.logical_or(n > l, n == 0)
    return jnp.asarray(feats, F32), n_feat


def _filter_taps(seq_len, f_w1, f_b1, f_w2, f_b2, f_w3, f_b3, f_wout, f_freq):
    feats, n_feat = _filter_features(seq_len)
    fh = f_w1.shape[1]
    d = f_wout.shape[1] // 4
    w1p = jnp.zeros((LANES, fh), F32).at[:n_feat].set(f_w1)
    min_decay = math.log(DECAY_TARGET) / SLOW_DECAY_PCT
    max_decay = math.log(DECAY_TARGET) / FAST_DECAY_PCT
    deltas = jnp.abs(jnp.linspace(min_decay, max_decay, d, dtype=F32)).reshape(1, d)
    n = 2 * seq_len
    tm = min(512, n)
    return pl.pallas_call(
        functools.partial(_filter_kernel, n_feat, d),
        grid=(n // tm,),
        in_specs=[pl.BlockSpec((tm, LANES), lambda i: (i, 0)),
                  _const_spec((LANES, fh)), _const_spec((1, fh)),
                  _const_spec((fh, fh)), _const_spec((1, fh)),
                  _const_spec((fh, fh)), _const_spec((1, fh)),
                  _const_spec(f_wout.shape), _const_spec((1, fh)), _const_spec((1, d))],
        out_specs=pl.BlockSpec((2, tm, d), lambda i: (0, i, 0)),
        out_shape=jax.ShapeDtypeStruct((2, n, d), F32),
        compiler_params=_cparams(("arbitrary",)),
        name="filter_taps",
    )(feats, w1p, f_b1.reshape(1, fh), f_w2, f_b2.reshape(1, fh), f_w3, f_b3.reshape(1, fh),
      f_wout, f_freq.reshape(1, fh), deltas)


def _dft_tables(seq_len):
    n_fft = 2 * seq_len
    n1 = n_fft // FFT_N2
    k1 = jnp.arange(n1, dtype=jnp.int32)[None, :, None]
    n2 = jnp.arange(FFT_N2, dtype=jnp.int32)[:, None, None]
    nn1 = jnp.arange(n1, dtype=jnp.int32)[None, None, :]
    ang = (2.0 * math.pi / n_fft) * ((k1 * (FFT_N2 * nn1 + n2)) % n_fft).astype(F32)
    c = jnp.cos(ang)
    s = jnp.sin(ang)
    h = n1 // 2
    g_first = jnp.concatenate([jnp.concatenate([c[:, :, :h], s[:, :, :h]], axis=2),
                               jnp.concatenate([-s[:, :, :h], c[:, :, :h]], axis=2)], axis=1)
    g_last = jnp.swapaxes(g_first, 1, 2) * (1.0 / n_fft)
    g_taps = jnp.concatenate([c, -s], axis=1)
    k2 = jnp.arange(FFT_N2, dtype=jnp.int32)
    ang2 = (2.0 * math.pi / FFT_N2) * ((k2[:, None] * k2[None, :]) % FFT_N2).astype(F32)
    c2, s2 = jnp.cos(ang2), jnp.sin(ang2)
    g_mid = jnp.concatenate([jnp.concatenate([c2, s2], axis=1),
                             jnp.concatenate([-s2, c2], axis=1)], axis=0)
    return (g_first.astype(BF16), g_last.astype(BF16), g_taps.astype(BF16),
            g_mid.astype(BF16), g_mid.T.astype(BF16))


def _fft_first_kernel(x_ref, g_ref, o_ref):
    for j in range(FFT_NB):
        o_ref[:, j, :] = jnp.dot(g_ref[j], x_ref[:, j, :].astype(BF16), preferred_element_type=F32)


def _fft_first(x3, col_block, d, g):
    r = x3.shape[0]
    rows_out = g.shape[1]
    return pl.pallas_call(
        _fft_first_kernel,
        grid=(FFT_N2 // FFT_NB, d // FFT_CT),
        in_specs=[pl.BlockSpec((r, FFT_NB, FFT_CT), lambda i, c: (0, i, col_block * (d // FFT_CT) + c)),
                  pl.BlockSpec((FFT_NB, rows_out, r), lambda i, c: (i, 0, 0))],
        out_specs=pl.BlockSpec((rows_out, FFT_NB, FFT_CT), lambda i, c: (0, i, c)),
        out_shape=jax.ShapeDtypeStruct((rows_out, FFT_N2, d), F32),
        compiler_params=_cparams(("arbitrary", "arbitrary")),
        name="fft_first",
    )(x3, g)


def _fft_mid_kernel(with_filter, t_ref, g_ref, *rest):
    if with_filter:
        h_ref, gi_ref, o_ref = rest
    else:
        (o_ref,) = rest
    half = FFT_N2
    for kk in range(FFT_KB):
        x = jnp.concatenate([t_ref[0, kk], t_ref[1, kk]], axis=0).astype(BF16)
        y = jnp.dot(g_ref[...], x, preferred_element_type=F32)
        if with_filter:
            yr, yi = y[:half], y[half:]
            hr, hi = h_ref[0, kk], h_ref[1, kk]
            z = jnp.concatenate([yr * hr - yi * hi, yr * hi + yi * hr], axis=0).astype(BF16)
            y = jnp.dot(gi_ref[...], z, preferred_element_type=F32)
        o_ref[0, kk] = y[:half]
        o_ref[1, kk] = y[half:]


def _fft_mid(t, g_mid, h=None, g_mid_inv=None):
    rows, _, d = t.shape
    n1 = rows // 2
    t4 = t.reshape(2, n1, FFT_N2, d)
    blk = pl.BlockSpec((2, FFT_KB, FFT_N2, d), lambda i: (0, i, 0, 0))
    in_specs = [blk, _const_spec(g_mid.shape)]
    args = [t4, g_mid]
    if h is not None:
        in_specs += [blk, _const_spec(g_mid_inv.shape)]
        args += [h.reshape(2, n1, FFT_N2, d), g_mid_inv]
    out = pl.pallas_call(
        functools.partial(_fft_mid_kernel, h is not None),
        grid=(n1 // FFT_KB,),
        in_specs=in_specs,
        out_specs=blk,
        out_shape=jax.ShapeDtypeStruct((2, n1, FFT_N2, d), F32),
        compiler_params=_cparams(("arbitrary",)),
        name="fft_mid",
    )(*args)
    return out.reshape(rows, FFT_N2, d)


def _fft_last_kernel(b_ref, g_ref, z_ref, gate_ref, bias_ref, o_ref):
    bias = bias_ref[...]
    for j in range(FFT_NB):
        y = jnp.dot(g_ref[j], b_ref[:, j, :].astype(BF16), preferred_element_type=F32)
        o_ref[:, j, :] = gate_ref[:, j, :] * (y + z_ref[:, j, :] * bias)


def _fft_last(b, g, z3, z_col, gate3, gate_col, bias):
    rows_in, _, d = b.shape
    r = g.shape[1]
    nc = d // FFT_CT
    return pl.pallas_call(
        _fft_last_kernel,
        grid=(FFT_N2 // FFT_NB, nc),
        in_specs=[pl.BlockSpec((rows_in, FFT_NB, FFT_CT), lambda i, c: (0, i, c)),
                  pl.BlockSpec((FFT_NB, r, rows_in), lambda i, c: (i, 0, 0)),
                  pl.BlockSpec((r, FFT_NB, FFT_CT), lambda i, c: (0, i, z_col * nc + c)),
                  pl.BlockSpec((r, FFT_NB, FFT_CT), lambda i, c: (0, i, gate_col * nc + c)),
                  pl.BlockSpec((1, FFT_CT), lambda i, c: (0, c))],
        out_specs=pl.BlockSpec((r, FFT_NB, FFT_CT), lambda i, c: (0, i, c)),
        out_shape=jax.ShapeDtypeStruct((r, FFT_N2, d), F32),
        compiler_params=_cparams(("arbitrary", "arbitrary")),
        name="fft_last",
    )(b, g, z3, gate3, bias.reshape(1, d))


def _hyena_long_convs(u, taps, f_bias, seq_len):
    n, d3 = u.shape
    d = d3 // 3
    g_first, g_last, g_taps, g_mid, g_mid_inv = _dft_tables(seq_len)
    n1 = 2 * seq_len // FFT_N2
    u3 = u.reshape(n1, FFT_N2, d3)
    z3, z_col = u3, 0
    for o in range(2):
        spec = _fft_mid(_fft_first(taps[o].reshape(n1, FFT_N2, d), 0, d, g_taps), g_mid)
        a = _fft_first(z3, z_col, d, g_first)
        bq = _fft_mid(a, g_mid, spec, g_mid_inv)
        z3 = _fft_last(bq, g_last, z3, z_col, u3, 1 + o, f_bias[o])
        z_col = 0
    return z3.reshape(n, d)


def _small_conv_kernel(u_v_ref, u_x1_ref, u_x2_ref, taps_ref, gf_ref, gt_ref, gi_ref, bias_ref, o_ref):
    nf = gf_ref.shape[0] // 2
    z = u_v_ref[...]
    gates = (u_x1_ref, u_x2_ref)
    for o in range(2):
        spec = jnp.dot(gt_ref[...], taps_ref[o].astype(BF16), preferred_element_type=F32)
        zq = jnp.dot(gf_ref[...], z.astype(BF16), preferred_element_type=F32)
        zr, zi = zq[:nf], zq[nf:]
        hr, hi = spec[:nf], spec[nf:]
        prod = jnp.concatenate([zr * hr - zi * hi, zr * hi + zi * hr], axis=0).astype(BF16)
        y = jnp.dot(gi_ref[...], prod, preferred_element_type=F32)
        z = gates[o][...] * (y + z * bias_ref[o:o + 1, :])
    o_ref[...] = z


def _small_dft_tables(seq_len):
    n_fft = 2 * seq_len
    k = jnp.arange(n_fft, dtype=jnp.int32)
    ang = (2.0 * math.pi / n_fft) * ((k[:, None] * k[None, :]) % n_fft).astype(F32)
    c, s = jnp.cos(ang), jnp.sin(ang)
    cl, sl = c[:, :seq_len], s[:, :seq_len]
    g_fwd = jnp.concatenate([jnp.concatenate([cl, sl], axis=1),
                             jnp.concatenate([-sl, cl], axis=1)], axis=0)
    g_taps = jnp.concatenate([c, -s], axis=0)
    g_inv = g_fwd.T * (1.0 / n_fft)
    return g_fwd.astype(BF16), g_taps.astype(BF16), g_inv.astype(BF16)


def _hyena_small_convs(u, taps, f_bias, seq_len):
    n, d3 = u.shape
    d = d3 // 3
    ct = 256
    nc = d // ct
    g_fwd, g_taps, g_inv = _small_dft_tables(seq_len)
    return pl.pallas_call(
        _small_conv_kernel,
        grid=(nc,),
        in_specs=[pl.BlockSpec((n, ct), lambda c: (0, c)),
                  pl.BlockSpec((n, ct), lambda c: (0, nc + c)),
                  pl.BlockSpec((n, ct), lambda c: (0, 2 * nc + c)),
                  pl.BlockSpec((2, 2 * seq_len, ct), lambda c: (0, 0, c)),
                  _const_spec(g_fwd.shape), _const_spec(g_taps.shape), _const_spec(g_inv.shape),
                  pl.BlockSpec((2, ct), lambda c: (0, c))],
        out_specs=pl.BlockSpec((n, ct), lambda c: (0, c)),
        out_shape=jax.ShapeDtypeStruct((n, d), F32),
        compiler_params=_cparams(("arbitrary",)),
        name="small_conv",
    )(u, u, u, taps, g_fwd, g_taps, g_inv, f_bias)


def kernel(x, c, ctx, c_ctx, w_mod, b_mod, norm_w, ffn_w_gate_up, ffn_w_down, attn_w_qkv, attn_w_o,
           attn_q_norm, attn_k_norm, hy_w_in, hy_b_in, hy_conv_w, hy_conv_b, hy_f_w1, hy_f_b1,
           hy_f_w2, hy_f_b2, hy_f_w3, hy_f_b3, hy_f_wout, hy_f_freq, hy_f_bias, hy_w_out, hy_b_out):
    bsz, seq, d = x.shape
    ctx_len = ctx.shape[1]
    depth = w_mod.shape[0]
    assert bsz == 2, "the long convolution packs exactly two batches into one complex sequence"
    xs = x.reshape(bsz * seq, d)
    cs = ctx.reshape(bsz * ctx_len, d)

    cc = jnp.zeros((8, d), F32).at[:bsz].set(c).at[bsz].set(c_ctx)
    mod_all = _mod_vectors(cc, w_mod, b_mod).reshape(depth, 8, N_MOD, d)
    ctx_row = bsz

    x_tiles = seq // FFN_TM
    p_sum, p_bcast = _head_sum_matrices()
    cos_t, sin_t = _rope_tables(seq)
    zero_bias = jnp.zeros((d,), F32)

    for l in range(depth):
        mod = mod_all[l]
        is_attn = (l % 2) == 0
        ctx_out = l < depth - 1
        ctx_live = ctx_out or is_attn
        wgu0, wd0 = _prep_ffn_weights(ffn_w_gate_up[l, 0], ffn_w_down[l, 0])
        wgu1, wd1 = _prep_ffn_weights(ffn_w_gate_up[l, 1], ffn_w_down[l, 1])

        xs = _ffn(xs, mod, 0, norm_w[l, 0], wgu0, wd0, x_tiles, None)
        if ctx_live:
            cs = _ffn(cs, mod, 0, norm_w[l, 0], wgu0, wd0, 1, ctx_row)

        if is_attn:
            a = l // 2
            w_qkv = attn_w_qkv[a].astype(BF16)
            w_o = attn_w_o[a].astype(BF16)
            qn = (jnp.tile(attn_q_norm[a], N_HEADS) * (HEAD_DIM ** -0.5 * math.log2(math.e))).reshape(1, -1)
            kn = jnp.tile(attn_k_norm[a], N_KV_HEADS).reshape(1, -1)
            q_l, kt_l, v_l = _attn_pre(xs, mod, norm_w[l, 1], w_qkv, qn, kn, p_sum, p_bcast,
                                       cos_t, sin_t, seq, None, True)
            q_c, kt_c, v_c = _attn_pre(cs, mod, norm_w[l, 1], w_qkv, qn, kn, p_sum, p_bcast,
                                       cos_t, sin_t, ctx_len, ctx_row, False)
            o_l = _attention(q_l, kt_c, v_c, kt_l, v_l, seq)
            xs = _proj_residual(o_l, xs, mod, w_o, zero_bias, x_tiles, None)
            if ctx_out:
                o_c = _attention(q_c, kt_c, v_c, None, None, ctx_len)
                cs = _proj_residual(o_c, cs, mod, w_o, zero_bias, 1, ctx_row)
        else:
            j = l // 2
            w_in = hy_w_in[j].astype(BF16)
            w_out = hy_w_out[j].astype(BF16)
            fargs = (hy_f_w1[j], hy_f_b1[j], hy_f_w2[j], hy_f_b2[j], hy_f_w3[j], hy_f_b3[j],
                     hy_f_wout[j], hy_f_freq[j])
            u_l = _hyena_pre(xs, mod, norm_w[l, 1], w_in, hy_b_in[j], hy_conv_w[j], hy_conv_b[j], seq, None)
            y_l = _hyena_long_convs(u_l, _filter_taps(seq, *fargs), hy_f_bias[j], seq)
            xs_new = _proj_residual(y_l, xs, mod, w_out, hy_b_out[j], x_tiles, None)
            if ctx_out:
                u_c = _hyena_pre(cs, mod, norm_w[l, 1], w_in, hy_b_in[j], hy_conv_w[j], hy_conv_b[j],
                                 ctx_len, ctx_row)
                y_c = _hyena_small_convs(u_c, _filter_taps(ctx_len, *fargs), hy_f_bias[j], ctx_len)
                cs = _proj_residual(y_c, cs, mod, w_out, hy_b_out[j], 1, ctx_row)
            xs = xs_new

        xs = _ffn(xs, mod, 2, norm_w[l, 2], wgu1, wd1, x_tiles, None)
        if ctx_out:
            cs = _ffn(cs, mod, 2, norm_w[l, 2], wgu1, wd1, 1, ctx_row)
    return xs.reshape(bsz, seq, d)
```

```python
import functools
import math

import jax
import jax.numpy as jnp
import numpy as np
from jax import lax
from jax.experimental import pallas as pl
from jax.experimental.pallas import tpu as pltpu

F32 = jnp.float32
BF16 = jnp.bfloat16

N_MOD = 9
N_HEADS = 16
N_KV_HEADS = 4
HEAD_DIM = 64
KV_GROUP = N_HEADS // N_KV_HEADS
GRID_W = 64
ROPE_THETA = 10000.0
EPS = 1e-6
FILTER_BANDS = 16
DECAY_TARGET = 1e-2
FAST_DECAY_PCT = 0.3
SLOW_DECAY_PCT = 1.5

LANES = 128
HALO = 16
FFT_N2 = 128
VMEM_LIMIT = 56 * 1024 * 1024

FFN_TM = 1024
PROJ_TM = 512
PRE_TM = 256
FFN_FC = 256
ATT_TQ = 256
ATT_TK = 512
FFT_NB = 16
FFT_KB = 8
FFT_CT = 512
FFT_CT_LAST = 256


def _cparams(sem):
    return pltpu.CompilerParams(dimension_semantics=sem, vmem_limit_bytes=VMEM_LIMIT)


def _const_spec(shape):
    nd = len(shape)
    return pl.BlockSpec(shape, lambda *_: (0,) * nd, pipeline_mode=pl.Buffered(1))


def _modulated(s, g, shift, scale):
    ms = jnp.mean(s * s, axis=-1, keepdims=True)
    return (s * lax.rsqrt(ms + EPS) * g) * (1.0 + scale) + shift


def _mod_kernel(c_ref, w_ref, b_ref, o_ref):
    c = c_ref[...]
    a = (c * jax.nn.sigmoid(c)).astype(BF16)
    o_ref[0] = jnp.dot(a, w_ref[0].astype(BF16), preferred_element_type=F32) + b_ref[0]


def _mod_vectors(cc, w_mod, b_mod):
    depth, d, n = w_mod.shape
    tn = n // 6
    return pl.pallas_call(
        _mod_kernel,
        grid=(depth, n // tn),
        in_specs=[pl.BlockSpec((8, d), lambda l, j: (0, 0)),
                  pl.BlockSpec((1, d, tn), lambda l, j: (l, 0, j)),
                  pl.BlockSpec((1, 1, tn), lambda l, j: (l, 0, j))],
        out_specs=pl.BlockSpec((1, 8, tn), lambda l, j: (l, 0, j)),
        out_shape=jax.ShapeDtypeStruct((depth, 8, n), F32),
        compiler_params=_cparams(("arbitrary", "arbitrary")),
        name="mod_vectors",
    )(cc, w_mod, b_mod.reshape(depth, 1, n))


def _mod_blockspec(d, tiles_per_row, fixed_row):
    if fixed_row is None:
        return pl.BlockSpec((None, N_MOD, d), lambda i: (i // tiles_per_row, 0, 0))
    return pl.BlockSpec((None, N_MOD, d), lambda i: (fixed_row, 0, 0))


def _ffn_kernel(k, s_ref, mod_ref, g_ref, wgu_ref, wd_ref, o_ref, h_ref, acc_ref):
    s = s_ref[...]
    shift = mod_ref[3 * k:3 * k + 1, :]
    scale = mod_ref[3 * k + 1:3 * k + 2, :]
    gate = mod_ref[3 * k + 2:3 * k + 3, :]
    h_ref[...] = _modulated(s, g_ref[...], shift, scale).astype(BF16)
    f = wd_ref.shape[0]
    for c in range(f // FFN_FC):
        lo = c * FFN_FC
        g = jnp.dot(h_ref[...], wgu_ref[:, lo:lo + FFN_FC], preferred_element_type=F32)
        u = jnp.dot(h_ref[...], wgu_ref[:, f + lo:f + lo + FFN_FC], preferred_element_type=F32)
        a = (g * jax.nn.sigmoid(g) * u).astype(BF16)
        y = jnp.dot(a, wd_ref[lo:lo + FFN_FC, :], preferred_element_type=F32)
        if c == 0:
            acc_ref[...] = y
        else:
            acc_ref[...] += y
    o_ref[...] = s + 0.5 * gate * acc_ref[...]


def _ffn(s, mod, k, g, wgu, wd, rows_per_cond, fixed_row):
    n, d = s.shape
    tm = min(FFN_TM, n)
    tiles_per_row = rows_per_cond // tm
    return pl.pallas_call(
        functools.partial(_ffn_kernel, k),
        grid=(n // tm,),
        in_specs=[pl.BlockSpec((tm, d), lambda i: (i, 0)),
                  _mod_blockspec(d, tiles_per_row, fixed_row),
                  _const_spec((1, d)),
                  _const_spec(wgu.shape),
                  _const_spec(wd.shape)],
        out_specs=pl.BlockSpec((tm, d), lambda i: (i, 0)),
        out_shape=jax.ShapeDtypeStruct((n, d), F32),
        scratch_shapes=[pltpu.VMEM((tm, d), BF16), pltpu.VMEM((tm, d), F32)],
        compiler_params=_cparams(("arbitrary",)),
        name="ffn",
    )(s, mod, g.reshape(1, d), wgu, wd)


def _prep_ffn_weights(w_gate_up, w_down):
    assert w_down.shape[0] % FFN_FC == 0
    return w_gate_up.astype(BF16), w_down.astype(BF16)


def _proj_kernel(a_ref, s_ref, mod_ref, w_ref, b_ref, o_ref):
    gate = mod_ref[5:6, :]
    y = jnp.dot(a_ref[...].astype(BF16), w_ref[...], preferred_element_type=F32) + b_ref[...]
    o_ref[...] = s_ref[...] + gate * y


def _proj_residual(a, s, mod, w, b, rows_per_cond, fixed_row):
    n, d = s.shape
    tm = min(PROJ_TM, n)
    tiles_per_row = rows_per_cond // tm
    return pl.pallas_call(
        _proj_kernel,
        grid=(n // tm,),
        in_specs=[pl.BlockSpec((tm, a.shape[1]), lambda i: (i, 0)),
                  pl.BlockSpec((tm, d), lambda i: (i, 0)),
                  _mod_blockspec(d, tiles_per_row, fixed_row),
                  _const_spec(w.shape),
                  _const_spec((1, d))],
        out_specs=pl.BlockSpec((tm, d), lambda i: (i, 0)),
        out_shape=jax.ShapeDtypeStruct((n, d), F32),
        compiler_params=_cparams(("arbitrary",)),
        name="proj_residual",
    )(a, s, mod, w, b.reshape(1, d))


def _head_norm(x, p, pt, w):
    sq = x * x
    hi = sq.astype(BF16)
    lo = (sq - hi.astype(F32)).astype(BF16)
    ss = jnp.dot(hi, p, preferred_element_type=F32) + jnp.dot(lo, p, preferred_element_type=F32)
    r = lax.rsqrt(ss * (1.0 / HEAD_DIM) + EPS)
    rh = r.astype(BF16)
    rl = (r - rh.astype(F32)).astype(BF16)
    rb = jnp.dot(rh, pt, preferred_element_type=F32) + jnp.dot(rl, pt, preferred_element_type=F32)
    return x * rb * w


def _rope_block(xb, cos, sin):
    lane = lax.broadcasted_iota(jnp.int32, xb.shape, 1)
    fwd = pltpu.roll(xb, 16, axis=1)
    bwd = pltpu.roll(xb, LANES - 16, axis=1)
    partner = jnp.where((lane % 32) < 16, bwd, fwd)
    return xb * cos + partner * sin


def _attn_pre_kernel(rope, s_ref, mod_ref, g_ref, w_ref, qn_ref, kn_ref, p_ref, pt_ref,
                     cos_ref, sin_ref, q_ref, kt_ref, v_ref):
    s = s_ref[...]
    tm = s.shape[0]
    nq = N_HEADS * HEAD_DIM
    nk = N_KV_HEADS * HEAD_DIM
    h = _modulated(s, g_ref[...], mod_ref[3:4, :], mod_ref[4:5, :]).astype(BF16)
    qkv = jnp.dot(h, w_ref[...], preferred_element_type=F32)
    q = _head_norm(qkv[:, :nq], p_ref[...], pt_ref[...], qn_ref[...])
    k = _head_norm(qkv[:, nq:nq + nk], p_ref[:nk, :], pt_ref[:, :nk], kn_ref[...])
    v = qkv[:, nq + nk:]
    cos = cos_ref[...]
    sin = sin_ref[...]
    for j in range(nq // LANES):
        qb = q[:, j * LANES:(j + 1) * LANES]
        if rope:
            qb = _rope_block(qb, cos, sin)
        q_ref[:, j * LANES:(j + 1) * LANES] = qb.astype(BF16)
    kblocks = []
    for j in range(nk // LANES):
        kb = k[:, j * LANES:(j + 1) * LANES]
        if rope:
            kb = _rope_block(kb, cos, sin)
        kblocks.append(kb)
    kt = jnp.concatenate(kblocks, axis=1).T.astype(BF16)
    for hh in range(N_KV_HEADS):
        kt_ref[hh] = kt[hh * HEAD_DIM:(hh + 1) * HEAD_DIM, :]
    lane = lax.broadcasted_iota(jnp.int32, (tm, LANES), 1)
    ones_col = jnp.where(lane == HEAD_DIM, 1.0, 0.0)
    for hh in range(N_KV_HEADS):
        vb = v[:, (hh // 2) * LANES:(hh // 2 + 1) * LANES]
        if hh % 2 == 1:
            vb = pltpu.roll(vb, HEAD_DIM, axis=1)
        v_ref[hh] = jnp.where(lane < HEAD_DIM, vb, ones_col).astype(BF16)


def _attn_pre(s, mod, g, w_qkv, qn, kn, p, pt, cos, sin, seq_len, fixed_row, rope):
    n, d = s.shape
    tm = PRE_TM
    tps = seq_len // tm
    n_seq = n // seq_len
    nq = N_HEADS * HEAD_DIM
    return pl.pallas_call(
        functools.partial(_attn_pre_kernel, rope),
        grid=(n // tm,),
        in_specs=[pl.BlockSpec((tm, d), lambda i: (i, 0)),
                  _mod_blockspec(d, tps, fixed_row),
                  _const_spec((1, d)),
                  _const_spec(w_qkv.shape),
                  _const_spec((1, nq)),
                  _const_spec((1, N_KV_HEADS * HEAD_DIM)),
                  _const_spec(p.shape),
                  _const_spec(pt.shape),
                  pl.BlockSpec((tm, LANES), lambda i: (i % tps, 0)),
                  pl.BlockSpec((tm, LANES), lambda i: (i % tps, 0))],
        out_specs=[pl.BlockSpec((tm, nq), lambda i: (i, 0)),
                   pl.BlockSpec((None, N_KV_HEADS, HEAD_DIM, tm), lambda i: (i // tps, 0, 0, i % tps)),
                   pl.BlockSpec((None, N_KV_HEADS, tm, LANES), lambda i: (i // tps, 0, i % tps, 0))],
        out_shape=[jax.ShapeDtypeStruct((n, nq), BF16),
                   jax.ShapeDtypeStruct((n_seq, N_KV_HEADS, HEAD_DIM, seq_len), BF16),
                   jax.ShapeDtypeStruct((n_seq, N_KV_HEADS, seq_len, LANES), BF16)],
        compiler_params=_cparams(("arbitrary",)),
        name="attn_pre",
    )(s, mod, g.reshape(1, d), w_qkv, qn, kn, p, pt, cos, sin)


def _rope_tables(seq_len):
    half = HEAD_DIM // 4
    t = np.arange(seq_len)
    pos = np.stack([t // GRID_W, t % GRID_W], axis=1).astype(np.float32)
    freqs = (ROPE_THETA ** (-np.arange(half, dtype=np.float32) / half)).astype(np.float32)
    lane = np.arange(LANES)
    axis = (lane % HEAD_DIM) // (HEAD_DIM // 2)
    e = lane % (HEAD_DIM // 2)
    ang = pos[:, axis] * freqs[e % half][None, :]
    sign = np.where(e < half, -1.0, 1.0).astype(np.float32)
    return jnp.asarray(np.cos(ang), F32), jnp.asarray(np.sin(ang) * sign[None, :], F32)


def _head_sum_matrices():
    lane = np.arange(N_HEADS * HEAD_DIM)
    p = (lane[:, None] // HEAD_DIM == np.arange(LANES)[None, :]).astype(np.float32)
    return jnp.asarray(p, BF16), jnp.asarray(p.T, BF16)


def _attn_kernel(n_lat_chunks, tk, q_ref, ktc_ref, vc_ref, *rest):
    if n_lat_chunks:
        ktl_ref, vl_ref, o_ref = rest
    else:
        (o_ref,) = rest
    q = q_ref[...]
    tq = q.shape[0]
    q4 = jnp.concatenate([q[:, g * HEAD_DIM:(g + 1) * HEAD_DIM] for g in range(KV_GROUP)], axis=0)

    def step(kt, v, m, acc):
        s = jnp.dot(q4, kt, preferred_element_type=F32)
        m_new = jnp.maximum(m, jnp.max(s, axis=-1, keepdims=True))
        p = jnp.exp2(s - m_new)
        alpha = jnp.exp2(m - m_new)
        acc = alpha * acc + jnp.dot(p.astype(BF16), v, preferred_element_type=F32)
        return m_new, acc

    m0 = jnp.full((KV_GROUP * tq, 1), -1e30, F32)
    acc0 = jnp.zeros((KV_GROUP * tq, LANES), F32)
    m, acc = step(ktc_ref[...], vc_ref[...], m0, acc0)
    if n_lat_chunks:
        def body(c, carry):
            off = pl.multiple_of(c * tk, tk)
            return step(ktl_ref[:, pl.ds(off, tk)], vl_ref[pl.ds(off, tk), :], *carry)
        m, acc = lax.fori_loop(0, n_lat_chunks, body, (m, acc), unroll=8)
    o = acc[:, :HEAD_DIM] / acc[:, HEAD_DIM:HEAD_DIM + 1]
    o_ref[...] = jnp.concatenate([o[g * tq:(g + 1) * tq, :] for g in range(KV_GROUP)], axis=1).astype(BF16)


def _attention(q, ktc, vc, ktl, vl, lq):
    n, nq = q.shape
    b = n // lq
    tq = ATT_TQ
    nqt = lq // tq
    gw = KV_GROUP * HEAD_DIM
    lc = ktc.shape[-1]
    in_specs = [pl.BlockSpec((tq, gw), lambda bb, j, i: (bb * nqt + i, j)),
                pl.BlockSpec((None, None, HEAD_DIM, lc), lambda bb, j, i: (bb, j, 0, 0)),
                pl.BlockSpec((None, None, lc, LANES), lambda bb, j, i: (bb, j, 0, 0))]
    args = [q, ktc, vc]
    n_lat = 0
    if ktl is not None:
        ll = ktl.shape[-1]
        n_lat = ll // ATT_TK
        in_specs += [pl.BlockSpec((None, None, HEAD_DIM, ll), lambda bb, j, i: (bb, j, 0, 0)),
                     pl.BlockSpec((None, None, ll, LANES), lambda bb, j, i: (bb, j, 0, 0))]
        args += [ktl, vl]
    return pl.pallas_call(
        functools.partial(_attn_kernel, n_lat, ATT_TK),
        grid=(b, N_KV_HEADS, nqt),
        in_specs=in_specs,
        out_specs=pl.BlockSpec((tq, gw), lambda bb, j, i: (bb * nqt + i, j)),
        out_shape=jax.ShapeDtypeStruct((n, nq), BF16),
        compiler_params=_cparams(("arbitrary", "arbitrary", "arbitrary")),
        name="attention",
    )(*args)


def _hyena_pre_kernel(tps, n_col, s_ref, prev_ref, next_ref, mod_ref, g_ref, w_ref, b_ref,
                      cw_ref, cb_ref, o_ref, h_ref, pre_ref):
    i = pl.program_id(0)
    tm = s_ref.shape[0]
    hb = HALO
    g = g_ref[...]
    shift = mod_ref[3:4, :]
    scale = mod_ref[4:5, :]
    h_ref[0:hb, :] = _modulated(prev_ref[...], g, shift, scale).astype(BF16)
    h_ref[hb:hb + tm, :] = _modulated(s_ref[...], g, shift, scale).astype(BF16)
    h_ref[hb + tm:, :] = _modulated(next_ref[...], g, shift, scale).astype(BF16)
    row = lax.broadcasted_iota(jnp.int32, (tm, 1), 0)
    drop_up = jnp.logical_and(row == 0, i % tps == 0)
    drop_dn = jnp.logical_and(row == tm - 1, i % tps == tps - 1)
    ct = w_ref.shape[1] // n_col
    for c in range(n_col):
        cols = slice(c * ct, (c + 1) * ct)
        pre_ref[...] = jnp.dot(h_ref[...], w_ref[:, cols], preferred_element_type=F32) + b_ref[:, cols]
        up = jnp.where(drop_up, 0.0, pre_ref[hb - 1:hb - 1 + tm, :])
        mid = pre_ref[hb:hb + tm, :]
        dn = jnp.where(drop_dn, 0.0, pre_ref[hb + 1:hb + 1 + tm, :])
        o_ref[:, cols] = (up * cw_ref[0:1, cols] + mid * cw_ref[1:2, cols]
                          + dn * cw_ref[2:3, cols] + cb_ref[:, cols])


def _hyena_pre(s, mod, g, w_in, b_in, conv_w, conv_b, seq_len, fixed_row):
    n, d = s.shape
    tm = PRE_TM
    tps = seq_len // tm
    n3 = w_in.shape[1]
    hb = HALO
    nblk = n // hb
    n_col = 6
    return pl.pallas_call(
        functools.partial(_hyena_pre_kernel, tps, n_col),
        grid=(n // tm,),
        in_specs=[pl.BlockSpec((tm, d), lambda i: (i, 0)),
                  pl.BlockSpec((hb, d), lambda i: (jnp.maximum(i * (tm // hb) - 1, 0), 0)),
                  pl.BlockSpec((hb, d), lambda i: (jnp.minimum((i + 1) * (tm // hb), nblk - 1), 0)),
                  _mod_blockspec(d, tps, fixed_row),
                  _const_spec((1, d)),
                  _const_spec(w_in.shape),
                  _const_spec((1, n3)),
                  _const_spec((3, n3)),
                  _const_spec((1, n3))],
        out_specs=pl.BlockSpec((tm, n3), lambda i: (i, 0)),
        out_shape=jax.ShapeDtypeStruct((n, n3), F32),
        scratch_shapes=[pltpu.VMEM((tm + 2 * hb, d), BF16),
                        pltpu.VMEM((tm + 2 * hb, n3 // n_col), F32)],
        compiler_params=_cparams(("arbitrary",)),
        name="hyena_pre",
    )(s, s, s, mod, g.reshape(1, d), w_in, b_in.reshape(1, n3), conv_w, conv_b.reshape(1, n3))


def _filter_kernel(n_feat, d, feat_ref, w1_ref, b1_ref, w2_ref, b2_ref, w3_ref, b3_ref,
                   wo_ref, a_ref, delta_ref, o_ref):
    hp = lax.Precision.HIGHEST
    feats = feat_ref[...]
    a = a_ref[...]
    hid = jnp.sin(a * (jnp.dot(feats, w1_ref[...], precision=hp, preferred_element_type=F32) + b1_ref[...]))
    hid = jnp.sin(a * (jnp.dot(hid, w2_ref[...], precision=hp, preferred_element_type=F32) + b2_ref[...]))
    hid = jnp.sin(a * (jnp.dot(hid, w3_ref[...], precision=hp, preferred_element_type=F32) + b3_ref[...]))
    t = feats[:, 0:1]
    use_fwd = feats[:, n_feat:n_feat + 1]
    use_bwd = feats[:, n_feat + 1:n_feat + 2]
    decay = jnp.exp(-t * delta_ref[...])
    hb = hid.astype(BF16)
    for o in range(2):
        fwd = jnp.dot(hb, wo_ref[:, (2 * o) * d:(2 * o + 1) * d], preferred_element_type=F32)
        bwd = jnp.dot(hb, wo_ref[:, (2 * o + 1) * d:(2 * o + 2) * d], preferred_element_type=F32)
        o_ref[o] = (decay * (use_fwd * fwd + use_bwd * bwd)).astype(BF16)


def _filter_features(seq_len):
    l = seq_len
    n_feat = 1 + 2 * FILTER_BANDS
    n = np.arange(2 * l)
    pos = np.where(n < l, n, 2 * l - n)
    pos = np.where(n == l, 0, pos)
    t = np.linspace(0.0, 1.0, l, dtype=np.float32)[pos]
    w = (2.0 * math.pi * pos.astype(np.float32) / l).astype(np.float32)
    bands = np.linspace(1e-4, FILTER_BANDS - 1, FILTER_BANDS, dtype=np.float32)
    bw = (bands[None, :] * w[:, None]).astype(np.float32)
    feats = np.zeros((2 * l, LANES), np.float32)
    feats[:, 0] = t
    feats[:, 1:1 + FILTER_BANDS] = np.cos(bw)
    feats[:, 1 + FILTER_BANDS:n_feat] = -np.sin(bw)
    feats[:, n_feat] = (n < l)
    feats[:, n_feat + 1] = np.logical_or(n > l, n == 0)
    return jnp.asarray(feats, F32), n_feat


def _filter_taps(seq_len, f_w1, f_b1, f_w2, f_b2, f_w3, f_b3, f_wout, f_freq):
    feats, n_feat = _filter_features(seq_len)
    fh = f_w1.shape[1]
    d = f_wout.shape[1] // 4
    w1p = jnp.zeros((LANES, fh), F32).at[:n_feat].set(f_w1)
    min_decay = math.log(DECAY_TARGET) / SLOW_DECAY_PCT
    max_decay = math.log(DECAY_TARGET) / FAST_DECAY_PCT
    deltas = jnp.abs(jnp.linspace(min_decay, max_decay, d, dtype=F32)).reshape(1, d)
    n = 2 * seq_len
    tm = min(512, n)
    return pl.pallas_call(
        functools.partial(_filter_kernel, n_feat, d),
        grid=(n // tm,),
        in_specs=[pl.BlockSpec((tm, LANES), lambda i: (i, 0)),
                  _const_spec((LANES, fh)), _const_spec((1, fh)),
                  _const_spec((fh, fh)), _const_spec((1, fh)),
                  _const_spec((fh, fh)), _const_spec((1, fh)),
                  _const_spec(f_wout.shape), _const_spec((1, fh)), _const_spec((1, d))],
        out_specs=pl.BlockSpec((2, tm, d), lambda i: (0, i, 0)),
        out_shape=jax.ShapeDtypeStruct((2, n, d), BF16),
        compiler_params=_cparams(("arbitrary",)),
        name="filter_taps",
    )(feats, w1p, f_b1.reshape(1, fh), f_w2, f_b2.reshape(1, fh), f_w3, f_b3.reshape(1, fh),
      f_wout.astype(BF16), f_freq.reshape(1, fh), deltas)


def _dft_tables(seq_len):
    n_fft = 2 * seq_len
    n1 = n_fft // FFT_N2
    k1 = jnp.arange(n1, dtype=jnp.int32)[None, :, None]
    n2 = jnp.arange(FFT_N2, dtype=jnp.int32)[:, None, None]
    nn1 = jnp.arange(n1, dtype=jnp.int32)[None, None, :]
    ang = (2.0 * math.pi / n_fft) * ((k1 * (FFT_N2 * nn1 + n2)) % n_fft).astype(F32)
    c = jnp.cos(ang)
    s = jnp.sin(ang)
    h = n1 // 2
    g_first = jnp.concatenate([jnp.concatenate([c[:, :, :h], s[:, :, :h]], axis=2),
                               jnp.concatenate([-s[:, :, :h], c[:, :, :h]], axis=2)], axis=1)
    g_last = jnp.swapaxes(g_first, 1, 2) * (1.0 / n_fft)
    g_taps = jnp.concatenate([c, -s], axis=1)
    k2 = jnp.arange(FFT_N2, dtype=jnp.int32)
    ang2 = (2.0 * math.pi / FFT_N2) * ((k2[:, None] * k2[None, :]) % FFT_N2).astype(F32)
    c2, s2 = jnp.cos(ang2), jnp.sin(ang2)
    g_mid = jnp.concatenate([jnp.concatenate([c2, s2], axis=1),
                             jnp.concatenate([-s2, c2], axis=1)], axis=0)
    return (g_first.astype(BF16), g_last.astype(BF16), g_taps.astype(BF16),
            g_mid.astype(BF16), g_mid.T.astype(BF16))


def _stage_rows(x_ref, rows_ref):
    r, nb, ct = x_ref.shape
    xf = x_ref[...].astype(F32).reshape(r * nb, ct)
    for s in range(ct // LANES):
        rows_ref[s] = xf[:, s * LANES:(s + 1) * LANES]


def _column_group(rows_ref, j, r, nb):
    return jnp.concatenate([rows_ref[s, pl.ds(j, r, stride=nb), :] for s in range(rows_ref.shape[0])], axis=1)


def _scatter_column_group(stage_ref, j, y, nb):
    for s in range(stage_ref.shape[0]):
        stage_ref[s, pl.ds(j, y.shape[0], stride=nb), :] = y[:, s * LANES:(s + 1) * LANES]


def _staged_block(stage_ref, nb):
    n_slab, rows, _ = stage_ref.shape
    return jnp.concatenate([stage_ref[s].reshape(rows // nb, nb, LANES) for s in range(n_slab)], axis=2)


def _fft_first_kernel(x_ref, g_ref, o_ref, rows_ref, stage_ref):
    r, nb, _ = x_ref.shape
    _stage_rows(x_ref, rows_ref)
    for j in range(nb):
        xj = _column_group(rows_ref, j, r, nb).astype(BF16)
        stage_ref[:, j, :] = jnp.dot(g_ref[j], xj, preferred_element_type=F32)
    o_ref[...] = stage_ref[...].astype(BF16)


def _fft_first(x3, col_block, d, g):
    r = x3.shape[0]
    rows_out = g.shape[1]
    nb, ct = FFT_NB, FFT_CT
    return pl.pallas_call(
        _fft_first_kernel,
        grid=(FFT_N2 // nb, d // ct),
        in_specs=[pl.BlockSpec((r, nb, ct), lambda i, c: (0, i, col_block * (d // ct) + c)),
                  pl.BlockSpec((nb, rows_out, r), lambda i, c: (i, 0, 0))],
        out_specs=pl.BlockSpec((rows_out, nb, ct), lambda i, c: (0, i, c)),
        out_shape=jax.ShapeDtypeStruct((rows_out, FFT_N2, d), BF16),
        scratch_shapes=[pltpu.VMEM((ct // LANES, r * nb, LANES), F32),
                        pltpu.VMEM((rows_out, nb, ct), F32)],
        compiler_params=_cparams(("arbitrary", "arbitrary")),
        name="fft_first",
    )(x3, g)


def _fft_mid_kernel(with_filter, t_ref, g_ref, *rest):
    if with_filter:
        h_ref, gi_ref, o_ref = rest
    else:
        (o_ref,) = rest
    half = FFT_N2
    for kk in range(FFT_KB):
        x = jnp.concatenate([t_ref[0, kk], t_ref[1, kk]], axis=0)
        y = jnp.dot(g_ref[...], x, preferred_element_type=F32)
        if with_filter:
            yr, yi = y[:half], y[half:]
            hr, hi = h_ref[0, kk].astype(F32), h_ref[1, kk].astype(F32)
            z = jnp.concatenate([yr * hr - yi * hi, yr * hi + yi * hr], axis=0).astype(BF16)
            y = jnp.dot(gi_ref[...], z, preferred_element_type=F32)
        o_ref[0, kk] = y[:half].astype(BF16)
        o_ref[1, kk] = y[half:].astype(BF16)


def _fft_mid(t, g_mid, h=None, g_mid_inv=None):
    rows, _, d = t.shape
    n1 = rows // 2
    t4 = t.reshape(2, n1, FFT_N2, d)
    blk = pl.BlockSpec((2, FFT_KB, FFT_N2, d), lambda i: (0, i, 0, 0))
    in_specs = [blk, _const_spec(g_mid.shape)]
    args = [t4, g_mid]
    if h is not None:
        in_specs += [blk, _const_spec(g_mid_inv.shape)]
        args += [h.reshape(2, n1, FFT_N2, d), g_mid_inv]
    out = pl.pallas_call(
        functools.partial(_fft_mid_kernel, h is not None),
        grid=(n1 // FFT_KB,),
        in_specs=in_specs,
        out_specs=blk,
        out_shape=jax.ShapeDtypeStruct((2, n1, FFT_N2, d), BF16),
        compiler_params=_cparams(("arbitrary",)),
        name="fft_mid",
    )(*args)
    return out.reshape(rows, FFT_N2, d)


def _fft_last_kernel(b_ref, g_ref, z_ref, gate_ref, bias_ref, o_ref, rows_ref, stage_ref):
    rows_in, nb, _ = b_ref.shape
    _stage_rows(b_ref, rows_ref)
    for j in range(nb):
        bj = _column_group(rows_ref, j, rows_in, nb).astype(BF16)
        _scatter_column_group(stage_ref, j, jnp.dot(g_ref[j], bj, preferred_element_type=F32), nb)
    o_ref[...] = gate_ref[...] * (_staged_block(stage_ref, nb) + z_ref[...] * bias_ref[...])


def _fft_last(b, g, z3, z_col, gate3, gate_col, bias):
    rows_in, _, d = b.shape
    r = g.shape[1]
    nb, ct = FFT_NB, FFT_CT_LAST
    nc = d // ct
    return pl.pallas_call(
        _fft_last_kernel,
        grid=(FFT_N2 // nb, nc),
        in_specs=[pl.BlockSpec((rows_in, nb, ct), lambda i, c: (0, i, c)),
                  pl.BlockSpec((nb, r, rows_in), lambda i, c: (i, 0, 0)),
                  pl.BlockSpec((r, nb, ct), lambda i, c: (0, i, z_col * nc + c)),
                  pl.BlockSpec((r, nb, ct), lambda i, c: (0, i, gate_col * nc + c)),
                  pl.BlockSpec((1, 1, ct), lambda i, c: (0, 0, c))],
        out_specs=pl.BlockSpec((r, nb, ct), lambda i, c: (0, i, c)),
        out_shape=jax.ShapeDtypeStruct((r, FFT_N2, d), F32),
        scratch_shapes=[pltpu.VMEM((ct // LANES, rows_in * nb, LANES), F32),
                        pltpu.VMEM((ct // LANES, r * nb, LANES), F32)],
        compiler_params=_cparams(("arbitrary", "arbitrary")),
        name="fft_last",
    )(b, g, z3, gate3, bias.reshape(1, 1, d))


def _hyena_long_convs(u, taps, f_bias, seq_len):
    n, d3 = u.shape
    d = d3 // 3
    g_first, g_last, g_taps, g_mid, g_mid_inv = _dft_tables(seq_len)
    n1 = 2 * seq_len // FFT_N2
    u3 = u.reshape(n1, FFT_N2, d3)
    z3, z_col = u3, 0
    for o in range(2):
        spec = _fft_mid(_fft_first(taps[o].reshape(n1, FFT_N2, d), 0, d, g_taps), g_mid)
        a = _fft_first(z3, z_col, d, g_first)
        bq = _fft_mid(a, g_mid, spec, g_mid_inv)
        z3 = _fft_last(bq, g_last, z3, z_col, u3, 1 + o, f_bias[o])
        z_col = 0
    return z3.reshape(n, d)


def _small_conv_kernel(u_v_ref, u_x1_ref, u_x2_ref, taps_ref, gf_ref, gt_ref, gi_ref, bias_ref, o_ref):
    nf = gf_ref.shape[0] // 2
    z = u_v_ref[...]
    gates = (u_x1_ref, u_x2_ref)
    for o in range(2):
        spec = jnp.dot(gt_ref[...], taps_ref[o].astype(BF16), preferred_element_type=F32)
        zq = jnp.dot(gf_ref[...], z.astype(BF16), preferred_element_type=F32)
        zr, zi = zq[:nf], zq[nf:]
        hr, hi = spec[:nf], spec[nf:]
        prod = jnp.concatenate([zr * hr - zi * hi, zr * hi + zi * hr], axis=0).astype(BF16)
        y = jnp.dot(gi_ref[...], prod, preferred_element_type=F32)
        z = gates[o][...] * (y + z * bias_ref[o:o + 1, :])
    o_ref[...] = z


def _small_dft_tables(seq_len):
    n_fft = 2 * seq_len
    k = jnp.arange(n_fft, dtype=jnp.int32)
    ang = (2.0 * math.pi / n_fft) * ((k[:, None] * k[None, :]) % n_fft).astype(F32)
    c, s = jnp.cos(ang), jnp.sin(ang)
    cl, sl = c[:, :seq_len], s[:, :seq_len]
    g_fwd = jnp.concatenate([jnp.concatenate([cl, sl], axis=1),
                             jnp.concatenate([-sl, cl], axis=1)], axis=0)
    g_taps = jnp.concatenate([c, -s], axis=0)
    g_inv = g_fwd.T * (1.0 / n_fft)
    return g_fwd.astype(BF16), g_taps.astype(BF16), g_inv.astype(BF16)


def _hyena_small_convs(u, taps, f_bias, seq_len):
    n, d3 = u.shape
    d = d3 // 3
    ct = 256
    nc = d // ct
    g_fwd, g_taps, g_inv = _small_dft_tables(seq_len)
    return pl.pallas_call(
        _small_conv_kernel,
        grid=(nc,),
        in_specs=[pl.BlockSpec((n, ct), lambda c: (0, c)),
                  pl.BlockSpec((n, ct), lambda c: (0, nc + c)),
                  pl.BlockSpec((n, ct), lambda c: (0, 2 * nc + c)),
                  pl.BlockSpec((2, 2 * seq_len, ct), lambda c: (0, 0, c)),
                  _const_spec(g_fwd.shape), _const_spec(g_taps.shape), _const_spec(g_inv.shape),
                  pl.BlockSpec((2, ct), lambda c: (0, c))],
        out_specs=pl.BlockSpec((n, ct), lambda c: (0, c)),
        out_shape=jax.ShapeDtypeStruct((n, d), F32),
        compiler_params=_cparams(("arbitrary",)),
        name="small_conv",
    )(u, u, u, taps, g_fwd, g_taps, g_inv, f_bias)


def kernel(x, c, ctx, c_ctx, w_mod, b_mod, norm_w, ffn_w_gate_up, ffn_w_down, attn_w_qkv, attn_w_o,
           attn_q_norm, attn_k_norm, hy_w_in, hy_b_in, hy_conv_w, hy_conv_b, hy_f_w1, hy_f_b1,
           hy_f_w2, hy_f_b2, hy_f_w3, hy_f_b3, hy_f_wout, hy_f_freq, hy_f_bias, hy_w_out, hy_b_out):
    bsz, seq, d = x.shape
    ctx_len = ctx.shape[1]
    depth = w_mod.shape[0]
    assert bsz == 2, "the long convolution packs exactly two batches into one complex sequence"
    xs = x.reshape(bsz * seq, d)
    cs = ctx.reshape(bsz * ctx_len, d)

    cc = jnp.zeros((8, d), F32).at[:bsz].set(c).at[bsz].set(c_ctx)
    mod_all = _mod_vectors(cc, w_mod, b_mod).reshape(depth, 8, N_MOD, d)
    ctx_row = bsz

    c_rows = bsz * ctx_len
    p_sum, p_bcast = _head_sum_matrices()
    cos_t, sin_t = _rope_tables(seq)
    zero_bias = jnp.zeros((d,), F32)

    for l in range(depth):
        mod = mod_all[l]
        is_attn = (l % 2) == 0
        ctx_out = l < depth - 1
        ctx_live = ctx_out or is_attn
        wgu0, wd0 = _prep_ffn_weights(ffn_w_gate_up[l, 0], ffn_w_down[l, 0])
        wgu1, wd1 = _prep_ffn_weights(ffn_w_gate_up[l, 1], ffn_w_down[l, 1])

        xs = _ffn(xs, mod, 0, norm_w[l, 0], wgu0, wd0, seq, None)
        if ctx_live:
            cs = _ffn(cs, mod, 0, norm_w[l, 0], wgu0, wd0, c_rows, ctx_row)

        if is_attn:
            a = l // 2
            w_qkv = attn_w_qkv[a].astype(BF16)
            w_o = attn_w_o[a].astype(BF16)
            qn = (jnp.tile(attn_q_norm[a], N_HEADS) * (HEAD_DIM ** -0.5 * math.log2(math.e))).reshape(1, -1)
            kn = jnp.tile(attn_k_norm[a], N_KV_HEADS).reshape(1, -1)
            q_l, kt_l, v_l = _attn_pre(xs, mod, norm_w[l, 1], w_qkv, qn, kn, p_sum, p_bcast,
                                       cos_t, sin_t, seq, None, True)
            q_c, kt_c, v_c = _attn_pre(cs, mod, norm_w[l, 1], w_qkv, qn, kn, p_sum, p_bcast,
                                       cos_t, sin_t, ctx_len, ctx_row, False)
            o_l = _attention(q_l, kt_c, v_c, kt_l, v_l, seq)
            xs = _proj_residual(o_l, xs, mod, w_o, zero_bias, seq, None)
            if ctx_out:
                o_c = _attention(q_c, kt_c, v_c, None, None, ctx_len)
                cs = _proj_residual(o_c, cs, mod, w_o, zero_bias, c_rows, ctx_row)
        else:
            j = l // 2
            w_in = hy_w_in[j].astype(BF16)
            w_out = hy_w_out[j].astype(BF16)
            fargs = (hy_f_w1[j], hy_f_b1[j], hy_f_w2[j], hy_f_b2[j], hy_f_w3[j], hy_f_b3[j],
                     hy_f_wout[j], hy_f_freq[j])
            u_l = _hyena_pre(xs, mod, norm_w[l, 1], w_in, hy_b_in[j], hy_conv_w[j], hy_conv_b[j], seq, None)
            y_l = _hyena_long_convs(u_l, _filter_taps(seq, *fargs), hy_f_bias[j], seq)
            xs_new = _proj_residual(y_l, xs, mod, w_out, hy_b_out[j], seq, None)
            if ctx_out:
                u_c = _hyena_pre(cs, mod, norm_w[l, 1], w_in, hy_b_in[j], hy_conv_w[j], hy_conv_b[j],
                                 ctx_len, ctx_row)
                y_c = _hyena_small_convs(u_c, _filter_taps(ctx_len, *fargs), hy_f_bias[j], ctx_len)
                cs = _proj_residual(y_c, cs, mod, w_out, hy_b_out[j], c_rows, ctx_row)
            xs = xs_new

        xs = _ffn(xs, mod, 2, norm_w[l, 2], wgu1, wd1, seq, None)
        if ctx_out:
            cs = _ffn(cs, mod, 2, norm_w[l, 2], wgu1, wd1, c_rows, ctx_row)
    return xs.reshape(bsz, seq, d)
```

```python
import functools
import math

import jax
import jax.numpy as jnp
import numpy as np
from jax import lax
from jax.experimental import pallas as pl
from jax.experimental.pallas import tpu as pltpu

F32 = jnp.float32
BF16 = jnp.bfloat16

N_MOD = 9
N_HEADS = 16
N_KV_HEADS = 4
HEAD_DIM = 64
KV_GROUP = N_HEADS // N_KV_HEADS
KT_ROWS = 2 * HEAD_DIM
ATT_SHIFT_LIMIT = 60.0
GRID_W = 64
ROPE_THETA = 10000.0
EPS = 1e-6
FILTER_BANDS = 16
DECAY_TARGET = 1e-2
FAST_DECAY_PCT = 0.3
SLOW_DECAY_PCT = 1.5

LANES = 128
HALO = 16
FFT_N2 = 128
VMEM_LIMIT = 56 * 1024 * 1024

FFN_TM = 1024
PROJ_TM = 512
PRE_TM = 256
FFN_FC = 256
ATT_TQ = 256
ATT_TK = 512
FFT_NB = 16
FFT_KB = 8
FFT_CT = 512
FFT_CT_LAST = 256


def _cparams(sem):
    return pltpu.CompilerParams(dimension_semantics=sem, vmem_limit_bytes=VMEM_LIMIT)


def _const_spec(shape):
    nd = len(shape)
    return pl.BlockSpec(shape, lambda *_: (0,) * nd, pipeline_mode=pl.Buffered(1))


def _modulated(s, g, shift, scale):
    ms = jnp.mean(s * s, axis=-1, keepdims=True)
    return (s * lax.rsqrt(ms + EPS) * g) * (1.0 + scale) + shift


def _mod_kernel(c_ref, w_ref, b_ref, o_ref):
    c = c_ref[...]
    a = (c * jax.nn.sigmoid(c)).astype(BF16)
    o_ref[0] = jnp.dot(a, w_ref[0].astype(BF16), preferred_element_type=F32) + b_ref[0]


def _mod_vectors(cc, w_mod, b_mod):
    depth, d, n = w_mod.shape
    tn = n // 6
    return pl.pallas_call(
        _mod_kernel,
        grid=(depth, n // tn),
        in_specs=[pl.BlockSpec((8, d), lambda l, j: (0, 0)),
                  pl.BlockSpec((1, d, tn), lambda l, j: (l, 0, j)),
                  pl.BlockSpec((1, 1, tn), lambda l, j: (l, 0, j))],
        out_specs=pl.BlockSpec((1, 8, tn), lambda l, j: (l, 0, j)),
        out_shape=jax.ShapeDtypeStruct((depth, 8, n), F32),
        compiler_params=_cparams(("arbitrary", "arbitrary")),
        name="mod_vectors",
    )(cc, w_mod, b_mod.reshape(depth, 1, n))


def _mod_blockspec(d, tiles_per_row, fixed_row):
    if fixed_row is None:
        return pl.BlockSpec((None, N_MOD, d), lambda i: (i // tiles_per_row, 0, 0))
    return pl.BlockSpec((None, N_MOD, d), lambda i: (fixed_row, 0, 0))


def _ffn_kernel(k, s_ref, mod_ref, g_ref, wgu_ref, wd_ref, o_ref, h_ref, acc_ref):
    s = s_ref[...]
    shift = mod_ref[3 * k:3 * k + 1, :]
    scale = mod_ref[3 * k + 1:3 * k + 2, :]
    gate = mod_ref[3 * k + 2:3 * k + 3, :]
    h_ref[...] = _modulated(s, g_ref[...], shift, scale).astype(BF16)
    f = wd_ref.shape[0]
    for c in range(f // FFN_FC):
        lo = c * FFN_FC
        g = jnp.dot(h_ref[...], wgu_ref[:, lo:lo + FFN_FC], preferred_element_type=F32)
        u = jnp.dot(h_ref[...], wgu_ref[:, f + lo:f + lo + FFN_FC], preferred_element_type=F32)
        a = (g * jax.nn.sigmoid(g) * u).astype(BF16)
        y = jnp.dot(a, wd_ref[lo:lo + FFN_FC, :], preferred_element_type=F32)
        if c == 0:
            acc_ref[...] = y
        else:
            acc_ref[...] += y
    o_ref[...] = s + 0.5 * gate * acc_ref[...]


def _ffn(s, mod, k, g, wgu, wd, rows_per_cond, fixed_row):
    n, d = s.shape
    tm = min(FFN_TM, n)
    tiles_per_row = rows_per_cond // tm
    return pl.pallas_call(
        functools.partial(_ffn_kernel, k),
        grid=(n // tm,),
        in_specs=[pl.BlockSpec((tm, d), lambda i: (i, 0)),
                  _mod_blockspec(d, tiles_per_row, fixed_row),
                  _const_spec((1, d)),
                  _const_spec(wgu.shape),
                  _const_spec(wd.shape)],
        out_specs=pl.BlockSpec((tm, d), lambda i: (i, 0)),
        out_shape=jax.ShapeDtypeStruct((n, d), F32),
        scratch_shapes=[pltpu.VMEM((tm, d), BF16), pltpu.VMEM((tm, d), F32)],
        compiler_params=_cparams(("arbitrary",)),
        name="ffn",
    )(s, mod, g.reshape(1, d), wgu, wd)


def _prep_ffn_weights(w_gate_up, w_down):
    assert w_down.shape[0] % FFN_FC == 0
    return w_gate_up.astype(BF16), w_down.astype(BF16)


def _proj_kernel(a_ref, s_ref, mod_ref, w_ref, b_ref, o_ref):
    gate = mod_ref[5:6, :]
    y = jnp.dot(a_ref[...].astype(BF16), w_ref[...], preferred_element_type=F32) + b_ref[...]
    o_ref[...] = s_ref[...] + gate * y


def _proj_residual(a, s, mod, w, b, rows_per_cond, fixed_row):
    n, d = s.shape
    tm = min(PROJ_TM, n)
    tiles_per_row = rows_per_cond // tm
    return pl.pallas_call(
        _proj_kernel,
        grid=(n // tm,),
        in_specs=[pl.BlockSpec((tm, a.shape[1]), lambda i: (i, 0)),
                  pl.BlockSpec((tm, d), lambda i: (i, 0)),
                  _mod_blockspec(d, tiles_per_row, fixed_row),
                  _const_spec(w.shape),
                  _const_spec((1, d))],
        out_specs=pl.BlockSpec((tm, d), lambda i: (i, 0)),
        out_shape=jax.ShapeDtypeStruct((n, d), F32),
        compiler_params=_cparams(("arbitrary",)),
        name="proj_residual",
    )(a, s, mod, w, b.reshape(1, d))


def _head_norm(x, p, pt, w):
    sq = x * x
    hi = sq.astype(BF16)
    lo = (sq - hi.astype(F32)).astype(BF16)
    ss = jnp.dot(hi, p, preferred_element_type=F32) + jnp.dot(lo, p, preferred_element_type=F32)
    r = lax.rsqrt(ss * (1.0 / HEAD_DIM) + EPS)
    rh = r.astype(BF16)
    rl = (r - rh.astype(F32)).astype(BF16)
    rb = jnp.dot(rh, pt, preferred_element_type=F32) + jnp.dot(rl, pt, preferred_element_type=F32)
    return x * rb * w


def _rope_block(xb, cos, sin):
    lane = lax.broadcasted_iota(jnp.int32, xb.shape, 1)
    fwd = pltpu.roll(xb, 16, axis=1)
    bwd = pltpu.roll(xb, LANES - 16, axis=1)
    partner = jnp.where((lane % 32) < 16, bwd, fwd)
    return xb * cos + partner * sin


def _attn_pre_kernel(rope, s_ref, mod_ref, g_ref, w_ref, qn_ref, kn_ref, p_ref, pt_ref,
                     cos_ref, sin_ref, q_ref, kt_ref, v_ref):
    s = s_ref[...]
    tm = s.shape[0]
    nq = N_HEADS * HEAD_DIM
    nk = N_KV_HEADS * HEAD_DIM
    h = _modulated(s, g_ref[...], mod_ref[3:4, :], mod_ref[4:5, :]).astype(BF16)
    qkv = jnp.dot(h, w_ref[...], preferred_element_type=F32)
    q = _head_norm(qkv[:, :nq], p_ref[...], pt_ref[...], qn_ref[...])
    k = _head_norm(qkv[:, nq:nq + nk], p_ref[:nk, :], pt_ref[:, :nk], kn_ref[...])
    v = qkv[:, nq + nk:]
    cos = cos_ref[...]
    sin = sin_ref[...]
    for j in range(nq // LANES):
        qb = q[:, j * LANES:(j + 1) * LANES]
        if rope:
            qb = _rope_block(qb, cos, sin)
        q_ref[:, j * LANES:(j + 1) * LANES] = qb.astype(BF16)
    kblocks = []
    for j in range(nk // LANES):
        kb = k[:, j * LANES:(j + 1) * LANES]
        if rope:
            kb = _rope_block(kb, cos, sin)
        kblocks.append(kb)
    kt = jnp.concatenate(kblocks, axis=1).T.astype(BF16)
    tail_row = lax.broadcasted_iota(jnp.int32, (KT_ROWS - HEAD_DIM, tm), 0)
    tail = jnp.where(tail_row == 0, 1.0, 0.0).astype(BF16)
    for hh in range(N_KV_HEADS):
        kt_ref[hh] = jnp.concatenate([kt[hh * HEAD_DIM:(hh + 1) * HEAD_DIM, :], tail], axis=0)
    lane = lax.broadcasted_iota(jnp.int32, (tm, LANES), 1)
    ones_col = jnp.where(lane == HEAD_DIM, 1.0, 0.0)
    for hh in range(N_KV_HEADS):
        vb = v[:, (hh // 2) * LANES:(hh // 2 + 1) * LANES]
        if hh % 2 == 1:
            vb = pltpu.roll(vb, HEAD_DIM, axis=1)
        v_ref[hh] = jnp.where(lane < HEAD_DIM, vb, ones_col).astype(BF16)


def _attn_pre(s, mod, g, w_qkv, qn, kn, p, pt, cos, sin, seq_len, fixed_row, rope):
    n, d = s.shape
    tm = min(PRE_TM, seq_len)
    tps = seq_len // tm
    n_seq = n // seq_len
    nq = N_HEADS * HEAD_DIM
    return pl.pallas_call(
        functools.partial(_attn_pre_kernel, rope),
        grid=(n // tm,),
        in_specs=[pl.BlockSpec((tm, d), lambda i: (i, 0)),
                  _mod_blockspec(d, tps, fixed_row),
                  _const_spec((1, d)),
                  _const_spec(w_qkv.shape),
                  _const_spec((1, nq)),
                  _const_spec((1, N_KV_HEADS * HEAD_DIM)),
                  _const_spec(p.shape),
                  _const_spec(pt.shape),
                  pl.BlockSpec((tm, LANES), lambda i: (i % tps, 0)),
                  pl.BlockSpec((tm, LANES), lambda i: (i % tps, 0))],
        out_specs=[pl.BlockSpec((tm, nq), lambda i: (i, 0)),
                   pl.BlockSpec((None, N_KV_HEADS, KT_ROWS, tm), lambda i: (i // tps, 0, 0, i % tps)),
                   pl.BlockSpec((None, N_KV_HEADS, tm, LANES), lambda i: (i // tps, 0, i % tps, 0))],
        out_shape=[jax.ShapeDtypeStruct((n, nq), BF16),
                   jax.ShapeDtypeStruct((n_seq, N_KV_HEADS, KT_ROWS, seq_len), BF16),
                   jax.ShapeDtypeStruct((n_seq, N_KV_HEADS, seq_len, LANES), BF16)],
        compiler_params=_cparams(("arbitrary",)),
        name="attn_pre",
    )(s, mod, g.reshape(1, d), w_qkv, qn, kn, p, pt, cos, sin)


def _rope_tables(seq_len):
    half = HEAD_DIM // 4
    t = np.arange(seq_len)
    pos = np.stack([t // GRID_W, t % GRID_W], axis=1).astype(np.float32)
    freqs = (ROPE_THETA ** (-np.arange(half, dtype=np.float32) / half)).astype(np.float32)
    lane = np.arange(LANES)
    axis = (lane % HEAD_DIM) // (HEAD_DIM // 2)
    e = lane % (HEAD_DIM // 2)
    ang = pos[:, axis] * freqs[e % half][None, :]
    sign = np.where(e < half, -1.0, 1.0).astype(np.float32)
    return jnp.asarray(np.cos(ang), F32), jnp.asarray(np.sin(ang) * sign[None, :], F32)


def _head_sum_matrices():
    lane = np.arange(N_HEADS * HEAD_DIM)
    p = (lane[:, None] // HEAD_DIM == np.arange(LANES)[None, :]).astype(np.float32)
    return jnp.asarray(p, BF16), jnp.asarray(p.T, BF16)


def _max_key_norm(kt_ref):
    k = kt_ref[:HEAD_DIM, :].astype(F32)
    return jnp.sqrt(jnp.max(jnp.sum(k * k, axis=0, keepdims=True), axis=1, keepdims=True))


def _attn_kernel(online, n_lat_chunks, tk, q_ref, ktc_ref, vc_ref, *rest):
    if n_lat_chunks:
        ktl_ref, vl_ref, o_ref, kmax_ref = rest
    else:
        o_ref, kmax_ref = rest
    if not online:
        @pl.when(pl.program_id(2) == 0)
        def _():
            kmax = _max_key_norm(ktc_ref)
            if n_lat_chunks:
                kmax = jnp.maximum(kmax, _max_key_norm(ktl_ref))
            kmax_ref[...] = kmax
    qf = q_ref[...].astype(F32)
    tq = qf.shape[0]
    lane = lax.broadcasted_iota(jnp.int32, (tq, LANES), 1)
    parts = []
    for g in range(KV_GROUP):
        blk = qf[:, (g // 2) * LANES:(g // 2 + 1) * LANES]
        if g % 2 == 1:
            blk = pltpu.roll(blk, HEAD_DIM, axis=1)
        qg = jnp.where(lane < HEAD_DIM, blk, 0.0)
        if not online:
            shift = jnp.sqrt(jnp.sum(qg * qg, axis=1, keepdims=True)) * kmax_ref[...]
            qg = jnp.where(lane == HEAD_DIM, -shift, qg)
        parts.append(qg.astype(BF16))
    q4 = jnp.concatenate(parts, axis=0)

    if online:
        def step(kt, v, carry):
            m, acc = carry
            s = jnp.dot(q4, kt, preferred_element_type=F32)
            m_new = jnp.maximum(m, jnp.max(s, axis=-1, keepdims=True))
            p = jnp.exp2(s - m_new)
            acc = jnp.exp2(m - m_new) * acc + jnp.dot(p.astype(BF16), v, preferred_element_type=F32)
            return m_new, acc
        carry = (jnp.full((KV_GROUP * tq, 1), -1e30, F32), jnp.zeros((KV_GROUP * tq, LANES), F32))
    else:
        def step(kt, v, acc):
            p = jnp.exp2(jnp.dot(q4, kt, preferred_element_type=F32))
            return acc + jnp.dot(p.astype(BF16), v, preferred_element_type=F32)
        carry = jnp.zeros((KV_GROUP * tq, LANES), F32)

    carry = step(ktc_ref[...], vc_ref[...], carry)
    if n_lat_chunks:
        def body(c, carry):
            off = pl.multiple_of(c * tk, tk)
            return step(ktl_ref[:, pl.ds(off, tk)], vl_ref[pl.ds(off, tk), :], carry)
        carry = lax.fori_loop(0, n_lat_chunks, body, carry, unroll=8 if online else True)
    acc = carry[1] if online else carry
    o = acc[:, :HEAD_DIM] / acc[:, HEAD_DIM:HEAD_DIM + 1]
    o_ref[...] = jnp.concatenate([o[g * tq:(g + 1) * tq, :] for g in range(KV_GROUP)], axis=1).astype(BF16)


def _attention(online, lq, q, ktc, vc, ktl=None, vl=None):
    n, nq = q.shape
    b = n // lq
    tq = ATT_TQ
    nqt = lq // tq
    gw = KV_GROUP * HEAD_DIM
    lc = ktc.shape[-1]
    in_specs = [pl.BlockSpec((tq, gw), lambda bb, j, i: (bb * nqt + i, j)),
                pl.BlockSpec((None, None, KT_ROWS, lc), lambda bb, j, i: (bb, j, 0, 0)),
                pl.BlockSpec((None, None, lc, LANES), lambda bb, j, i: (bb, j, 0, 0))]
    args = [q, ktc, vc]
    n_lat = 0
    if ktl is not None:
        ll = ktl.shape[-1]
        n_lat = ll // ATT_TK
        in_specs += [pl.BlockSpec((None, None, KT_ROWS, ll), lambda bb, j, i: (bb, j, 0, 0)),
                     pl.BlockSpec((None, None, ll, LANES), lambda bb, j, i: (bb, j, 0, 0))]
        args += [ktl, vl]
    return pl.pallas_call(
        functools.partial(_attn_kernel, online, n_lat, ATT_TK),
        grid=(b, N_KV_HEADS, nqt),
        in_specs=in_specs,
        out_specs=pl.BlockSpec((tq, gw), lambda bb, j, i: (bb * nqt + i, j)),
        out_shape=jax.ShapeDtypeStruct((n, nq), BF16),
        scratch_shapes=[pltpu.VMEM((1, 1), F32)],
        compiler_params=_cparams(("arbitrary", "arbitrary", "arbitrary")),
        name="attention",
    )(*args)


def _hyena_pre_kernel(tps, n_col, s_ref, prev_ref, next_ref, mod_ref, g_ref, w_ref, b_ref,
                      cw_ref, cb_ref, o_ref, h_ref, pre_ref):
    i = pl.program_id(0)
    tm = s_ref.shape[0]
    hb = HALO
    g = g_ref[...]
    shift = mod_ref[3:4, :]
    scale = mod_ref[4:5, :]
    h_ref[0:hb, :] = _modulated(prev_ref[...], g, shift, scale).astype(BF16)
    h_ref[hb:hb + tm, :] = _modulated(s_ref[...], g, shift, scale).astype(BF16)
    h_ref[hb + tm:, :] = _modulated(next_ref[...], g, shift, scale).astype(BF16)
    row = lax.broadcasted_iota(jnp.int32, (tm, 1), 0)
    drop_up = jnp.logical_and(row == 0, i % tps == 0)
    drop_dn = jnp.logical_and(row == tm - 1, i % tps == tps - 1)
    ct = w_ref.shape[1] // n_col
    for c in range(n_col):
        cols = slice(c * ct, (c + 1) * ct)
        pre_ref[...] = jnp.dot(h_ref[...], w_ref[:, cols], preferred_element_type=F32) + b_ref[:, cols]
        up = jnp.where(drop_up, 0.0, pre_ref[hb - 1:hb - 1 + tm, :])
        mid = pre_ref[hb:hb + tm, :]
        dn = jnp.where(drop_dn, 0.0, pre_ref[hb + 1:hb + 1 + tm, :])
        o_ref[:, cols] = (up * cw_ref[0:1, cols] + mid * cw_ref[1:2, cols]
                          + dn * cw_ref[2:3, cols] + cb_ref[:, cols])


def _hyena_pre(s, mod, g, w_in, b_in, conv_w, conv_b, seq_len, fixed_row):
    n, d = s.shape
    tm = min(PRE_TM, seq_len)
    tps = seq_len // tm
    n3 = w_in.shape[1]
    hb = HALO
    nblk = n // hb
    n_col = 6
    return pl.pallas_call(
        functools.partial(_hyena_pre_kernel, tps, n_col),
        grid=(n // tm,),
        in_specs=[pl.BlockSpec((tm, d), lambda i: (i, 0)),
                  pl.BlockSpec((hb, d), lambda i: (jnp.maximum(i * (tm // hb) - 1, 0), 0)),
                  pl.BlockSpec((hb, d), lambda i: (jnp.minimum((i + 1) * (tm // hb), nblk - 1), 0)),
                  _mod_blockspec(d, tps, fixed_row),
                  _const_spec((1, d)),
                  _const_spec(w_in.shape),
                  _const_spec((1, n3)),
                  _const_spec((3, n3)),
                  _const_spec((1, n3))],
        out_specs=pl.BlockSpec((tm, n3), lambda i: (i, 0)),
        out_shape=jax.ShapeDtypeStruct((n, n3), F32),
        scratch_shapes=[pltpu.VMEM((tm + 2 * hb, d), BF16),
                        pltpu.VMEM((tm + 2 * hb, n3 // n_col), F32)],
        compiler_params=_cparams(("arbitrary",)),
        name="hyena_pre",
    )(s, s, s, mod, g.reshape(1, d), w_in, b_in.reshape(1, n3), conv_w, conv_b.reshape(1, n3))


def _filter_kernel(n_feat, d, feat_ref, w1_ref, b1_ref, w2_ref, b2_ref, w3_ref, b3_ref,
                   wo_ref, a_ref, delta_ref, o_ref):
    hp = lax.Precision.HIGHEST
    feats = feat_ref[...]
    a = a_ref[...]
    hid = jnp.sin(a * (jnp.dot(feats, w1_ref[...], precision=hp, preferred_element_type=F32) + b1_ref[...]))
    hid = jnp.sin(a * (jnp.dot(hid, w2_ref[...], precision=hp, preferred_element_type=F32) + b2_ref[...]))
    hid = jnp.sin(a * (jnp.dot(hid, w3_ref[...], precision=hp, preferred_element_type=F32) + b3_ref[...]))
    t = feats[:, 0:1]
    use_fwd = feats[:, n_feat:n_feat + 1]
    use_bwd = feats[:, n_feat + 1:n_feat + 2]
    decay = jnp.exp(-t * delta_ref[...])
    hb = hid.astype(BF16)
    for o in range(2):
        fwd = jnp.dot(hb, wo_ref[:, (2 * o) * d:(2 * o + 1) * d], preferred_element_type=F32)
        bwd = jnp.dot(hb, wo_ref[:, (2 * o + 1) * d:(2 * o + 2) * d], preferred_element_type=F32)
        o_ref[o] = (decay * (use_fwd * fwd + use_bwd * bwd)).astype(BF16)


def _filter_features(seq_len):
    l = seq_len
    n_feat = 1 + 2 * FILTER_BANDS
    n = np.arange(2 * l)
    pos = np.where(n < l, n, 2 * l - n)
    pos = np.where(n == l, 0, pos)
    t = np.linspace(0.0, 1.0, l, dtype=np.float32)[pos]
    w = (2.0 * math.pi * pos.astype(np.float32) / l).astype(np.float32)
    bands = np.linspace(1e-4, FILTER_BANDS - 1, FILTER_BANDS, dtype=np.float32)
    bw = (bands[None, :] * w[:, None]).astype(np.float32)
    feats = np.zeros((2 * l, LANES), np.float32)
    feats[:, 0] = t
    feats[:, 1:1 + FILTER_BANDS] = np.cos(bw)
    feats[:, 1 + FILTER_BANDS:n_feat] = -np.sin(bw)
    feats[:, n_feat] = (n < l)
    feats[:, n_feat + 1] = np.logical_or(n > l, n == 0)
    return jnp.asarray(feats, F32), n_feat


def _filter_taps(seq_len, f_w1, f_b1, f_w2, f_b2, f_w3, f_b3, f_wout, f_freq):
    feats, n_feat = _filter_features(seq_len)
    fh = f_w1.shape[1]
    d = f_wout.shape[1] // 4
    w1p = jnp.zeros((LANES, fh), F32).at[:n_feat].set(f_w1)
    min_decay = math.log(DECAY_TARGET) / SLOW_DECAY_PCT
    max_decay = math.log(DECAY_TARGET) / FAST_DECAY_PCT
    deltas = jnp.abs(jnp.linspace(min_decay, max_decay, d, dtype=F32)).reshape(1, d)
    n = 2 * seq_len
    tm = min(512, n)
    return pl.pallas_call(
        functools.partial(_filter_kernel, n_feat, d),
        grid=(n // tm,),
        in_specs=[pl.BlockSpec((tm, LANES), lambda i: (i, 0)),
                  _const_spec((LANES, fh)), _const_spec((1, fh)),
                  _const_spec((fh, fh)), _const_spec((1, fh)),
                  _const_spec((fh, fh)), _const_spec((1, fh)),
                  _const_spec(f_wout.shape), _const_spec((1, fh)), _const_spec((1, d))],
        out_specs=pl.BlockSpec((2, tm, d), lambda i: (0, i, 0)),
        out_shape=jax.ShapeDtypeStruct((2, n, d), BF16),
        compiler_params=_cparams(("arbitrary",)),
        name="filter_taps",
    )(feats, w1p, f_b1.reshape(1, fh), f_w2, f_b2.reshape(1, fh), f_w3, f_b3.reshape(1, fh),
      f_wout.astype(BF16), f_freq.reshape(1, fh), deltas)


def _dft_tables(seq_len):
    n_fft = 2 * seq_len
    n1 = n_fft // FFT_N2
    k1 = jnp.arange(n1, dtype=jnp.int32)[None, :, None]
    n2 = jnp.arange(FFT_N2, dtype=jnp.int32)[:, None, None]
    nn1 = jnp.arange(n1, dtype=jnp.int32)[None, None, :]
    ang = (2.0 * math.pi / n_fft) * ((k1 * (FFT_N2 * nn1 + n2)) % n_fft).astype(F32)
    c = jnp.cos(ang)
    s = jnp.sin(ang)
    h = n1 // 2
    g_first = jnp.concatenate([jnp.concatenate([c[:, :, :h], s[:, :, :h]], axis=2),
                               jnp.concatenate([-s[:, :, :h], c[:, :, :h]], axis=2)], axis=1)
    g_last = jnp.swapaxes(g_first, 1, 2) * (1.0 / n_fft)
    g_taps = jnp.concatenate([c, -s], axis=1)
    k2 = jnp.arange(FFT_N2, dtype=jnp.int32)
    ang2 = (2.0 * math.pi / FFT_N2) * ((k2[:, None] * k2[None, :]) % FFT_N2).astype(F32)
    c2, s2 = jnp.cos(ang2), jnp.sin(ang2)
    g_mid = jnp.concatenate([jnp.concatenate([c2, s2], axis=1),
                             jnp.concatenate([-s2, c2], axis=1)], axis=0)
    return (g_first.astype(BF16), g_last.astype(BF16), g_taps.astype(BF16),
            g_mid.astype(BF16), g_mid.T.astype(BF16))


def _stage_rows(x_ref, rows_ref):
    r, nb, ct = x_ref.shape
    xf = x_ref[...].astype(F32).reshape(r * nb, ct)
    for s in range(ct // LANES):
        rows_ref[s] = xf[:, s * LANES:(s + 1) * LANES]


def _column_group(rows_ref, j, r, nb):
    return jnp.concatenate([rows_ref[s, pl.ds(j, r, stride=nb), :] for s in range(rows_ref.shape[0])], axis=1)


def _scatter_column_group(stage_ref, j, y, nb):
    for s in range(stage_ref.shape[0]):
        stage_ref[s, pl.ds(j, y.shape[0], stride=nb), :] = y[:, s * LANES:(s + 1) * LANES]


def _staged_block(stage_ref, nb):
    n_slab, rows, _ = stage_ref.shape
    return jnp.concatenate([stage_ref[s].reshape(rows // nb, nb, LANES) for s in range(n_slab)], axis=2)


def _fft_first_kernel(x_ref, g_ref, o_ref, rows_ref, stage_ref):
    r, nb, _ = x_ref.shape
    _stage_rows(x_ref, rows_ref)
    for j in range(nb):
        xj = _column_group(rows_ref, j, r, nb).astype(BF16)
        stage_ref[:, j, :] = jnp.dot(g_ref[j], xj, preferred_element_type=F32)
    o_ref[...] = stage_ref[...].astype(BF16)


def _fft_first(x3, col_block, d, g):
    r = x3.shape[0]
    rows_out = g.shape[1]
    nb, ct = FFT_NB, FFT_CT
    return pl.pallas_call(
        _fft_first_kernel,
        grid=(FFT_N2 // nb, d // ct),
        in_specs=[pl.BlockSpec((r, nb, ct), lambda i, c: (0, i, col_block * (d // ct) + c)),
                  pl.BlockSpec((nb, rows_out, r), lambda i, c: (i, 0, 0))],
        out_specs=pl.BlockSpec((rows_out, nb, ct), lambda i, c: (0, i, c)),
        out_shape=jax.ShapeDtypeStruct((rows_out, FFT_N2, d), BF16),
        scratch_shapes=[pltpu.VMEM((ct // LANES, r * nb, LANES), F32),
                        pltpu.VMEM((rows_out, nb, ct), F32)],
        compiler_params=_cparams(("arbitrary", "arbitrary")),
        name="fft_first",
    )(x3, g)


def _fft_mid_kernel(t_ref, f_ref, g_ref, gi_ref, o_ref):
    half = FFT_N2
    for kk in range(FFT_KB):
        x = jnp.dot(g_ref[...], jnp.concatenate([t_ref[0, kk], t_ref[1, kk]], axis=0),
                    preferred_element_type=F32)
        h = jnp.dot(g_ref[...], jnp.concatenate([f_ref[0, kk], f_ref[1, kk]], axis=0),
                    preferred_element_type=F32)
        xr, xi = x[:half], x[half:]
        hr, hi = h[:half], h[half:]
        z = jnp.concatenate([xr * hr - xi * hi, xr * hi + xi * hr], axis=0).astype(BF16)
        y = jnp.dot(gi_ref[...], z, preferred_element_type=F32)
        o_ref[0, kk] = y[:half].astype(BF16)
        o_ref[1, kk] = y[half:].astype(BF16)


def _fft_mid(t, f, g_mid, g_mid_inv):
    rows, _, d = t.shape
    n1 = rows // 2
    blk = pl.BlockSpec((2, FFT_KB, FFT_N2, d), lambda i: (0, i, 0, 0))
    out = pl.pallas_call(
        _fft_mid_kernel,
        grid=(n1 // FFT_KB,),
        in_specs=[blk, blk, _const_spec(g_mid.shape), _const_spec(g_mid_inv.shape)],
        out_specs=blk,
        out_shape=jax.ShapeDtypeStruct((2, n1, FFT_N2, d), BF16),
        compiler_params=_cparams(("arbitrary",)),
        name="fft_mid",
    )(t.reshape(2, n1, FFT_N2, d), f.reshape(2, n1, FFT_N2, d), g_mid, g_mid_inv)
    return out.reshape(rows, FFT_N2, d)


def _fft_last_kernel(b_ref, g_ref, z_ref, gate_ref, bias_ref, o_ref, rows_ref, stage_ref):
    rows_in, nb, _ = b_ref.shape
    _stage_rows(b_ref, rows_ref)
    for j in range(nb):
        bj = _column_group(rows_ref, j, rows_in, nb).astype(BF16)
        _scatter_column_group(stage_ref, j, jnp.dot(g_ref[j], bj, preferred_element_type=F32), nb)
    o_ref[...] = gate_ref[...] * (_staged_block(stage_ref, nb) + z_ref[...] * bias_ref[...])


def _fft_last(b, g, z3, z_col, gate3, gate_col, bias):
    rows_in, _, d = b.shape
    r = g.shape[1]
    nb, ct = FFT_NB, FFT_CT_LAST
    nc = d // ct
    return pl.pallas_call(
        _fft_last_kernel,
        grid=(FFT_N2 // nb, nc),
        in_specs=[pl.BlockSpec((rows_in, nb, ct), lambda i, c: (0, i, c)),
                  pl.BlockSpec((nb, r, rows_in), lambda i, c: (i, 0, 0)),
                  pl.BlockSpec((r, nb, ct), lambda i, c: (0, i, z_col * nc + c)),
                  pl.BlockSpec((r, nb, ct), lambda i, c: (0, i, gate_col * nc + c)),
                  pl.BlockSpec((1, 1, ct), lambda i, c: (0, 0, c))],
        out_specs=pl.BlockSpec((r, nb, ct), lambda i, c: (0, i, c)),
        out_shape=jax.ShapeDtypeStruct((r, FFT_N2, d), F32),
        scratch_shapes=[pltpu.VMEM((ct // LANES, rows_in * nb, LANES), F32),
                        pltpu.VMEM((ct // LANES, r * nb, LANES), F32)],
        compiler_params=_cparams(("arbitrary", "arbitrary")),
        name="fft_last",
    )(b, g, z3, gate3, bias.reshape(1, 1, d))


def _hyena_long_convs(u, taps, f_bias, seq_len, tables):
    n, d3 = u.shape
    d = d3 // 3
    g_first, g_last, g_taps, g_mid, g_mid_inv = tables
    n1 = 2 * seq_len // FFT_N2
    u3 = u.reshape(n1, FFT_N2, d3)
    z3, z_col = u3, 0
    for o in range(2):
        f = _fft_first(taps[o].reshape(n1, FFT_N2, d), 0, d, g_taps)
        a = _fft_first(z3, z_col, d, g_first)
        bq = _fft_mid(a, f, g_mid, g_mid_inv)
        z3 = _fft_last(bq, g_last, z3, z_col, u3, 1 + o, f_bias[o])
        z_col = 0
    return z3.reshape(n, d)


def _small_conv_kernel(u_v_ref, u_x1_ref, u_x2_ref, taps_ref, gf_ref, gt_ref, gi_ref, bias_ref, o_ref):
    nf = gf_ref.shape[0] // 2
    z = u_v_ref[...]
    gates = (u_x1_ref, u_x2_ref)
    for o in range(2):
        spec = jnp.dot(gt_ref[...], taps_ref[o].astype(BF16), preferred_element_type=F32)
        zq = jnp.dot(gf_ref[...], z.astype(BF16), preferred_element_type=F32)
        zr, zi = zq[:nf], zq[nf:]
        hr, hi = spec[:nf], spec[nf:]
        prod = jnp.concatenate([zr * hr - zi * hi, zr * hi + zi * hr], axis=0).astype(BF16)
        y = jnp.dot(gi_ref[...], prod, preferred_element_type=F32)
        z = gates[o][...] * (y + z * bias_ref[o:o + 1, :])
    o_ref[...] = z


def _small_dft_tables(seq_len):
    n_fft = 2 * seq_len
    k = jnp.arange(n_fft, dtype=jnp.int32)
    ang = (2.0 * math.pi / n_fft) * ((k[:, None] * k[None, :]) % n_fft).astype(F32)
    c, s = jnp.cos(ang), jnp.sin(ang)
    cl, sl = c[:, :seq_len], s[:, :seq_len]
    g_fwd = jnp.concatenate([jnp.concatenate([cl, sl], axis=1),
                             jnp.concatenate([-sl, cl], axis=1)], axis=0)
    g_taps = jnp.concatenate([c, -s], axis=0)
    g_inv = g_fwd.T * (1.0 / n_fft)
    return g_fwd.astype(BF16), g_taps.astype(BF16), g_inv.astype(BF16)


def _hyena_small_convs(u, taps, f_bias, seq_len):
    n, d3 = u.shape
    d = d3 // 3
    ct = 256
    nc = d // ct
    g_fwd, g_taps, g_inv = _small_dft_tables(seq_len)
    return pl.pallas_call(
        _small_conv_kernel,
        grid=(nc,),
        in_specs=[pl.BlockSpec((n, ct), lambda c: (0, c)),
                  pl.BlockSpec((n, ct), lambda c: (0, nc + c)),
                  pl.BlockSpec((n, ct), lambda c: (0, 2 * nc + c)),
                  pl.BlockSpec((2, 2 * seq_len, ct), lambda c: (0, 0, c)),
                  _const_spec(g_fwd.shape), _const_spec(g_taps.shape), _const_spec(g_inv.shape),
                  pl.BlockSpec((2, ct), lambda c: (0, c))],
        out_specs=pl.BlockSpec((n, ct), lambda c: (0, c)),
        out_shape=jax.ShapeDtypeStruct((n, d), F32),
        compiler_params=_cparams(("arbitrary",)),
        name="small_conv",
    )(u, u, u, taps, g_fwd, g_taps, g_inv, f_bias)


def kernel(x, c, ctx, c_ctx, w_mod, b_mod, norm_w, ffn_w_gate_up, ffn_w_down, attn_w_qkv, attn_w_o,
           attn_q_norm, attn_k_norm, hy_w_in, hy_b_in, hy_conv_w, hy_conv_b, hy_f_w1, hy_f_b1,
           hy_f_w2, hy_f_b2, hy_f_w3, hy_f_b3, hy_f_wout, hy_f_freq, hy_f_bias, hy_w_out, hy_b_out):
    bsz, seq, d = x.shape
    ctx_len = ctx.shape[1]
    depth = w_mod.shape[0]
    assert bsz == 2, "the long convolution packs exactly two batches into one complex sequence"
    xs = x.reshape(bsz * seq, d)
    cs = ctx.reshape(bsz * ctx_len, d)

    cc = jnp.zeros((8, d), F32).at[:bsz].set(c).at[bsz].set(c_ctx)
    mod_all = _mod_vectors(cc, w_mod, b_mod).reshape(depth, 8, N_MOD, d)
    ctx_row = bsz

    c_rows = bsz * ctx_len
    p_sum, p_bcast = _head_sum_matrices()
    cos_t, sin_t = _rope_tables(seq)
    dft_tables = _dft_tables(seq)
    zero_bias = jnp.zeros((d,), F32)

    for l in range(depth):
        mod = mod_all[l]
        is_attn = (l % 2) == 0
        ctx_out = l < depth - 1
        ctx_live = ctx_out or is_attn
        wgu0, wd0 = _prep_ffn_weights(ffn_w_gate_up[l, 0], ffn_w_down[l, 0])
        wgu1, wd1 = _prep_ffn_weights(ffn_w_gate_up[l, 1], ffn_w_down[l, 1])

        xs = _ffn(xs, mod, 0, norm_w[l, 0], wgu0, wd0, seq, None)
        if ctx_live:
            cs = _ffn(cs, mod, 0, norm_w[l, 0], wgu0, wd0, c_rows, ctx_row)

        if is_attn:
            a = l // 2
            w_qkv = attn_w_qkv[a].astype(BF16)
            w_o = attn_w_o[a].astype(BF16)
            qn = (jnp.tile(attn_q_norm[a], N_HEADS) * (HEAD_DIM ** -0.5 * math.log2(math.e))).reshape(1, -1)
            kn = jnp.tile(attn_k_norm[a], N_KV_HEADS).reshape(1, -1)
            q_l, kt_l, v_l = _attn_pre(xs, mod, norm_w[l, 1], w_qkv, qn, kn, p_sum, p_bcast,
                                       cos_t, sin_t, seq, None, True)
            q_c, kt_c, v_c = _attn_pre(cs, mod, norm_w[l, 1], w_qkv, qn, kn, p_sum, p_bcast,
                                       cos_t, sin_t, ctx_len, ctx_row, False)
            shift_ok = (HEAD_DIM * jnp.max(jnp.abs(qn)) * jnp.max(jnp.abs(kn))) < ATT_SHIFT_LIMIT
            o_l = lax.cond(shift_ok, functools.partial(_attention, False, seq),
                           functools.partial(_attention, True, seq), q_l, kt_c, v_c, kt_l, v_l)
            xs = _proj_residual(o_l, xs, mod, w_o, zero_bias, seq, None)
            if ctx_out:
                o_c = lax.cond(shift_ok, functools.partial(_attention, False, ctx_len),
                               functools.partial(_attention, True, ctx_len), q_c, kt_c, v_c)
                cs = _proj_residual(o_c, cs, mod, w_o, zero_bias, c_rows, ctx_row)
        else:
            j = l // 2
            w_in = hy_w_in[j].astype(BF16)
            w_out = hy_w_out[j].astype(BF16)
            fargs = (hy_f_w1[j], hy_f_b1[j], hy_f_w2[j], hy_f_b2[j], hy_f_w3[j], hy_f_b3[j],
                     hy_f_wout[j], hy_f_freq[j])
            u_l = _hyena_pre(xs, mod, norm_w[l, 1], w_in, hy_b_in[j], hy_conv_w[j], hy_conv_b[j], seq, None)
            y_l = _hyena_long_convs(u_l, _filter_taps(seq, *fargs), hy_f_bias[j], seq, dft_tables)
            xs_new = _proj_residual(y_l, xs, mod, w_out, hy_b_out[j], seq, None)
            if ctx_out:
                u_c = _hyena_pre(cs, mod, norm_w[l, 1], w_in, hy_b_in[j], hy_conv_w[j], hy_conv_b[j],
                                 ctx_len, ctx_row)
                y_c = _hyena_small_convs(u_c, _filter_taps(ctx_len, *fargs), hy_f_bias[j], ctx_len)
                cs = _proj_residual(y_c, cs, mod, w_out, hy_b_out[j], c_rows, ctx_row)
            xs = xs_new

        xs = _ffn(xs, mod, 2, norm_w[l, 2], wgu1, wd1, seq, None)
        if ctx_out:
            cs = _ffn(cs, mod, 2, norm_w[l, 2], wgu1, wd1, c_rows, ctx_row)
    return xs.reshape(bsz, seq, d)
```

```python
import functools
import math

import jax
import jax.numpy as jnp
import numpy as np
from jax import lax
from jax.experimental import pallas as pl
from jax.experimental.pallas import tpu as pltpu

F32 = jnp.float32
BF16 = jnp.bfloat16

N_MOD = 9
N_HEADS = 16
N_KV_HEADS = 4
HEAD_DIM = 64
KV_GROUP = N_HEADS // N_KV_HEADS
KT_ROWS = 2 * HEAD_DIM
ATT_SHIFT_LIMIT = 60.0
GRID_W = 64
ROPE_THETA = 10000.0
EPS = 1e-6
FILTER_BANDS = 16
N_FILTER_FEAT = 1 + 2 * FILTER_BANDS
DECAY_TARGET = 1e-2
FAST_DECAY_PCT = 0.3
SLOW_DECAY_PCT = 1.5

LANES = 128
HALO = 16
FFT_N2 = 128
VMEM_LIMIT = 56 * 1024 * 1024

FFN_TM = 1024
PROJ_TM = 512
PRE_TM = 256
FFN_FC = 256
ATT_TQ = 256
ATT_TK = 512
FFT_NB = 16
FFT_KB = 8
FFT_CT = 512
FFT_CT_LAST = 256


def _cparams(sem):
    return pltpu.CompilerParams(dimension_semantics=sem, vmem_limit_bytes=VMEM_LIMIT)


def _const_spec(shape):
    nd = len(shape)
    return pl.BlockSpec(shape, lambda *_: (0,) * nd, pipeline_mode=pl.Buffered(1))


def _modulated(s, g, shift, scale):
    ms = jnp.mean(s * s, axis=-1, keepdims=True)
    return (s * lax.rsqrt(ms + EPS) * g) * (1.0 + scale) + shift


def _mod_kernel(c_ref, w_ref, b_ref, o_ref):
    c = c_ref[...]
    a = (c * jax.nn.sigmoid(c)).astype(BF16)
    o_ref[0] = jnp.dot(a, w_ref[0].astype(BF16), preferred_element_type=F32) + b_ref[0]


def _mod_vectors(cc, w_mod, b_mod):
    depth, d, n = w_mod.shape
    tn = n // 6
    return pl.pallas_call(
        _mod_kernel,
        grid=(depth, n // tn),
        in_specs=[pl.BlockSpec((8, d), lambda l, j: (0, 0)),
                  pl.BlockSpec((1, d, tn), lambda l, j: (l, 0, j)),
                  pl.BlockSpec((1, 1, tn), lambda l, j: (l, 0, j))],
        out_specs=pl.BlockSpec((1, 8, tn), lambda l, j: (l, 0, j)),
        out_shape=jax.ShapeDtypeStruct((depth, 8, n), F32),
        compiler_params=_cparams(("arbitrary", "arbitrary")),
        name="mod_vectors",
    )(cc, w_mod, b_mod.reshape(depth, 1, n))


def _mod_blockspec(d, tiles_per_row, fixed_row):
    if fixed_row is None:
        return pl.BlockSpec((None, N_MOD, d), lambda i: (i // tiles_per_row, 0, 0))
    return pl.BlockSpec((None, N_MOD, d), lambda i: (fixed_row, 0, 0))


def _ffn_kernel(k, s_ref, mod_ref, g_ref, wgu_ref, wd_ref, o_ref, h_ref, acc_ref):
    s = s_ref[...]
    shift = mod_ref[3 * k:3 * k + 1, :]
    scale = mod_ref[3 * k + 1:3 * k + 2, :]
    gate = mod_ref[3 * k + 2:3 * k + 3, :]
    h_ref[...] = _modulated(s, g_ref[...], shift, scale).astype(BF16)
    f = wd_ref.shape[0]
    for c in range(f // FFN_FC):
        lo = c * FFN_FC
        g = jnp.dot(h_ref[...], wgu_ref[:, lo:lo + FFN_FC], preferred_element_type=F32)
        u = jnp.dot(h_ref[...], wgu_ref[:, f + lo:f + lo + FFN_FC], preferred_element_type=F32)
        a = (g * jax.nn.sigmoid(g) * u).astype(BF16)
        y = jnp.dot(a, wd_ref[lo:lo + FFN_FC, :], preferred_element_type=F32)
        if c == 0:
            acc_ref[...] = y
        else:
            acc_ref[...] += y
    o_ref[...] = s + 0.5 * gate * acc_ref[...]


def _ffn(s, mod, k, g, wgu, wd, rows_per_cond, fixed_row):
    n, d = s.shape
    tm = min(FFN_TM, n)
    tiles_per_row = rows_per_cond // tm
    return pl.pallas_call(
        functools.partial(_ffn_kernel, k),
        grid=(n // tm,),
        in_specs=[pl.BlockSpec((tm, d), lambda i: (i, 0)),
                  _mod_blockspec(d, tiles_per_row, fixed_row),
                  _const_spec((1, d)),
                  _const_spec(wgu.shape),
                  _const_spec(wd.shape)],
        out_specs=pl.BlockSpec((tm, d), lambda i: (i, 0)),
        out_shape=jax.ShapeDtypeStruct((n, d), F32),
        scratch_shapes=[pltpu.VMEM((tm, d), BF16), pltpu.VMEM((tm, d), F32)],
        compiler_params=_cparams(("arbitrary",)),
        name="ffn",
    )(s, mod, g.reshape(1, d), wgu, wd)


def _prep_ffn_weights(w_gate_up, w_down):
    assert w_down.shape[0] % FFN_FC == 0
    return w_gate_up.astype(BF16), w_down.astype(BF16)


def _proj_kernel(a_ref, s_ref, mod_ref, w_ref, b_ref, o_ref):
    gate = mod_ref[5:6, :]
    y = jnp.dot(a_ref[...].astype(BF16), w_ref[...], preferred_element_type=F32) + b_ref[...]
    o_ref[...] = s_ref[...] + gate * y


def _proj_residual(a, s, mod, w, b, rows_per_cond, fixed_row):
    n, d = s.shape
    tm = min(PROJ_TM, n)
    tiles_per_row = rows_per_cond // tm
    return pl.pallas_call(
        _proj_kernel,
        grid=(n // tm,),
        in_specs=[pl.BlockSpec((tm, a.shape[1]), lambda i: (i, 0)),
                  pl.BlockSpec((tm, d), lambda i: (i, 0)),
                  _mod_blockspec(d, tiles_per_row, fixed_row),
                  _const_spec(w.shape),
                  _const_spec((1, d))],
        out_specs=pl.BlockSpec((tm, d), lambda i: (i, 0)),
        out_shape=jax.ShapeDtypeStruct((n, d), F32),
        compiler_params=_cparams(("arbitrary",)),
        name="proj_residual",
    )(a, s, mod, w, b.reshape(1, d))


def _head_norm(x, p, pt, w):
    sq = x * x
    hi = sq.astype(BF16)
    lo = (sq - hi.astype(F32)).astype(BF16)
    ss = jnp.dot(hi, p, preferred_element_type=F32) + jnp.dot(lo, p, preferred_element_type=F32)
    r = lax.rsqrt(ss * (1.0 / HEAD_DIM) + EPS)
    rh = r.astype(BF16)
    rl = (r - rh.astype(F32)).astype(BF16)
    rb = jnp.dot(rh, pt, preferred_element_type=F32) + jnp.dot(rl, pt, preferred_element_type=F32)
    return x * rb * w


def _rope_block(xb, cos, sin):
    lane = lax.broadcasted_iota(jnp.int32, xb.shape, 1)
    fwd = pltpu.roll(xb, 16, axis=1)
    bwd = pltpu.roll(xb, LANES - 16, axis=1)
    partner = jnp.where((lane % 32) < 16, bwd, fwd)
    return xb * cos + partner * sin


def _attn_pre_kernel(rope, s_ref, mod_ref, g_ref, w_ref, qn_ref, kn_ref, p_ref, pt_ref,
                     cos_ref, sin_ref, q_ref, kt_ref, v_ref):
    s = s_ref[...]
    tm = s.shape[0]
    nq = N_HEADS * HEAD_DIM
    nk = N_KV_HEADS * HEAD_DIM
    h = _modulated(s, g_ref[...], mod_ref[3:4, :], mod_ref[4:5, :]).astype(BF16)
    qkv = jnp.dot(h, w_ref[...], preferred_element_type=F32)
    q = _head_norm(qkv[:, :nq], p_ref[...], pt_ref[...], qn_ref[...])
    k = _head_norm(qkv[:, nq:nq + nk], p_ref[:nk, :], pt_ref[:, :nk], kn_ref[...])
    v = qkv[:, nq + nk:]
    cos = cos_ref[...]
    sin = sin_ref[...]
    for j in range(nq // LANES):
        qb = q[:, j * LANES:(j + 1) * LANES]
        if rope:
            qb = _rope_block(qb, cos, sin)
        q_ref[:, j * LANES:(j + 1) * LANES] = qb.astype(BF16)
    kblocks = []
    for j in range(nk // LANES):
        kb = k[:, j * LANES:(j + 1) * LANES]
        if rope:
            kb = _rope_block(kb, cos, sin)
        kblocks.append(kb)
    kt = jnp.concatenate(kblocks, axis=1).T.astype(BF16)
    tail_row = lax.broadcasted_iota(jnp.int32, (KT_ROWS - HEAD_DIM, tm), 0)
    tail = jnp.where(tail_row == 0, 1.0, 0.0).astype(BF16)
    for hh in range(N_KV_HEADS):
        kt_ref[hh] = jnp.concatenate([kt[hh * HEAD_DIM:(hh + 1) * HEAD_DIM, :], tail], axis=0)
    lane = lax.broadcasted_iota(jnp.int32, (tm, LANES), 1)
    ones_col = jnp.where(lane == HEAD_DIM, 1.0, 0.0)
    for hh in range(N_KV_HEADS):
        vb = v[:, (hh // 2) * LANES:(hh // 2 + 1) * LANES]
        if hh % 2 == 1:
            vb = pltpu.roll(vb, HEAD_DIM, axis=1)
        v_ref[hh] = jnp.where(lane < HEAD_DIM, vb, ones_col).astype(BF16)


def _attn_pre(s, mod, g, w_qkv, qn, kn, p, pt, cos, sin, seq_len, fixed_row, rope):
    n, d = s.shape
    tm = min(PRE_TM, seq_len)
    tps = seq_len // tm
    n_seq = n // seq_len
    nq = N_HEADS * HEAD_DIM
    return pl.pallas_call(
        functools.partial(_attn_pre_kernel, rope),
        grid=(n // tm,),
        in_specs=[pl.BlockSpec((tm, d), lambda i: (i, 0)),
                  _mod_blockspec(d, tps, fixed_row),
                  _const_spec((1, d)),
                  _const_spec(w_qkv.shape),
                  _const_spec((1, nq)),
                  _const_spec((1, N_KV_HEADS * HEAD_DIM)),
                  _const_spec(p.shape),
                  _const_spec(pt.shape),
                  pl.BlockSpec((tm, LANES), lambda i: (i % tps, 0)),
                  pl.BlockSpec((tm, LANES), lambda i: (i % tps, 0))],
        out_specs=[pl.BlockSpec((tm, nq), lambda i: (i, 0)),
                   pl.BlockSpec((None, N_KV_HEADS, KT_ROWS, tm), lambda i: (i // tps, 0, 0, i % tps)),
                   pl.BlockSpec((None, N_KV_HEADS, tm, LANES), lambda i: (i // tps, 0, i % tps, 0))],
        out_shape=[jax.ShapeDtypeStruct((n, nq), BF16),
                   jax.ShapeDtypeStruct((n_seq, N_KV_HEADS, KT_ROWS, seq_len), BF16),
                   jax.ShapeDtypeStruct((n_seq, N_KV_HEADS, seq_len, LANES), BF16)],
        compiler_params=_cparams(("arbitrary",)),
        name="attn_pre",
    )(s, mod, g.reshape(1, d), w_qkv, qn, kn, p, pt, cos, sin)


def _rope_tables(seq_len):
    half = HEAD_DIM // 4
    t = np.arange(seq_len)
    pos = np.stack([t // GRID_W, t % GRID_W], axis=1).astype(np.float32)
    freqs = (ROPE_THETA ** (-np.arange(half, dtype=np.float32) / half)).astype(np.float32)
    lane = np.arange(LANES)
    axis = (lane % HEAD_DIM) // (HEAD_DIM // 2)
    e = lane % (HEAD_DIM // 2)
    ang = pos[:, axis] * freqs[e % half][None, :]
    sign = np.where(e < half, -1.0, 1.0).astype(np.float32)
    return jnp.asarray(np.cos(ang), F32), jnp.asarray(np.sin(ang) * sign[None, :], F32)


def _head_sum_matrices():
    lane = np.arange(N_HEADS * HEAD_DIM)
    p = (lane[:, None] // HEAD_DIM == np.arange(LANES)[None, :]).astype(np.float32)
    return jnp.asarray(p, BF16), jnp.asarray(p.T, BF16)


def _max_key_norm(kt_ref):
    k = kt_ref[:HEAD_DIM, :].astype(F32)
    return jnp.sqrt(jnp.max(jnp.sum(k * k, axis=0, keepdims=True), axis=1, keepdims=True))


def _attn_kernel(online, n_lat_chunks, tk, q_ref, ktc_ref, vc_ref, *rest):
    if n_lat_chunks:
        ktl_ref, vl_ref, o_ref, kmax_ref = rest
    else:
        o_ref, kmax_ref = rest
    if not online:
        @pl.when(pl.program_id(2) == 0)
        def _():
            kmax = _max_key_norm(ktc_ref)
            if n_lat_chunks:
                kmax = jnp.maximum(kmax, _max_key_norm(ktl_ref))
            kmax_ref[...] = kmax
    qf = q_ref[...].astype(F32)
    tq = qf.shape[0]
    lane = lax.broadcasted_iota(jnp.int32, (tq, LANES), 1)
    parts = []
    for g in range(KV_GROUP):
        blk = qf[:, (g // 2) * LANES:(g // 2 + 1) * LANES]
        if g % 2 == 1:
            blk = pltpu.roll(blk, HEAD_DIM, axis=1)
        qg = jnp.where(lane < HEAD_DIM, blk, 0.0)
        if not online:
            shift = jnp.sqrt(jnp.sum(qg * qg, axis=1, keepdims=True)) * kmax_ref[...]
            qg = jnp.where(lane == HEAD_DIM, -shift, qg)
        parts.append(qg.astype(BF16))
    q4 = jnp.concatenate(parts, axis=0)

    if online:
        def step(kt, v, carry):
            m, acc = carry
            s = jnp.dot(q4, kt, preferred_element_type=F32)
            m_new = jnp.maximum(m, jnp.max(s, axis=-1, keepdims=True))
            p = jnp.exp2(s - m_new)
            acc = jnp.exp2(m - m_new) * acc + jnp.dot(p.astype(BF16), v, preferred_element_type=F32)
            return m_new, acc
        carry = (jnp.full((KV_GROUP * tq, 1), -1e30, F32), jnp.zeros((KV_GROUP * tq, LANES), F32))
    else:
        def step(kt, v, acc):
            p = jnp.exp2(jnp.dot(q4, kt, preferred_element_type=F32))
            return acc + jnp.dot(p.astype(BF16), v, preferred_element_type=F32)
        carry = jnp.zeros((KV_GROUP * tq, LANES), F32)

    carry = step(ktc_ref[...], vc_ref[...], carry)
    if n_lat_chunks:
        def body(c, carry):
            off = pl.multiple_of(c * tk, tk)
            return step(ktl_ref[:, pl.ds(off, tk)], vl_ref[pl.ds(off, tk), :], carry)
        carry = lax.fori_loop(0, n_lat_chunks, body, carry, unroll=8 if online else True)
    acc = carry[1] if online else carry
    o = acc[:, :HEAD_DIM] / acc[:, HEAD_DIM:HEAD_DIM + 1]
    o_ref[...] = jnp.concatenate([o[g * tq:(g + 1) * tq, :] for g in range(KV_GROUP)], axis=1).astype(BF16)


def _attention(online, lq, q, ktc, vc, ktl=None, vl=None):
    n, nq = q.shape
    b = n // lq
    tq = ATT_TQ
    nqt = lq // tq
    gw = KV_GROUP * HEAD_DIM
    lc = ktc.shape[-1]
    in_specs = [pl.BlockSpec((tq, gw), lambda bb, j, i: (bb * nqt + i, j)),
                pl.BlockSpec((None, None, KT_ROWS, lc), lambda bb, j, i: (bb, j, 0, 0)),
                pl.BlockSpec((None, None, lc, LANES), lambda bb, j, i: (bb, j, 0, 0))]
    args = [q, ktc, vc]
    n_lat = 0
    if ktl is not None:
        ll = ktl.shape[-1]
        n_lat = ll // ATT_TK
        in_specs += [pl.BlockSpec((None, None, KT_ROWS, ll), lambda bb, j, i: (bb, j, 0, 0)),
                     pl.BlockSpec((None, None, ll, LANES), lambda bb, j, i: (bb, j, 0, 0))]
        args += [ktl, vl]
    return pl.pallas_call(
        functools.partial(_attn_kernel, online, n_lat, ATT_TK),
        grid=(b, N_KV_HEADS, nqt),
        in_specs=in_specs,
        out_specs=pl.BlockSpec((tq, gw), lambda bb, j, i: (bb * nqt + i, j)),
        out_shape=jax.ShapeDtypeStruct((n, nq), BF16),
        scratch_shapes=[pltpu.VMEM((1, 1), F32)],
        compiler_params=_cparams(("arbitrary", "arbitrary", "arbitrary")),
        name="attention",
    )(*args)


def _hyena_pre_kernel(tps, n_col, s_ref, prev_ref, next_ref, mod_ref, g_ref, w_ref, b_ref,
                      cw_ref, cb_ref, o_ref, h_ref, pre_ref):
    i = pl.program_id(0)
    tm = s_ref.shape[0]
    hb = HALO
    g = g_ref[...]
    shift = mod_ref[3:4, :]
    scale = mod_ref[4:5, :]
    h_ref[0:hb, :] = _modulated(prev_ref[...], g, shift, scale).astype(BF16)
    h_ref[hb:hb + tm, :] = _modulated(s_ref[...], g, shift, scale).astype(BF16)
    h_ref[hb + tm:, :] = _modulated(next_ref[...], g, shift, scale).astype(BF16)
    row = lax.broadcasted_iota(jnp.int32, (tm, 1), 0)
    drop_up = jnp.logical_and(row == 0, i % tps == 0)
    drop_dn = jnp.logical_and(row == tm - 1, i % tps == tps - 1)
    ct = w_ref.shape[1] // n_col
    for c in range(n_col):
        cols = slice(c * ct, (c + 1) * ct)
        pre_ref[...] = jnp.dot(h_ref[...], w_ref[:, cols], preferred_element_type=F32) + b_ref[:, cols]
        up = jnp.where(drop_up, 0.0, pre_ref[hb - 1:hb - 1 + tm, :])
        mid = pre_ref[hb:hb + tm, :]
        dn = jnp.where(drop_dn, 0.0, pre_ref[hb + 1:hb + 1 + tm, :])
        o_ref[:, cols] = (up * cw_ref[0:1, cols] + mid * cw_ref[1:2, cols]
                          + dn * cw_ref[2:3, cols] + cb_ref[:, cols])


def _hyena_pre(s, mod, g, w_in, b_in, conv_w, conv_b, seq_len, fixed_row):
    n, d = s.shape
    tm = min(PRE_TM, seq_len)
    tps = seq_len // tm
    n3 = w_in.shape[1]
    hb = HALO
    nblk = n // hb
    n_col = 6
    return pl.pallas_call(
        functools.partial(_hyena_pre_kernel, tps, n_col),
        grid=(n // tm,),
        in_specs=[pl.BlockSpec((tm, d), lambda i: (i, 0)),
                  pl.BlockSpec((hb, d), lambda i: (jnp.maximum(i * (tm // hb) - 1, 0), 0)),
                  pl.BlockSpec((hb, d), lambda i: (jnp.minimum((i + 1) * (tm // hb), nblk - 1), 0)),
                  _mod_blockspec(d, tps, fixed_row),
                  _const_spec((1, d)),
                  _const_spec(w_in.shape),
                  _const_spec((1, n3)),
                  _const_spec((3, n3)),
                  _const_spec((1, n3))],
        out_specs=pl.BlockSpec((tm, n3), lambda i: (i, 0)),
        out_shape=jax.ShapeDtypeStruct((n, n3), F32),
        scratch_shapes=[pltpu.VMEM((tm + 2 * hb, d), BF16),
                        pltpu.VMEM((tm + 2 * hb, n3 // n_col), F32)],
        compiler_params=_cparams(("arbitrary",)),
        name="hyena_pre",
    )(s, s, s, mod, g.reshape(1, d), w_in, b_in.reshape(1, n3), conv_w, conv_b.reshape(1, n3))


def _filter_kernel(feat_ref, w1_ref, b1_ref, w2_ref, b2_ref, w3_ref, b3_ref, a_ref, o_ref):
    hp = lax.Precision.HIGHEST
    a = a_ref[...]
    hid = jnp.sin(a * (jnp.dot(feat_ref[...], w1_ref[...], precision=hp, preferred_element_type=F32) + b1_ref[...]))
    hid = jnp.sin(a * (jnp.dot(hid, w2_ref[...], precision=hp, preferred_element_type=F32) + b2_ref[...]))
    o_ref[...] = jnp.sin(a * (jnp.dot(hid, w3_ref[...], precision=hp, preferred_element_type=F32) + b3_ref[...]))


def _taps_from_hidden(hid, feats, w_fwd, w_bwd, delta):
    hb = hid.astype(BF16)
    t = feats[:, 0:1]
    use_fwd = feats[:, N_FILTER_FEAT:N_FILTER_FEAT + 1]
    use_bwd = feats[:, N_FILTER_FEAT + 1:N_FILTER_FEAT + 2]
    fwd = jnp.dot(hb, w_fwd, preferred_element_type=F32)
    bwd = jnp.dot(hb, w_bwd, preferred_element_type=F32)
    return (jnp.exp(-t * delta) * (use_fwd * fwd + use_bwd * bwd)).astype(BF16)


def _filter_features(seq_len):
    l = seq_len
    n_feat = N_FILTER_FEAT
    n = np.arange(2 * l)
    pos = np.where(n < l, n, 2 * l - n)
    pos = np.where(n == l, 0, pos)
    t = np.linspace(0.0, 1.0, l, dtype=np.float32)[pos]
    w = (2.0 * math.pi * pos.astype(np.float32) / l).astype(np.float32)
    bands = np.linspace(1e-4, FILTER_BANDS - 1, FILTER_BANDS, dtype=np.float32)
    bw = (bands[None, :] * w[:, None]).astype(np.float32)
    feats = np.zeros((2 * l, LANES), np.float32)
    feats[:, 0] = t
    feats[:, 1:1 + FILTER_BANDS] = np.cos(bw)
    feats[:, 1 + FILTER_BANDS:n_feat] = -np.sin(bw)
    feats[:, n_feat] = (n < l)
    feats[:, n_feat + 1] = np.logical_or(n > l, n == 0)
    return jnp.asarray(feats, F32)


def _filter_hidden(feats, f_w1, f_b1, f_w2, f_b2, f_w3, f_b3, f_freq):
    n = feats.shape[0]
    fh = f_w1.shape[1]
    w1p = jnp.zeros((LANES, fh), F32).at[:N_FILTER_FEAT].set(f_w1)
    tm = min(512, n)
    return pl.pallas_call(
        _filter_kernel,
        grid=(n // tm,),
        in_specs=[pl.BlockSpec((tm, LANES), lambda i: (i, 0)),
                  _const_spec((LANES, fh)), _const_spec((1, fh)),
                  _const_spec((fh, fh)), _const_spec((1, fh)),
                  _const_spec((fh, fh)), _const_spec((1, fh)),
                  _const_spec((1, fh))],
        out_specs=pl.BlockSpec((tm, fh), lambda i: (i, 0)),
        out_shape=jax.ShapeDtypeStruct((n, fh), F32),
        compiler_params=_cparams(("arbitrary",)),
        name="filter_hidden",
    )(feats, w1p, f_b1.reshape(1, fh), f_w2, f_b2.reshape(1, fh), f_w3, f_b3.reshape(1, fh),
      f_freq.reshape(1, fh))


def _decay_rates(d):
    min_decay = math.log(DECAY_TARGET) / SLOW_DECAY_PCT
    max_decay = math.log(DECAY_TARGET) / FAST_DECAY_PCT
    return jnp.abs(jnp.linspace(min_decay, max_decay, d, dtype=F32)).reshape(1, d)


def _dft_tables(seq_len):
    n_fft = 2 * seq_len
    n1 = n_fft // FFT_N2
    k1 = jnp.arange(n1, dtype=jnp.int32)[None, :, None]
    n2 = jnp.arange(FFT_N2, dtype=jnp.int32)[:, None, None]
    nn1 = jnp.arange(n1, dtype=jnp.int32)[None, None, :]
    ang = (2.0 * math.pi / n_fft) * ((k1 * (FFT_N2 * nn1 + n2)) % n_fft).astype(F32)
    c = jnp.cos(ang)
    s = jnp.sin(ang)
    h = n1 // 2
    g_first = jnp.concatenate([jnp.concatenate([c[:, :, :h], s[:, :, :h]], axis=2),
                               jnp.concatenate([-s[:, :, :h], c[:, :, :h]], axis=2)], axis=1)
    g_last = jnp.swapaxes(g_first, 1, 2) * (1.0 / n_fft)
    g_taps = jnp.concatenate([c, -s], axis=1)
    k2 = jnp.arange(FFT_N2, dtype=jnp.int32)
    ang2 = (2.0 * math.pi / FFT_N2) * ((k2[:, None] * k2[None, :]) % FFT_N2).astype(F32)
    c2, s2 = jnp.cos(ang2), jnp.sin(ang2)
    g_mid = jnp.concatenate([jnp.concatenate([c2, s2], axis=1),
                             jnp.concatenate([-s2, c2], axis=1)], axis=0)
    return (g_first.astype(BF16), g_last.astype(BF16), g_taps.astype(BF16),
            g_mid.astype(BF16), g_mid.T.astype(BF16))


def _stage_rows(x, rows_ref):
    for s in range(rows_ref.shape[0]):
        rows_ref[s] = x[:, s * LANES:(s + 1) * LANES]


def _column_group(rows_ref, j, r, nb):
    return jnp.concatenate([rows_ref[s, pl.ds(j, r, stride=nb), :] for s in range(rows_ref.shape[0])], axis=1)


def _scatter_column_group(stage_ref, j, y, nb):
    for s in range(stage_ref.shape[0]):
        stage_ref[s, pl.ds(j, y.shape[0], stride=nb), :] = y[:, s * LANES:(s + 1) * LANES]


def _staged_block(stage_ref, nb):
    n_slab, rows, _ = stage_ref.shape
    return jnp.concatenate([stage_ref[s].reshape(rows // nb, nb, LANES) for s in range(n_slab)], axis=2)


def _first_stage_dft(rows_ref, g_ref, stage_ref, r, nb):
    for j in range(nb):
        xj = _column_group(rows_ref, j, r, nb).astype(BF16)
        stage_ref[:, j, :] = jnp.dot(g_ref[j], xj, preferred_element_type=F32)
    return stage_ref[...].astype(BF16)


def _fft_first_kernel(x_ref, g_ref, o_ref, rows_ref, stage_ref):
    r, nb, ct = x_ref.shape
    _stage_rows(x_ref[...].reshape(r * nb, ct), rows_ref)
    o_ref[...] = _first_stage_dft(rows_ref, g_ref, stage_ref, r, nb)


def _fft_first(x3, col_block, d, g):
    r = x3.shape[0]
    rows_out = g.shape[1]
    nb, ct = FFT_NB, FFT_CT
    return pl.pallas_call(
        _fft_first_kernel,
        grid=(FFT_N2 // nb, d // ct),
        in_specs=[pl.BlockSpec((r, nb, ct), lambda i, c: (0, i, col_block * (d // ct) + c)),
                  pl.BlockSpec((nb, rows_out, r), lambda i, c: (i, 0, 0))],
        out_specs=pl.BlockSpec((rows_out, nb, ct), lambda i, c: (0, i, c)),
        out_shape=jax.ShapeDtypeStruct((rows_out, FFT_N2, d), BF16),
        scratch_shapes=[pltpu.VMEM((ct // LANES, r * nb, LANES), F32),
                        pltpu.VMEM((rows_out, nb, ct), F32)],
        compiler_params=_cparams(("arbitrary", "arbitrary")),
        name="fft_first",
    )(x3, g)


def _fft_taps_kernel(ct, h_ref, feat_ref, wo_ref, delta_ref, g_ref, o_ref, rows_ref, stage_ref):
    r, nb, fh = h_ref.shape
    d = wo_ref.shape[1] // 4
    col0 = pl.program_id(1) * ct
    hid = h_ref[...].reshape(r * nb, fh)
    feats = feat_ref[...].reshape(r * nb, LANES)
    for o in range(2):
        w_fwd = wo_ref[:, pl.ds(pl.multiple_of(2 * o * d + col0, ct), ct)]
        w_bwd = wo_ref[:, pl.ds(pl.multiple_of((2 * o + 1) * d + col0, ct), ct)]
        taps = _taps_from_hidden(hid, feats, w_fwd, w_bwd, delta_ref[...])
        _stage_rows(taps.astype(F32), rows_ref)
        o_ref[o] = _first_stage_dft(rows_ref, g_ref, stage_ref, r, nb)


def _fft_taps(hid, feats, w_out, deltas, g):
    n, fh = hid.shape
    n1 = n // FFT_N2
    d = deltas.shape[1]
    rows_out = g.shape[1]
    nb, ct = FFT_NB, FFT_CT
    return pl.pallas_call(
        functools.partial(_fft_taps_kernel, ct),
        grid=(FFT_N2 // nb, d // ct),
        in_specs=[pl.BlockSpec((n1, nb, fh), lambda i, c: (0, i, 0)),
                  pl.BlockSpec((n1, nb, LANES), lambda i, c: (0, i, 0)),
                  _const_spec(w_out.shape),
                  pl.BlockSpec((1, ct), lambda i, c: (0, c)),
                  pl.BlockSpec((nb, rows_out, n1), lambda i, c: (i, 0, 0))],
        out_specs=pl.BlockSpec((2, rows_out, nb, ct), lambda i, c: (0, 0, i, c)),
        out_shape=jax.ShapeDtypeStruct((2, rows_out, FFT_N2, d), BF16),
        scratch_shapes=[pltpu.VMEM((ct // LANES, n1 * nb, LANES), F32),
                        pltpu.VMEM((rows_out, nb, ct), F32)],
        compiler_params=_cparams(("arbitrary", "arbitrary")),
        name="fft_taps",
    )(hid.reshape(n1, FFT_N2, fh), feats.reshape(n1, FFT_N2, LANES), w_out, deltas, g)


def _fft_mid_kernel(t_ref, f_ref, g_ref, gi_ref, o_ref):
    half = FFT_N2
    for kk in range(FFT_KB):
        x = jnp.dot(g_ref[...], jnp.concatenate([t_ref[0, kk], t_ref[1, kk]], axis=0),
                    preferred_element_type=F32)
        h = jnp.dot(g_ref[...], jnp.concatenate([f_ref[0, kk], f_ref[1, kk]], axis=0),
                    preferred_element_type=F32)
        xr, xi = x[:half], x[half:]
        hr, hi = h[:half], h[half:]
        z = jnp.concatenate([xr * hr - xi * hi, xr * hi + xi * hr], axis=0).astype(BF16)
        y = jnp.dot(gi_ref[...], z, preferred_element_type=F32)
        o_ref[0, kk] = y[:half].astype(BF16)
        o_ref[1, kk] = y[half:].astype(BF16)


def _fft_mid(t, f, g_mid, g_mid_inv):
    rows, _, d = t.shape
    n1 = rows // 2
    blk = pl.BlockSpec((2, FFT_KB, FFT_N2, d), lambda i: (0, i, 0, 0))
    out = pl.pallas_call(
        _fft_mid_kernel,
        grid=(n1 // FFT_KB,),
        in_specs=[blk, blk, _const_spec(g_mid.shape), _const_spec(g_mid_inv.shape)],
        out_specs=blk,
        out_shape=jax.ShapeDtypeStruct((2, n1, FFT_N2, d), BF16),
        compiler_params=_cparams(("arbitrary",)),
        name="fft_mid",
    )(t.reshape(2, n1, FFT_N2, d), f.reshape(2, n1, FFT_N2, d), g_mid, g_mid_inv)
    return out.reshape(rows, FFT_N2, d)


def _fft_last_kernel(b_ref, g_ref, z_ref, gate_ref, bias_ref, o_ref, rows_ref, stage_ref):
    rows_in, nb, ct = b_ref.shape
    _stage_rows(b_ref[...].astype(F32).reshape(rows_in * nb, ct), rows_ref)
    for j in range(nb):
        bj = _column_group(rows_ref, j, rows_in, nb).astype(BF16)
        _scatter_column_group(stage_ref, j, jnp.dot(g_ref[j], bj, preferred_element_type=F32), nb)
    o_ref[...] = gate_ref[...] * (_staged_block(stage_ref, nb) + z_ref[...] * bias_ref[...])


def _fft_last(b, g, z3, z_col, gate3, gate_col, bias):
    rows_in, _, d = b.shape
    r = g.shape[1]
    nb, ct = FFT_NB, FFT_CT_LAST
    nc = d // ct
    return pl.pallas_call(
        _fft_last_kernel,
        grid=(FFT_N2 // nb, nc),
        in_specs=[pl.BlockSpec((rows_in, nb, ct), lambda i, c: (0, i, c)),
                  pl.BlockSpec((nb, r, rows_in), lambda i, c: (i, 0, 0)),
                  pl.BlockSpec((r, nb, ct), lambda i, c: (0, i, z_col * nc + c)),
                  pl.BlockSpec((r, nb, ct), lambda i, c: (0, i, gate_col * nc + c)),
                  pl.BlockSpec((1, 1, ct), lambda i, c: (0, 0, c))],
        out_specs=pl.BlockSpec((r, nb, ct), lambda i, c: (0, i, c)),
        out_shape=jax.ShapeDtypeStruct((r, FFT_N2, d), F32),
        scratch_shapes=[pltpu.VMEM((ct // LANES, rows_in * nb, LANES), F32),
                        pltpu.VMEM((ct // LANES, r * nb, LANES), F32)],
        compiler_params=_cparams(("arbitrary", "arbitrary")),
        name="fft_last",
    )(b, g, z3, gate3, bias.reshape(1, 1, d))


def _hyena_long_convs(u, hid, feats, w_out, f_bias, seq_len, tables):
    n, d3 = u.shape
    d = d3 // 3
    g_first, g_last, g_taps, g_mid, g_mid_inv = tables
    n1 = 2 * seq_len // FFT_N2
    u3 = u.reshape(n1, FFT_N2, d3)
    f = _fft_taps(hid, feats, w_out, _decay_rates(d), g_taps)
    z3, z_col = u3, 0
    for o in range(2):
        a = _fft_first(z3, z_col, d, g_first)
        bq = _fft_mid(a, f[o], g_mid, g_mid_inv)
        z3 = _fft_last(bq, g_last, z3, z_col, u3, 1 + o, f_bias[o])
        z_col = 0
    return z3.reshape(n, d)


def _small_conv_kernel(ct, u_v_ref, u_x1_ref, u_x2_ref, h_ref, feat_ref, wo_ref, delta_ref,
                       gf_ref, gt_ref, gi_ref, bias_ref, o_ref):
    nf = gf_ref.shape[0] // 2
    d = wo_ref.shape[1] // 4
    col0 = pl.program_id(0) * ct
    z = u_v_ref[...]
    gates = (u_x1_ref, u_x2_ref)
    for o in range(2):
        w_fwd = wo_ref[:, pl.ds(pl.multiple_of(2 * o * d + col0, ct), ct)]
        w_bwd = wo_ref[:, pl.ds(pl.multiple_of((2 * o + 1) * d + col0, ct), ct)]
        taps = _taps_from_hidden(h_ref[...], feat_ref[...], w_fwd, w_bwd, delta_ref[...])
        spec = jnp.dot(gt_ref[...], taps, preferred_element_type=F32)
        zq = jnp.dot(gf_ref[...], z.astype(BF16), preferred_element_type=F32)
        zr, zi = zq[:nf], zq[nf:]
        hr, hi = spec[:nf], spec[nf:]
        prod = jnp.concatenate([zr * hr - zi * hi, zr * hi + zi * hr], axis=0).astype(BF16)
        y = jnp.dot(gi_ref[...], prod, preferred_element_type=F32)
        z = gates[o][...] * (y + z * bias_ref[o:o + 1, :])
    o_ref[...] = z


def _small_dft_tables(seq_len):
    n_fft = 2 * seq_len
    k = jnp.arange(n_fft, dtype=jnp.int32)
    ang = (2.0 * math.pi / n_fft) * ((k[:, None] * k[None, :]) % n_fft).astype(F32)
    c, s = jnp.cos(ang), jnp.sin(ang)
    cl, sl = c[:, :seq_len], s[:, :seq_len]
    g_fwd = jnp.concatenate([jnp.concatenate([cl, sl], axis=1),
                             jnp.concatenate([-sl, cl], axis=1)], axis=0)
    g_taps = jnp.concatenate([c, -s], axis=0)
    g_inv = g_fwd.T * (1.0 / n_fft)
    return g_fwd.astype(BF16), g_taps.astype(BF16), g_inv.astype(BF16)


def _hyena_small_convs(u, hid, feats, w_out, f_bias, seq_len):
    n, d3 = u.shape
    d = d3 // 3
    ct = 256
    nc = d // ct
    g_fwd, g_taps, g_inv = _small_dft_tables(seq_len)
    return pl.pallas_call(
        functools.partial(_small_conv_kernel, ct),
        grid=(nc,),
        in_specs=[pl.BlockSpec((n, ct), lambda c: (0, c)),
                  pl.BlockSpec((n, ct), lambda c: (0, nc + c)),
                  pl.BlockSpec((n, ct), lambda c: (0, 2 * nc + c)),
                  _const_spec(hid.shape), _const_spec(feats.shape), _const_spec(w_out.shape),
                  pl.BlockSpec((1, ct), lambda c: (0, c)),
                  _const_spec(g_fwd.shape), _const_spec(g_taps.shape), _const_spec(g_inv.shape),
                  pl.BlockSpec((2, ct), lambda c: (0, c))],
        out_specs=pl.BlockSpec((n, ct), lambda c: (0, c)),
        out_shape=jax.ShapeDtypeStruct((n, d), F32),
        compiler_params=_cparams(("arbitrary",)),
        name="small_conv",
    )(u, u, u, hid, feats, w_out, _decay_rates(d), g_fwd, g_taps, g_inv, f_bias)


def kernel(x, c, ctx, c_ctx, w_mod, b_mod, norm_w, ffn_w_gate_up, ffn_w_down, attn_w_qkv, attn_w_o,
           attn_q_norm, attn_k_norm, hy_w_in, hy_b_in, hy_conv_w, hy_conv_b, hy_f_w1, hy_f_b1,
           hy_f_w2, hy_f_b2, hy_f_w3, hy_f_b3, hy_f_wout, hy_f_freq, hy_f_bias, hy_w_out, hy_b_out):
    bsz, seq, d = x.shape
    ctx_len = ctx.shape[1]
    depth = w_mod.shape[0]
    assert bsz == 2, "the long convolution packs exactly two batches into one complex sequence"
    xs = x.reshape(bsz * seq, d)
    cs = ctx.reshape(bsz * ctx_len, d)

    cc = jnp.zeros((8, d), F32).at[:bsz].set(c).at[bsz].set(c_ctx)
    mod_all = _mod_vectors(cc, w_mod, b_mod).reshape(depth, 8, N_MOD, d)
    ctx_row = bsz

    c_rows = bsz * ctx_len
    p_sum, p_bcast = _head_sum_matrices()
    cos_t, sin_t = _rope_tables(seq)
    dft_tables = _dft_tables(seq)
    feats_x = _filter_features(seq)
    feats_c = _filter_features(ctx_len)
    zero_bias = jnp.zeros((d,), F32)

    for l in range(depth):
        mod = mod_all[l]
        is_attn = (l % 2) == 0
        ctx_out = l < depth - 1
        ctx_live = ctx_out or is_attn
        wgu0, wd0 = _prep_ffn_weights(ffn_w_gate_up[l, 0], ffn_w_down[l, 0])
        wgu1, wd1 = _prep_ffn_weights(ffn_w_gate_up[l, 1], ffn_w_down[l, 1])

        xs = _ffn(xs, mod, 0, norm_w[l, 0], wgu0, wd0, seq, None)
        if ctx_live:
            cs = _ffn(cs, mod, 0, norm_w[l, 0], wgu0, wd0, c_rows, ctx_row)

        if is_attn:
            a = l // 2
            w_qkv = attn_w_qkv[a].astype(BF16)
            w_o = attn_w_o[a].astype(BF16)
            qn = (jnp.tile(attn_q_norm[a], N_HEADS) * (HEAD_DIM ** -0.5 * math.log2(math.e))).reshape(1, -1)
            kn = jnp.tile(attn_k_norm[a], N_KV_HEADS).reshape(1, -1)
            q_l, kt_l, v_l = _attn_pre(xs, mod, norm_w[l, 1], w_qkv, qn, kn, p_sum, p_bcast,
                                       cos_t, sin_t, seq, None, True)
            q_c, kt_c, v_c = _attn_pre(cs, mod, norm_w[l, 1], w_qkv, qn, kn, p_sum, p_bcast,
                                       cos_t, sin_t, ctx_len, ctx_row, False)
            shift_ok = (HEAD_DIM * jnp.max(jnp.abs(qn)) * jnp.max(jnp.abs(kn))) < ATT_SHIFT_LIMIT
            o_l = lax.cond(shift_ok, functools.partial(_attention, False, seq),
                           functools.partial(_attention, True, seq), q_l, kt_c, v_c, kt_l, v_l)
            xs = _proj_residual(o_l, xs, mod, w_o, zero_bias, seq, None)
            if ctx_out:
                o_c = lax.cond(shift_ok, functools.partial(_attention, False, ctx_len),
                               functools.partial(_attention, True, ctx_len), q_c, kt_c, v_c)
                cs = _proj_residual(o_c, cs, mod, w_o, zero_bias, c_rows, ctx_row)
        else:
            j = l // 2
            w_in = hy_w_in[j].astype(BF16)
            w_out = hy_w_out[j].astype(BF16)
            fargs = (hy_f_w1[j], hy_f_b1[j], hy_f_w2[j], hy_f_b2[j], hy_f_w3[j], hy_f_b3[j], hy_f_freq[j])
            f_wout = hy_f_wout[j].astype(BF16)
            u_l = _hyena_pre(xs, mod, norm_w[l, 1], w_in, hy_b_in[j], hy_conv_w[j], hy_conv_b[j], seq, None)
            y_l = _hyena_long_convs(u_l, _filter_hidden(feats_x, *fargs), feats_x, f_wout, hy_f_bias[j],
                                    seq, dft_tables)
            if ctx_out:
                u_c = _hyena_pre(cs, mod, norm_w[l, 1], w_in, hy_b_in[j], hy_conv_w[j], hy_conv_b[j],
                                 ctx_len, ctx_row)
                y_c = _hyena_small_convs(u_c, _filter_hidden(feats_c, *fargs), feats_c, f_wout,
                                         hy_f_bias[j], ctx_len)
                cs = _proj_residual(y_c, cs, mod, w_out, hy_b_out[j], c_rows, ctx_row)
            xs = _proj_residual(y_l, xs, mod, w_out, hy_b_out[j], seq, None)

        xs = _ffn(xs, mod, 2, norm_w[l, 2], wgu1, wd1, seq, None)
        if ctx_out:
            cs = _ffn(cs, mod, 2, norm_w[l, 2], wgu1, wd1, c_rows, ctx_row)
    return xs.reshape(bsz, seq, d)
```

```python
import functools
import math

import jax
import jax.numpy as jnp
import numpy as np
from jax import lax
from jax.experimental import pallas as pl
from jax.experimental.pallas import tpu as pltpu

F32 = jnp.float32
BF16 = jnp.bfloat16

N_MOD = 9
N_HEADS = 16
N_KV_HEADS = 4
HEAD_DIM = 64
KV_GROUP = N_HEADS // N_KV_HEADS
KT_ROWS = 2 * HEAD_DIM
ATT_SHIFT_LIMIT = 60.0
GRID_W = 64
ROPE_THETA = 10000.0
EPS = 1e-6
FILTER_BANDS = 16
N_FILTER_FEAT = 1 + 2 * FILTER_BANDS
DECAY_TARGET = 1e-2
FAST_DECAY_PCT = 0.3
SLOW_DECAY_PCT = 1.5

LANES = 128
HALO = 16
FFT_N2 = 128
VMEM_LIMIT = 56 * 1024 * 1024

FFN_TM = 1024
PROJ_TM = 512
PRE_TM = 256
FFN_FC = 256
ATT_TQ = 256
ATT_TK = 512
FFT_NB = 16
FFT_KB = 8
FFT_CT = 512
FFT_CT_LAST = 256


def _cparams(sem):
    return pltpu.CompilerParams(dimension_semantics=sem, vmem_limit_bytes=VMEM_LIMIT)


def _const_spec(shape):
    nd = len(shape)
    return pl.BlockSpec(shape, lambda *_: (0,) * nd, pipeline_mode=pl.Buffered(1))


def _modulated(s, g, shift, scale):
    ms = jnp.mean(s * s, axis=-1, keepdims=True)
    return (s * lax.rsqrt(ms + EPS) * g) * (1.0 + scale) + shift


def _mod_kernel(c_ref, w_ref, b_ref, o_ref):
    c = c_ref[...]
    a = (c * jax.nn.sigmoid(c)).astype(BF16)
    o_ref[0] = jnp.dot(a, w_ref[0].astype(BF16), preferred_element_type=F32) + b_ref[0]


def _mod_vectors(cc, w_mod, b_mod):
    depth, d, n = w_mod.shape
    tn = n // 6
    return pl.pallas_call(
        _mod_kernel,
        grid=(depth, n // tn),
        in_specs=[pl.BlockSpec((8, d), lambda l, j: (0, 0)),
                  pl.BlockSpec((1, d, tn), lambda l, j: (l, 0, j)),
                  pl.BlockSpec((1, 1, tn), lambda l, j: (l, 0, j))],
        out_specs=pl.BlockSpec((1, 8, tn), lambda l, j: (l, 0, j)),
        out_shape=jax.ShapeDtypeStruct((depth, 8, n), F32),
        compiler_params=_cparams(("arbitrary", "arbitrary")),
        name="mod_vectors",
    )(cc, w_mod, b_mod.reshape(depth, 1, n))


def _mod_blockspec(d, tiles_per_row, fixed_row):
    if fixed_row is None:
        return pl.BlockSpec((None, N_MOD, d), lambda i: (i // tiles_per_row, 0, 0))
    return pl.BlockSpec((None, N_MOD, d), lambda i: (fixed_row, 0, 0))


def _ffn_kernel(k, s_ref, mod_ref, g_ref, wgu_ref, wd_ref, o_ref, h_ref, acc_ref):
    s = s_ref[...]
    shift = mod_ref[3 * k:3 * k + 1, :]
    scale = mod_ref[3 * k + 1:3 * k + 2, :]
    gate = mod_ref[3 * k + 2:3 * k + 3, :]
    h_ref[...] = _modulated(s, g_ref[...], shift, scale).astype(BF16)
    f = wd_ref.shape[0]
    for c in range(f // FFN_FC):
        lo = c * FFN_FC
        g = jnp.dot(h_ref[...], wgu_ref[:, lo:lo + FFN_FC], preferred_element_type=F32)
        u = jnp.dot(h_ref[...], wgu_ref[:, f + lo:f + lo + FFN_FC], preferred_element_type=F32)
        a = (g * jax.nn.sigmoid(g) * u).astype(BF16)
        y = jnp.dot(a, wd_ref[lo:lo + FFN_FC, :], preferred_element_type=F32)
        if c == 0:
            acc_ref[...] = y
        else:
            acc_ref[...] += y
    o_ref[...] = s + 0.5 * gate * acc_ref[...]


def _ffn(s, mod, k, g, wgu_all, wd_all, layer, which, rows_per_cond, fixed_row):
    n, d = s.shape
    tm = min(FFN_TM, n)
    tiles_per_row = rows_per_cond // tm
    f = wd_all.shape[2]
    assert f % FFN_FC == 0

    def weight_spec(shape):
        return pl.BlockSpec((None, None) + shape, lambda i: (layer, which, 0, 0), pipeline_mode=pl.Buffered(1))

    return pl.pallas_call(
        functools.partial(_ffn_kernel, k),
        grid=(n // tm,),
        in_specs=[pl.BlockSpec((tm, d), lambda i: (i, 0)),
                  _mod_blockspec(d, tiles_per_row, fixed_row),
                  _const_spec((1, d)),
                  weight_spec((d, 2 * f)),
                  weight_spec((f, d))],
        out_specs=pl.BlockSpec((tm, d), lambda i: (i, 0)),
        out_shape=jax.ShapeDtypeStruct((n, d), F32),
        scratch_shapes=[pltpu.VMEM((tm, d), BF16), pltpu.VMEM((tm, d), F32)],
        compiler_params=_cparams(("arbitrary",)),
        name="ffn",
    )(s, mod, g.reshape(1, d), wgu_all, wd_all)


def _proj_kernel(a_ref, s_ref, mod_ref, w_ref, b_ref, o_ref):
    gate = mod_ref[5:6, :]
    y = jnp.dot(a_ref[...].astype(BF16), w_ref[...], preferred_element_type=F32) + b_ref[...]
    o_ref[...] = s_ref[...] + gate * y


def _proj_residual(a, s, mod, w, b, rows_per_cond, fixed_row):
    n, d = s.shape
    tm = min(PROJ_TM, n)
    tiles_per_row = rows_per_cond // tm
    return pl.pallas_call(
        _proj_kernel,
        grid=(n // tm,),
        in_specs=[pl.BlockSpec((tm, a.shape[1]), lambda i: (i, 0)),
                  pl.BlockSpec((tm, d), lambda i: (i, 0)),
                  _mod_blockspec(d, tiles_per_row, fixed_row),
                  _const_spec(w.shape),
                  _const_spec((1, d))],
        out_specs=pl.BlockSpec((tm, d), lambda i: (i, 0)),
        out_shape=jax.ShapeDtypeStruct((n, d), F32),
        compiler_params=_cparams(("arbitrary",)),
        name="proj_residual",
    )(a, s, mod, w, b.reshape(1, d))


def _head_norm(x, p, pt, w):
    sq = x * x
    hi = sq.astype(BF16)
    lo = (sq - hi.astype(F32)).astype(BF16)
    ss = jnp.dot(hi, p, preferred_element_type=F32) + jnp.dot(lo, p, preferred_element_type=F32)
    r = lax.rsqrt(ss * (1.0 / HEAD_DIM) + EPS)
    rh = r.astype(BF16)
    rl = (r - rh.astype(F32)).astype(BF16)
    rb = jnp.dot(rh, pt, preferred_element_type=F32) + jnp.dot(rl, pt, preferred_element_type=F32)
    return x * rb * w


def _rope_block(xb, cos, sin):
    lane = lax.broadcasted_iota(jnp.int32, xb.shape, 1)
    fwd = pltpu.roll(xb, 16, axis=1)
    bwd = pltpu.roll(xb, LANES - 16, axis=1)
    partner = jnp.where((lane % 32) < 16, bwd, fwd)
    return xb * cos + partner * sin


def _attn_pre_kernel(rope, s_ref, mod_ref, g_ref, w_ref, qn_ref, kn_ref, p_ref, pt_ref,
                     cos_ref, sin_ref, q_ref, kt_ref, v_ref):
    s = s_ref[...]
    tm = s.shape[0]
    nq = N_HEADS * HEAD_DIM
    nk = N_KV_HEADS * HEAD_DIM
    h = _modulated(s, g_ref[...], mod_ref[3:4, :], mod_ref[4:5, :]).astype(BF16)
    qkv = jnp.dot(h, w_ref[...], preferred_element_type=F32)
    q = _head_norm(qkv[:, :nq], p_ref[...], pt_ref[...], qn_ref[...])
    k = _head_norm(qkv[:, nq:nq + nk], p_ref[:nk, :], pt_ref[:, :nk], kn_ref[...])
    v = qkv[:, nq + nk:]
    cos = cos_ref[...]
    sin = sin_ref[...]
    for j in range(nq // LANES):
        qb = q[:, j * LANES:(j + 1) * LANES]
        if rope:
            qb = _rope_block(qb, cos, sin)
        q_ref[:, j * LANES:(j + 1) * LANES] = qb.astype(BF16)
    kblocks = []
    for j in range(nk // LANES):
        kb = k[:, j * LANES:(j + 1) * LANES]
        if rope:
            kb = _rope_block(kb, cos, sin)
        kblocks.append(kb)
    kt = jnp.concatenate(kblocks, axis=1).T.astype(BF16)
    tail_row = lax.broadcasted_iota(jnp.int32, (KT_ROWS - HEAD_DIM, tm), 0)
    tail = jnp.where(tail_row == 0, 1.0, 0.0).astype(BF16)
    for hh in range(N_KV_HEADS):
        kt_ref[hh] = jnp.concatenate([kt[hh * HEAD_DIM:(hh + 1) * HEAD_DIM, :], tail], axis=0)
    lane = lax.broadcasted_iota(jnp.int32, (tm, LANES), 1)
    ones_col = jnp.where(lane == HEAD_DIM, 1.0, 0.0)
    for hh in range(N_KV_HEADS):
        vb = v[:, (hh // 2) * LANES:(hh // 2 + 1) * LANES]
        if hh % 2 == 1:
            vb = pltpu.roll(vb, HEAD_DIM, axis=1)
        v_ref[hh] = jnp.where(lane < HEAD_DIM, vb, ones_col).astype(BF16)


def _attn_pre(s, mod, g, w_qkv, qn, kn, p, pt, cos, sin, seq_len, fixed_row, rope):
    n, d = s.shape
    tm = min(PRE_TM, seq_len)
    tps = seq_len // tm
    n_seq = n // seq_len
    nq = N_HEADS * HEAD_DIM
    return pl.pallas_call(
        functools.partial(_attn_pre_kernel, rope),
        grid=(n // tm,),
        in_specs=[pl.BlockSpec((tm, d), lambda i: (i, 0)),
                  _mod_blockspec(d, tps, fixed_row),
                  _const_spec((1, d)),
                  _const_spec(w_qkv.shape),
                  _const_spec((1, nq)),
                  _const_spec((1, N_KV_HEADS * HEAD_DIM)),
                  _const_spec(p.shape),
                  _const_spec(pt.shape),
                  pl.BlockSpec((tm, LANES), lambda i: (i % tps, 0)),
                  pl.BlockSpec((tm, LANES), lambda i: (i % tps, 0))],
        out_specs=[pl.BlockSpec((tm, nq), lambda i: (i, 0)),
                   pl.BlockSpec((None, N_KV_HEADS, KT_ROWS, tm), lambda i: (i // tps, 0, 0, i % tps)),
                   pl.BlockSpec((None, N_KV_HEADS, tm, LANES), lambda i: (i // tps, 0, i % tps, 0))],
        out_shape=[jax.ShapeDtypeStruct((n, nq), BF16),
                   jax.ShapeDtypeStruct((n_seq, N_KV_HEADS, KT_ROWS, seq_len), BF16),
                   jax.ShapeDtypeStruct((n_seq, N_KV_HEADS, seq_len, LANES), BF16)],
        compiler_params=_cparams(("arbitrary",)),
        name="attn_pre",
    )(s, mod, g.reshape(1, d), w_qkv, qn, kn, p, pt, cos, sin)


def _rope_tables(seq_len):
    half = HEAD_DIM // 4
    t = np.arange(seq_len)
    pos = np.stack([t // GRID_W, t % GRID_W], axis=1).astype(np.float32)
    freqs = (ROPE_THETA ** (-np.arange(half, dtype=np.float32) / half)).astype(np.float32)
    lane = np.arange(LANES)
    axis = (lane % HEAD_DIM) // (HEAD_DIM // 2)
    e = lane % (HEAD_DIM // 2)
    ang = pos[:, axis] * freqs[e % half][None, :]
    sign = np.where(e < half, -1.0, 1.0).astype(np.float32)
    return jnp.asarray(np.cos(ang), F32), jnp.asarray(np.sin(ang) * sign[None, :], F32)


def _head_sum_matrices():
    lane = np.arange(N_HEADS * HEAD_DIM)
    p = (lane[:, None] // HEAD_DIM == np.arange(LANES)[None, :]).astype(np.float32)
    return jnp.asarray(p, BF16), jnp.asarray(p.T, BF16)


def _max_key_norm(kt_ref):
    k = kt_ref[:HEAD_DIM, :].astype(F32)
    return jnp.sqrt(jnp.max(jnp.sum(k * k, axis=0, keepdims=True), axis=1, keepdims=True))


def _attn_kernel(online, n_lat_chunks, tk, q_ref, ktc_ref, vc_ref, *rest):
    if n_lat_chunks:
        ktl_ref, vl_ref, o_ref, kmax_ref = rest
    else:
        o_ref, kmax_ref = rest
    if not online:
        @pl.when(pl.program_id(2) == 0)
        def _():
            kmax = _max_key_norm(ktc_ref)
            if n_lat_chunks:
                kmax = jnp.maximum(kmax, _max_key_norm(ktl_ref))
            kmax_ref[...] = kmax
    qf = q_ref[...].astype(F32)
    tq = qf.shape[0]
    lane = lax.broadcasted_iota(jnp.int32, (tq, LANES), 1)
    parts = []
    for g in range(KV_GROUP):
        blk = qf[:, (g // 2) * LANES:(g // 2 + 1) * LANES]
        if g % 2 == 1:
            blk = pltpu.roll(blk, HEAD_DIM, axis=1)
        qg = jnp.where(lane < HEAD_DIM, blk, 0.0)
        if not online:
            shift = jnp.sqrt(jnp.sum(qg * qg, axis=1, keepdims=True)) * kmax_ref[...]
            qg = jnp.where(lane == HEAD_DIM, -shift, qg)
        parts.append(qg.astype(BF16))
    q4 = jnp.concatenate(parts, axis=0)

    if online:
        def step(kt, v, carry):
            m, acc = carry
            s = jnp.dot(q4, kt, preferred_element_type=F32)
            m_new = jnp.maximum(m, jnp.max(s, axis=-1, keepdims=True))
            p = jnp.exp2(s - m_new)
            acc = jnp.exp2(m - m_new) * acc + jnp.dot(p.astype(BF16), v, preferred_element_type=F32)
            return m_new, acc
        carry = (jnp.full((KV_GROUP * tq, 1), -1e30, F32), jnp.zeros((KV_GROUP * tq, LANES), F32))
    else:
        def step(kt, v, acc):
            p = jnp.exp2(jnp.dot(q4, kt, preferred_element_type=F32))
            return acc + jnp.dot(p.astype(BF16), v, preferred_element_type=F32)
        carry = jnp.zeros((KV_GROUP * tq, LANES), F32)

    carry = step(ktc_ref[...], vc_ref[...], carry)
    if n_lat_chunks:
        def body(c, carry):
            off = pl.multiple_of(c * tk, tk)
            return step(ktl_ref[:, pl.ds(off, tk)], vl_ref[pl.ds(off, tk), :], carry)
        carry = lax.fori_loop(0, n_lat_chunks, body, carry, unroll=8 if online else True)
    acc = carry[1] if online else carry
    o = acc[:, :HEAD_DIM] / acc[:, HEAD_DIM:HEAD_DIM + 1]
    o_ref[...] = jnp.concatenate([o[g * tq:(g + 1) * tq, :] for g in range(KV_GROUP)], axis=1).astype(BF16)


def _attention(online, lq, q, ktc, vc, ktl=None, vl=None):
    n, nq = q.shape
    b = n // lq
    tq = ATT_TQ
    nqt = lq // tq
    gw = KV_GROUP * HEAD_DIM
    lc = ktc.shape[-1]
    in_specs = [pl.BlockSpec((tq, gw), lambda bb, j, i: (bb * nqt + i, j)),
                pl.BlockSpec((None, None, KT_ROWS, lc), lambda bb, j, i: (bb, j, 0, 0)),
                pl.BlockSpec((None, None, lc, LANES), lambda bb, j, i: (bb, j, 0, 0))]
    args = [q, ktc, vc]
    n_lat = 0
    if ktl is not None:
        ll = ktl.shape[-1]
        n_lat = ll // ATT_TK
        in_specs += [pl.BlockSpec((None, None, KT_ROWS, ll), lambda bb, j, i: (bb, j, 0, 0)),
                     pl.BlockSpec((None, None, ll, LANES), lambda bb, j, i: (bb, j, 0, 0))]
        args += [ktl, vl]
    return pl.pallas_call(
        functools.partial(_attn_kernel, online, n_lat, ATT_TK),
        grid=(b, N_KV_HEADS, nqt),
        in_specs=in_specs,
        out_specs=pl.BlockSpec((tq, gw), lambda bb, j, i: (bb * nqt + i, j)),
        out_shape=jax.ShapeDtypeStruct((n, nq), BF16),
        scratch_shapes=[pltpu.VMEM((1, 1), F32)],
        compiler_params=_cparams(("arbitrary", "arbitrary", "arbitrary")),
        name="attention",
    )(*args)


def _hyena_pre_kernel(tps, n_col, s_ref, prev_ref, next_ref, mod_ref, g_ref, w_ref, b_ref,
                      cw_ref, cb_ref, o_ref, h_ref, pre_ref):
    i = pl.program_id(0)
    tm = s_ref.shape[0]
    hb = HALO
    g = g_ref[...]
    shift = mod_ref[3:4, :]
    scale = mod_ref[4:5, :]
    h_ref[0:hb, :] = _modulated(prev_ref[...], g, shift, scale).astype(BF16)
    h_ref[hb:hb + tm, :] = _modulated(s_ref[...], g, shift, scale).astype(BF16)
    h_ref[hb + tm:, :] = _modulated(next_ref[...], g, shift, scale).astype(BF16)
    row = lax.broadcasted_iota(jnp.int32, (tm, 1), 0)
    drop_up = jnp.logical_and(row == 0, i % tps == 0)
    drop_dn = jnp.logical_and(row == tm - 1, i % tps == tps - 1)
    ct = w_ref.shape[1] // n_col
    for c in range(n_col):
        cols = slice(c * ct, (c + 1) * ct)
        pre_ref[...] = jnp.dot(h_ref[...], w_ref[:, cols], preferred_element_type=F32) + b_ref[:, cols]
        up = jnp.where(drop_up, 0.0, pre_ref[hb - 1:hb - 1 + tm, :])
        mid = pre_ref[hb:hb + tm, :]
        dn = jnp.where(drop_dn, 0.0, pre_ref[hb + 1:hb + 1 + tm, :])
        o_ref[:, cols] = (up * cw_ref[0:1, cols] + mid * cw_ref[1:2, cols]
                          + dn * cw_ref[2:3, cols] + cb_ref[:, cols])


def _hyena_pre(s, mod, g, w_in, b_in, conv_w, conv_b, seq_len, fixed_row):
    n, d = s.shape
    tm = min(PRE_TM, seq_len)
    tps = seq_len // tm
    n3 = w_in.shape[1]
    hb = HALO
    nblk = n // hb
    n_col = 6
    return pl.pallas_call(
        functools.partial(_hyena_pre_kernel, tps, n_col),
        grid=(n // tm,),
        in_specs=[pl.BlockSpec((tm, d), lambda i: (i, 0)),
                  pl.BlockSpec((hb, d), lambda i: (jnp.maximum(i * (tm // hb) - 1, 0), 0)),
                  pl.BlockSpec((hb, d), lambda i: (jnp.minimum((i + 1) * (tm // hb), nblk - 1), 0)),
                  _mod_blockspec(d, tps, fixed_row),
                  _const_spec((1, d)),
                  _const_spec(w_in.shape),
                  _const_spec((1, n3)),
                  _const_spec((3, n3)),
                  _const_spec((1, n3))],
        out_specs=pl.BlockSpec((tm, n3), lambda i: (i, 0)),
        out_shape=jax.ShapeDtypeStruct((n, n3), F32),
        scratch_shapes=[pltpu.VMEM((tm + 2 * hb, d), BF16),
                        pltpu.VMEM((tm + 2 * hb, n3 // n_col), F32)],
        compiler_params=_cparams(("arbitrary",)),
        name="hyena_pre",
    )(s, s, s, mod, g.reshape(1, d), w_in, b_in.reshape(1, n3), conv_w, conv_b.reshape(1, n3))


def _filter_kernel(feat_ref, w1_ref, b1_ref, w2_ref, b2_ref, w3_ref, b3_ref, a_ref, o_ref):
    hp = lax.Precision.HIGHEST
    a = a_ref[...]
    hid = jnp.sin(a * (jnp.dot(w1_ref[...], feat_ref[...], precision=hp, preferred_element_type=F32) + b1_ref[...]))
    hid = jnp.sin(a * (jnp.dot(w2_ref[...], hid, precision=hp, preferred_element_type=F32) + b2_ref[...]))
    hid = jnp.sin(a * (jnp.dot(w3_ref[...], hid, precision=hp, preferred_element_type=F32) + b3_ref[...]))
    o_ref[...] = hid.T


def _taps_from_hidden(hid, feats, w_fwd, w_bwd, delta):
    hb = hid.astype(BF16)
    t = feats[:, 0:1]
    use_fwd = feats[:, N_FILTER_FEAT:N_FILTER_FEAT + 1]
    use_bwd = feats[:, N_FILTER_FEAT + 1:N_FILTER_FEAT + 2]
    fwd = jnp.dot(hb, w_fwd, preferred_element_type=F32)
    bwd = jnp.dot(hb, w_bwd, preferred_element_type=F32)
    return (jnp.exp(-t * delta) * (use_fwd * fwd + use_bwd * bwd)).astype(BF16)


def _filter_features(seq_len):
    l = seq_len
    n_feat = N_FILTER_FEAT
    n = np.arange(2 * l)
    pos = np.where(n < l, n, 2 * l - n)
    pos = np.where(n == l, 0, pos)
    t = np.linspace(0.0, 1.0, l, dtype=np.float32)[pos]
    w = (2.0 * math.pi * pos.astype(np.float32) / l).astype(np.float32)
    bands = np.linspace(1e-4, FILTER_BANDS - 1, FILTER_BANDS, dtype=np.float32)
    bw = (bands[None, :] * w[:, None]).astype(np.float32)
    feats = np.zeros((2 * l, LANES), np.float32)
    feats[:, 0] = t
    feats[:, 1:1 + FILTER_BANDS] = np.cos(bw)
    feats[:, 1 + FILTER_BANDS:n_feat] = -np.sin(bw)
    feats[:, n_feat] = (n < l)
    feats[:, n_feat + 1] = np.logical_or(n > l, n == 0)
    return jnp.asarray(feats, F32), jnp.asarray(np.ascontiguousarray(feats.T), F32)


def _filter_hidden(feats_t, f_w1, f_b1, f_w2, f_b2, f_w3, f_b3, f_freq):
    n = feats_t.shape[1]
    fh = f_w1.shape[1]
    w1t = jnp.zeros((fh, LANES), F32).at[:, :N_FILTER_FEAT].set(f_w1.T)
    tm = min(512, n)
    return pl.pallas_call(
        _filter_kernel,
        grid=(n // tm,),
        in_specs=[pl.BlockSpec((LANES, tm), lambda i: (0, i)),
                  _const_spec((fh, LANES)), _const_spec((fh, 1)),
                  _const_spec((fh, fh)), _const_spec((fh, 1)),
                  _const_spec((fh, fh)), _const_spec((fh, 1)),
                  _const_spec((fh, 1))],
        out_specs=pl.BlockSpec((tm, fh), lambda i: (i, 0)),
        out_shape=jax.ShapeDtypeStruct((n, fh), F32),
        compiler_params=_cparams(("arbitrary",)),
        name="filter_hidden",
    )(feats_t, w1t, f_b1.reshape(fh, 1), f_w2.T, f_b2.reshape(fh, 1), f_w3.T, f_b3.reshape(fh, 1),
      f_freq.reshape(fh, 1))


def _decay_rates(d):
    min_decay = math.log(DECAY_TARGET) / SLOW_DECAY_PCT
    max_decay = math.log(DECAY_TARGET) / FAST_DECAY_PCT
    return jnp.abs(jnp.linspace(min_decay, max_decay, d, dtype=F32)).reshape(1, d)


def _dft_tables(seq_len):
    n_fft = 2 * seq_len
    n1 = n_fft // FFT_N2
    k1 = jnp.arange(n1, dtype=jnp.int32)[None, :, None]
    n2 = jnp.arange(FFT_N2, dtype=jnp.int32)[:, None, None]
    nn1 = jnp.arange(n1, dtype=jnp.int32)[None, None, :]
    ang = (2.0 * math.pi / n_fft) * ((k1 * (FFT_N2 * nn1 + n2)) % n_fft).astype(F32)
    c = jnp.cos(ang)
    s = jnp.sin(ang)
    h = n1 // 2
    g_first = jnp.concatenate([jnp.concatenate([c[:, :, :h], s[:, :, :h]], axis=2),
                               jnp.concatenate([-s[:, :, :h], c[:, :, :h]], axis=2)], axis=1)
    g_last = jnp.swapaxes(g_first, 1, 2) * (1.0 / n_fft)
    g_taps = jnp.concatenate([c, -s], axis=1)
    k2 = jnp.arange(FFT_N2, dtype=jnp.int32)
    ang2 = (2.0 * math.pi / FFT_N2) * ((k2[:, None] * k2[None, :]) % FFT_N2).astype(F32)
    c2, s2 = jnp.cos(ang2), jnp.sin(ang2)
    g_mid = jnp.concatenate([jnp.concatenate([c2, s2], axis=1),
                             jnp.concatenate([-s2, c2], axis=1)], axis=0)
    return (g_first.astype(BF16), g_last.astype(BF16), g_taps.astype(BF16),
            g_mid.astype(BF16), g_mid.T.astype(BF16))


def _stage_rows(x, rows_ref):
    for s in range(rows_ref.shape[0]):
        rows_ref[s] = x[:, s * LANES:(s + 1) * LANES]


def _column_group(rows_ref, j, r, nb):
    return jnp.concatenate([rows_ref[s, pl.ds(j, r, stride=nb), :] for s in range(rows_ref.shape[0])], axis=1)


def _scatter_column_group(stage_ref, j, y, nb):
    for s in range(stage_ref.shape[0]):
        stage_ref[s, pl.ds(j, y.shape[0], stride=nb), :] = y[:, s * LANES:(s + 1) * LANES]


def _staged_block(stage_ref, nb):
    n_slab, rows, _ = stage_ref.shape
    return jnp.concatenate([stage_ref[s].reshape(rows // nb, nb, LANES) for s in range(n_slab)], axis=2)


def _first_stage_dft(rows_ref, g_ref, stage_ref, r, nb):
    for j in range(nb):
        xj = _column_group(rows_ref, j, r, nb).astype(BF16)
        stage_ref[:, j, :] = jnp.dot(g_ref[j], xj, preferred_element_type=F32)
    return stage_ref[...].astype(BF16)


def _fft_first_kernel(x_ref, g_ref, o_ref, rows_ref, stage_ref):
    r, nb, ct = x_ref.shape
    _stage_rows(x_ref[...].reshape(r * nb, ct), rows_ref)
    o_ref[...] = _first_stage_dft(rows_ref, g_ref, stage_ref, r, nb)


def _fft_first(x3, col_block, d, g):
    r = x3.shape[0]
    rows_out = g.shape[1]
    nb, ct = FFT_NB, FFT_CT
    return pl.pallas_call(
        _fft_first_kernel,
        grid=(FFT_N2 // nb, d // ct),
        in_specs=[pl.BlockSpec((r, nb, ct), lambda i, c: (0, i, col_block * (d // ct) + c)),
                  pl.BlockSpec((nb, rows_out, r), lambda i, c: (i, 0, 0))],
        out_specs=pl.BlockSpec((rows_out, nb, ct), lambda i, c: (0, i, c)),
        out_shape=jax.ShapeDtypeStruct((rows_out, FFT_N2, d), BF16),
        scratch_shapes=[pltpu.VMEM((ct // LANES, r * nb, LANES), F32),
                        pltpu.VMEM((rows_out, nb, ct), F32)],
        compiler_params=_cparams(("arbitrary", "arbitrary")),
        name="fft_first",
    )(x3, g)


def _fft_taps_kernel(ct, h_ref, feat_ref, wo_ref, delta_ref, g_ref, o_ref, rows_ref, stage_ref):
    r, nb, fh = h_ref.shape
    d = wo_ref.shape[1] // 4
    col0 = pl.program_id(1) * ct
    hid = h_ref[...].reshape(r * nb, fh)
    feats = feat_ref[...].reshape(r * nb, LANES)
    for o in range(2):
        w_fwd = wo_ref[:, pl.ds(pl.multiple_of(2 * o * d + col0, ct), ct)]
        w_bwd = wo_ref[:, pl.ds(pl.multiple_of((2 * o + 1) * d + col0, ct), ct)]
        taps = _taps_from_hidden(hid, feats, w_fwd, w_bwd, delta_ref[...])
        _stage_rows(taps.astype(F32), rows_ref)
        o_ref[o] = _first_stage_dft(rows_ref, g_ref, stage_ref, r, nb)


def _fft_taps(hid, feats, w_out, deltas, g):
    n, fh = hid.shape
    n1 = n // FFT_N2
    d = deltas.shape[1]
    rows_out = g.shape[1]
    nb, ct = FFT_NB, FFT_CT
    return pl.pallas_call(
        functools.partial(_fft_taps_kernel, ct),
        grid=(FFT_N2 // nb, d // ct),
        in_specs=[pl.BlockSpec((n1, nb, fh), lambda i, c: (0, i, 0)),
                  pl.BlockSpec((n1, nb, LANES), lambda i, c: (0, i, 0)),
                  _const_spec(w_out.shape),
                  pl.BlockSpec((1, ct), lambda i, c: (0, c)),
                  pl.BlockSpec((nb, rows_out, n1), lambda i, c: (i, 0, 0))],
        out_specs=pl.BlockSpec((2, rows_out, nb, ct), lambda i, c: (0, 0, i, c)),
        out_shape=jax.ShapeDtypeStruct((2, rows_out, FFT_N2, d), BF16),
        scratch_shapes=[pltpu.VMEM((ct // LANES, n1 * nb, LANES), F32),
                        pltpu.VMEM((rows_out, nb, ct), F32)],
        compiler_params=_cparams(("arbitrary", "arbitrary")),
        name="fft_taps",
    )(hid.reshape(n1, FFT_N2, fh), feats.reshape(n1, FFT_N2, LANES), w_out, deltas, g)


def _fft_mid_kernel(t_ref, f_ref, g_ref, gi_ref, o_ref):
    half = FFT_N2
    for kk in range(FFT_KB):
        x = jnp.dot(g_ref[...], jnp.concatenate([t_ref[0, kk], t_ref[1, kk]], axis=0),
                    preferred_element_type=F32)
        h = jnp.dot(g_ref[...], jnp.concatenate([f_ref[0, kk], f_ref[1, kk]], axis=0),
                    preferred_element_type=F32)
        xr, xi = x[:half], x[half:]
        hr, hi = h[:half], h[half:]
        z = jnp.concatenate([xr * hr - xi * hi, xr * hi + xi * hr], axis=0).astype(BF16)
        y = jnp.dot(gi_ref[...], z, preferred_element_type=F32)
        o_ref[0, kk] = y[:half].astype(BF16)
        o_ref[1, kk] = y[half:].astype(BF16)


def _fft_mid(t, f, order, g_mid, g_mid_inv):
    rows, _, d = t.shape
    n1 = rows // 2
    blk = pl.BlockSpec((2, FFT_KB, FFT_N2, d), lambda i: (0, i, 0, 0))
    taps_blk = pl.BlockSpec((None, 2, FFT_KB, FFT_N2, d), lambda i: (order, 0, i, 0, 0))
    out = pl.pallas_call(
        _fft_mid_kernel,
        grid=(n1 // FFT_KB,),
        in_specs=[blk, taps_blk, _const_spec(g_mid.shape), _const_spec(g_mid_inv.shape)],
        out_specs=blk,
        out_shape=jax.ShapeDtypeStruct((2, n1, FFT_N2, d), BF16),
        compiler_params=_cparams(("arbitrary",)),
        name="fft_mid",
    )(t.reshape(2, n1, FFT_N2, d), f.reshape(2, 2, n1, FFT_N2, d), g_mid, g_mid_inv)
    return out.reshape(rows, FFT_N2, d)


def _fft_last_kernel(b_ref, g_ref, z_ref, gate_ref, bias_ref, o_ref, rows_ref, stage_ref):
    rows_in, nb, ct = b_ref.shape
    _stage_rows(b_ref[...].astype(F32).reshape(rows_in * nb, ct), rows_ref)
    for j in range(nb):
        bj = _column_group(rows_ref, j, rows_in, nb).astype(BF16)
        _scatter_column_group(stage_ref, j, jnp.dot(g_ref[j], bj, preferred_element_type=F32), nb)
    o_ref[...] = gate_ref[...] * (_staged_block(stage_ref, nb) + z_ref[...] * bias_ref[...])


def _fft_last(b, g, z3, z_col, gate3, gate_col, bias):
    rows_in, _, d = b.shape
    r = g.shape[1]
    nb, ct = FFT_NB, FFT_CT_LAST
    nc = d // ct
    return pl.pallas_call(
        _fft_last_kernel,
        grid=(FFT_N2 // nb, nc),
        in_specs=[pl.BlockSpec((rows_in, nb, ct), lambda i, c: (0, i, c)),
                  pl.BlockSpec((nb, r, rows_in), lambda i, c: (i, 0, 0)),
                  pl.BlockSpec((r, nb, ct), lambda i, c: (0, i, z_col * nc + c)),
                  pl.BlockSpec((r, nb, ct), lambda i, c: (0, i, gate_col * nc + c)),
                  pl.BlockSpec((1, 1, ct), lambda i, c: (0, 0, c))],
        out_specs=pl.BlockSpec((r, nb, ct), lambda i, c: (0, i, c)),
        out_shape=jax.ShapeDtypeStruct((r, FFT_N2, d), F32),
        scratch_shapes=[pltpu.VMEM((ct // LANES, rows_in * nb, LANES), F32),
                        pltpu.VMEM((ct // LANES, r * nb, LANES), F32)],
        compiler_params=_cparams(("arbitrary", "arbitrary")),
        name="fft_last",
    )(b, g, z3, gate3, bias.reshape(1, 1, d))


def _hyena_long_convs(u, hid, feats, w_out, f_bias, seq_len, tables):
    n, d3 = u.shape
    d = d3 // 3
    g_first, g_last, g_taps, g_mid, g_mid_inv = tables
    n1 = 2 * seq_len // FFT_N2
    u3 = u.reshape(n1, FFT_N2, d3)
    f = _fft_taps(hid, feats, w_out, _decay_rates(d), g_taps)
    z3, z_col = u3, 0
    for o in range(2):
        a = _fft_first(z3, z_col, d, g_first)
        bq = _fft_mid(a, f, o, g_mid, g_mid_inv)
        z3 = _fft_last(bq, g_last, z3, z_col, u3, 1 + o, f_bias[o])
        z_col = 0
    return z3.reshape(n, d)


def _small_conv_kernel(ct, u_v_ref, u_x1_ref, u_x2_ref, h_ref, feat_ref, wo_ref, delta_ref,
                       gf_ref, gt_ref, gi_ref, bias_ref, o_ref):
    nf = gf_ref.shape[0] // 2
    d = wo_ref.shape[1] // 4
    col0 = pl.program_id(0) * ct
    z = u_v_ref[...]
    gates = (u_x1_ref, u_x2_ref)
    for o in range(2):
        w_fwd = wo_ref[:, pl.ds(pl.multiple_of(2 * o * d + col0, ct), ct)]
        w_bwd = wo_ref[:, pl.ds(pl.multiple_of((2 * o + 1) * d + col0, ct), ct)]
        taps = _taps_from_hidden(h_ref[...], feat_ref[...], w_fwd, w_bwd, delta_ref[...])
        spec = jnp.dot(gt_ref[...], taps, preferred_element_type=F32)
        zq = jnp.dot(gf_ref[...], z.astype(BF16), preferred_element_type=F32)
        zr, zi = zq[:nf], zq[nf:]
        hr, hi = spec[:nf], spec[nf:]
        prod = jnp.concatenate([zr * hr - zi * hi, zr * hi + zi * hr], axis=0).astype(BF16)
        y = jnp.dot(gi_ref[...], prod, preferred_element_type=F32)
        z = gates[o][...] * (y + z * bias_ref[o:o + 1, :])
    o_ref[...] = z


def _small_dft_tables(seq_len):
    n_fft = 2 * seq_len
    k = jnp.arange(n_fft, dtype=jnp.int32)
    ang = (2.0 * math.pi / n_fft) * ((k[:, None] * k[None, :]) % n_fft).astype(F32)
    c, s = jnp.cos(ang), jnp.sin(ang)
    cl, sl = c[:, :seq_len], s[:, :seq_len]
    g_fwd = jnp.concatenate([jnp.concatenate([cl, sl], axis=1),
                             jnp.concatenate([-sl, cl], axis=1)], axis=0)
    g_taps = jnp.concatenate([c, -s], axis=0)
    g_inv = g_fwd.T * (1.0 / n_fft)
    return g_fwd.astype(BF16), g_taps.astype(BF16), g_inv.astype(BF16)


def _hyena_small_convs(u, hid, feats, w_out, f_bias, seq_len):
    n, d3 = u.shape
    d = d3 // 3
    ct = 256
    nc = d // ct
    g_fwd, g_taps, g_inv = _small_dft_tables(seq_len)
    return pl.pallas_call(
        functools.partial(_small_conv_kernel, ct),
        grid=(nc,),
        in_specs=[pl.BlockSpec((n, ct), lambda c: (0, c)),
                  pl.BlockSpec((n, ct), lambda c: (0, nc + c)),
                  pl.BlockSpec((n, ct), lambda c: (0, 2 * nc + c)),
                  _const_spec(hid.shape), _const_spec(feats.shape), _const_spec(w_out.shape),
                  pl.BlockSpec((1, ct), lambda c: (0, c)),
                  _const_spec(g_fwd.shape), _const_spec(g_taps.shape), _const_spec(g_inv.shape),
                  pl.BlockSpec((2, ct), lambda c: (0, c))],
        out_specs=pl.BlockSpec((n, ct), lambda c: (0, c)),
        out_shape=jax.ShapeDtypeStruct((n, d), F32),
        compiler_params=_cparams(("arbitrary",)),
        name="small_conv",
    )(u, u, u, hid, feats, w_out, _decay_rates(d), g_fwd, g_taps, g_inv, f_bias)


def kernel(x, c, ctx, c_ctx, w_mod, b_mod, norm_w, ffn_w_gate_up, ffn_w_down, attn_w_qkv, attn_w_o,
           attn_q_norm, attn_k_norm, hy_w_in, hy_b_in, hy_conv_w, hy_conv_b, hy_f_w1, hy_f_b1,
           hy_f_w2, hy_f_b2, hy_f_w3, hy_f_b3, hy_f_wout, hy_f_freq, hy_f_bias, hy_w_out, hy_b_out):
    bsz, seq, d = x.shape
    ctx_len = ctx.shape[1]
    depth = w_mod.shape[0]
    assert bsz == 2, "the long convolution packs exactly two batches into one complex sequence"
    xs = x.reshape(bsz * seq, d)
    cs = ctx.reshape(bsz * ctx_len, d)

    cc = jnp.zeros((8, d), F32).at[:bsz].set(c).at[bsz].set(c_ctx)
    mod_all = _mod_vectors(cc, w_mod, b_mod).reshape(depth, 8, N_MOD, d)
    ctx_row = bsz

    c_rows = bsz * ctx_len
    p_sum, p_bcast = _head_sum_matrices()
    cos_t, sin_t = _rope_tables(seq)
    dft_tables = _dft_tables(seq)
    feats_x, feats_xt = _filter_features(seq)
    feats_c, feats_ct = _filter_features(ctx_len)
    wgu_all = ffn_w_gate_up.astype(BF16)
    wd_all = ffn_w_down.astype(BF16)
    zero_bias = jnp.zeros((d,), F32)

    for l in range(depth):
        mod = mod_all[l]
        is_attn = (l % 2) == 0
        ctx_out = l < depth - 1
        ctx_live = ctx_out or is_attn

        xs = _ffn(xs, mod, 0, norm_w[l, 0], wgu_all, wd_all, l, 0, seq, None)
        if ctx_live:
            cs = _ffn(cs, mod, 0, norm_w[l, 0], wgu_all, wd_all, l, 0, c_rows, ctx_row)

        if is_attn:
            a = l // 2
            w_qkv = attn_w_qkv[a].astype(BF16)
            w_o = attn_w_o[a].astype(BF16)
            qn = (jnp.tile(attn_q_norm[a], N_HEADS) * (HEAD_DIM ** -0.5 * math.log2(math.e))).reshape(1, -1)
            kn = jnp.tile(attn_k_norm[a], N_KV_HEADS).reshape(1, -1)
            q_l, kt_l, v_l = _attn_pre(xs, mod, norm_w[l, 1], w_qkv, qn, kn, p_sum, p_bcast,
                                       cos_t, sin_t, seq, None, True)
            q_c, kt_c, v_c = _attn_pre(cs, mod, norm_w[l, 1], w_qkv, qn, kn, p_sum, p_bcast,
                                       cos_t, sin_t, ctx_len, ctx_row, False)
            shift_ok = (HEAD_DIM * jnp.max(jnp.abs(qn)) * jnp.max(jnp.abs(kn))) < ATT_SHIFT_LIMIT
            o_l = lax.cond(shift_ok, functools.partial(_attention, False, seq),
                           functools.partial(_attention, True, seq), q_l, kt_c, v_c, kt_l, v_l)
            xs = _proj_residual(o_l, xs, mod, w_o, zero_bias, seq, None)
            if ctx_out:
                o_c = lax.cond(shift_ok, functools.partial(_attention, False, ctx_len),
                               functools.partial(_attention, True, ctx_len), q_c, kt_c, v_c)
                cs = _proj_residual(o_c, cs, mod, w_o, zero_bias, c_rows, ctx_row)
        else:
            j = l // 2
            w_in = hy_w_in[j].astype(BF16)
            w_out = hy_w_out[j].astype(BF16)
            fargs = (hy_f_w1[j], hy_f_b1[j], hy_f_w2[j], hy_f_b2[j], hy_f_w3[j], hy_f_b3[j], hy_f_freq[j])
            f_wout = hy_f_wout[j].astype(BF16)
            u_l = _hyena_pre(xs, mod, norm_w[l, 1], w_in, hy_b_in[j], hy_conv_w[j], hy_conv_b[j], seq, None)
            y_l = _hyena_long_convs(u_l, _filter_hidden(feats_xt, *fargs), feats_x, f_wout, hy_f_bias[j],
                                    seq, dft_tables)
            if ctx_out:
                u_c = _hyena_pre(cs, mod, norm_w[l, 1], w_in, hy_b_in[j], hy_conv_w[j], hy_conv_b[j],
                                 ctx_len, ctx_row)
                y_c = _hyena_small_convs(u_c, _filter_hidden(feats_ct, *fargs), feats_c, f_wout,
                                         hy_f_bias[j], ctx_len)
                cs = _proj_residual(y_c, cs, mod, w_out, hy_b_out[j], c_rows, ctx_row)
            xs = _proj_residual(y_l, xs, mod, w_out, hy_b_out[j], seq, None)

        xs = _ffn(xs, mod, 2, norm_w[l, 2], wgu_all, wd_all, l, 1, seq, None)
        if ctx_out:
            cs = _ffn(cs, mod, 2, norm_w[l, 2], wgu_all, wd_all, l, 1, c_rows, ctx_row)
    return xs.reshape(bsz, seq, d)
```

```python
import functools
import math

import jax
import jax.numpy as jnp
import numpy as np
from jax import lax
from jax.experimental import pallas as pl
from jax.experimental.pallas import tpu as pltpu

F32 = jnp.float32
BF16 = jnp.bfloat16

N_MOD = 9
N_HEADS = 16
N_KV_HEADS = 4
HEAD_DIM = 64
KV_GROUP = N_HEADS // N_KV_HEADS
KT_ROWS = 2 * HEAD_DIM
ATT_SHIFT_LIMIT = 60.0
GRID_W = 64
ROPE_THETA = 10000.0
EPS = 1e-6
FILTER_BANDS = 16
N_FILTER_FEAT = 1 + 2 * FILTER_BANDS
DECAY_TARGET = 1e-2
FAST_DECAY_PCT = 0.3
SLOW_DECAY_PCT = 1.5

LANES = 128
HALO = 16
FFT_N2 = 128
VMEM_LIMIT = 56 * 1024 * 1024

FFN_TM = 1024
PRE_TM = 256
FFN_FC = 256
ATT_TQ = 256
ATT_SUB = 4
ATT_TK = 512
FFT_NB = 16
FFT_KB = 8
FFT_CT = 512
FFT_CT_LAST = 256


def _cparams(sem):
    return pltpu.CompilerParams(dimension_semantics=sem, vmem_limit_bytes=VMEM_LIMIT)


def _const_spec(shape):
    nd = len(shape)
    return pl.BlockSpec(shape, lambda *_: (0,) * nd, pipeline_mode=pl.Buffered(1))


def _modulated(s, g, shift, scale):
    ms = jnp.mean(s * s, axis=-1, keepdims=True)
    return (s * lax.rsqrt(ms + EPS) * g) * (1.0 + scale) + shift


def _mod_kernel(c_ref, w_ref, b_ref, o_ref):
    c = c_ref[...]
    a = (c * jax.nn.sigmoid(c)).astype(BF16)
    o_ref[0] = jnp.dot(a, w_ref[0].astype(BF16), preferred_element_type=F32) + b_ref[0]


def _mod_vectors(cc, w_mod, b_mod):
    depth, d, n = w_mod.shape
    tn = n // 6
    return pl.pallas_call(
        _mod_kernel,
        grid=(depth, n // tn),
        in_specs=[pl.BlockSpec((8, d), lambda l, j: (0, 0)),
                  pl.BlockSpec((1, d, tn), lambda l, j: (l, 0, j)),
                  pl.BlockSpec((1, 1, tn), lambda l, j: (l, 0, j))],
        out_specs=pl.BlockSpec((1, 8, tn), lambda l, j: (l, 0, j)),
        out_shape=jax.ShapeDtypeStruct((depth, 8, n), F32),
        compiler_params=_cparams(("arbitrary", "arbitrary")),
        name="mod_vectors",
    )(cc, w_mod, b_mod.reshape(depth, 1, n))


def _mod_blockspec(d, tiles_per_row, fixed_row):
    if fixed_row is None:
        return pl.BlockSpec((None, N_MOD, d), lambda i: (i // tiles_per_row, 0, 0))
    return pl.BlockSpec((None, N_MOD, d), lambda i: (fixed_row, 0, 0))


def _ffn_kernel(k, with_mixer, s_ref, mod_ref, g_ref, wgu_ref, wd_ref, *rest):
    if with_mixer:
        a_ref, wp_ref, bp_ref, o_ref, h_ref, acc_ref = rest
        mixed = jnp.dot(a_ref[...].astype(BF16), wp_ref[...], preferred_element_type=F32) + bp_ref[...]
        s = s_ref[...] + mod_ref[5:6, :] * mixed
    else:
        o_ref, h_ref, acc_ref = rest
        s = s_ref[...]
    shift = mod_ref[3 * k:3 * k + 1, :]
    scale = mod_ref[3 * k + 1:3 * k + 2, :]
    gate = mod_ref[3 * k + 2:3 * k + 3, :]
    h_ref[...] = _modulated(s, g_ref[...], shift, scale).astype(BF16)
    f = wd_ref.shape[0]
    for c in range(f // FFN_FC):
        lo = c * FFN_FC
        g = jnp.dot(h_ref[...], wgu_ref[:, lo:lo + FFN_FC], preferred_element_type=F32)
        u = jnp.dot(h_ref[...], wgu_ref[:, f + lo:f + lo + FFN_FC], preferred_element_type=F32)
        a = (g * jax.nn.sigmoid(g) * u).astype(BF16)
        y = jnp.dot(a, wd_ref[lo:lo + FFN_FC, :], preferred_element_type=F32)
        if c == 0:
            acc_ref[...] = y
        else:
            acc_ref[...] += y
    o_ref[...] = s + 0.5 * gate * acc_ref[...]


def _ffn(s, mod, k, g, wgu_all, wd_all, layer, which, rows_per_cond, fixed_row, mixer=None):
    n, d = s.shape
    tm = min(FFN_TM, n)
    tiles_per_row = rows_per_cond // tm
    f = wd_all.shape[2]
    assert f % FFN_FC == 0

    def weight_spec(shape):
        return pl.BlockSpec((None, None) + shape, lambda i: (layer, which, 0, 0), pipeline_mode=pl.Buffered(1))

    in_specs = [pl.BlockSpec((tm, d), lambda i: (i, 0)),
                _mod_blockspec(d, tiles_per_row, fixed_row),
                _const_spec((1, d)),
                weight_spec((d, 2 * f)),
                weight_spec((f, d))]
    args = [s, mod, g.reshape(1, d), wgu_all, wd_all]
    if mixer is not None:
        a, w, b = mixer
        in_specs += [pl.BlockSpec((tm, a.shape[1]), lambda i: (i, 0)), _const_spec(w.shape), _const_spec((1, d))]
        args += [a, w, b.reshape(1, d)]
    return pl.pallas_call(
        functools.partial(_ffn_kernel, k, mixer is not None),
        grid=(n // tm,),
        in_specs=in_specs,
        out_specs=pl.BlockSpec((tm, d), lambda i: (i, 0)),
        out_shape=jax.ShapeDtypeStruct((n, d), F32),
        scratch_shapes=[pltpu.VMEM((tm, d), BF16), pltpu.VMEM((tm, d), F32)],
        compiler_params=_cparams(("arbitrary",)),
        name="ffn",
    )(*args)


def _head_norm(x, p, pt, w):
    sq = x * x
    hi = sq.astype(BF16)
    lo = (sq - hi.astype(F32)).astype(BF16)
    ss = jnp.dot(hi, p, preferred_element_type=F32) + jnp.dot(lo, p, preferred_element_type=F32)
    r = lax.rsqrt(ss * (1.0 / HEAD_DIM) + EPS)
    rh = r.astype(BF16)
    rl = (r - rh.astype(F32)).astype(BF16)
    rb = jnp.dot(rh, pt, preferred_element_type=F32) + jnp.dot(rl, pt, preferred_element_type=F32)
    return x * rb * w


def _rope_block(xb, cos, sin):
    lane = lax.broadcasted_iota(jnp.int32, xb.shape, 1)
    fwd = pltpu.roll(xb, 16, axis=1)
    bwd = pltpu.roll(xb, LANES - 16, axis=1)
    partner = jnp.where((lane % 32) < 16, bwd, fwd)
    return xb * cos + partner * sin


def _attn_pre_kernel(rope, s_ref, mod_ref, g_ref, w_ref, qn_ref, kn_ref, p_ref, pt_ref,
                     cos_ref, sin_ref, q_ref, kt_ref, v_ref):
    s = s_ref[...]
    tm = s.shape[0]
    nq = N_HEADS * HEAD_DIM
    nk = N_KV_HEADS * HEAD_DIM
    h = _modulated(s, g_ref[...], mod_ref[3:4, :], mod_ref[4:5, :]).astype(BF16)
    qkv = jnp.dot(h, w_ref[...], preferred_element_type=F32)
    q = _head_norm(qkv[:, :nq], p_ref[...], pt_ref[...], qn_ref[...])
    k = _head_norm(qkv[:, nq:nq + nk], p_ref[:nk, :], pt_ref[:, :nk], kn_ref[...])
    v = qkv[:, nq + nk:]
    cos = cos_ref[...]
    sin = sin_ref[...]
    for j in range(nq // LANES):
        qb = q[:, j * LANES:(j + 1) * LANES]
        if rope:
            qb = _rope_block(qb, cos, sin)
        q_ref[:, j * LANES:(j + 1) * LANES] = qb.astype(BF16)
    kblocks = []
    for j in range(nk // LANES):
        kb = k[:, j * LANES:(j + 1) * LANES]
        if rope:
            kb = _rope_block(kb, cos, sin)
        kblocks.append(kb)
    kt = jnp.concatenate(kblocks, axis=1).T.astype(BF16)
    tail_row = lax.broadcasted_iota(jnp.int32, (KT_ROWS - HEAD_DIM, tm), 0)
    tail = jnp.where(tail_row == 0, 1.0, 0.0).astype(BF16)
    for hh in range(N_KV_HEADS):
        kt_ref[hh] = jnp.concatenate([kt[hh * HEAD_DIM:(hh + 1) * HEAD_DIM, :], tail], axis=0)
    lane = lax.broadcasted_iota(jnp.int32, (tm, LANES), 1)
    ones_col = jnp.where(lane == HEAD_DIM, 1.0, 0.0)
    for hh in range(N_KV_HEADS):
        vb = v[:, (hh // 2) * LANES:(hh // 2 + 1) * LANES]
        if hh % 2 == 1:
            vb = pltpu.roll(vb, HEAD_DIM, axis=1)
        v_ref[hh] = jnp.where(lane < HEAD_DIM, vb, ones_col).astype(BF16)


def _attn_pre(s, mod, g, w_qkv, qn, kn, p, pt, cos, sin, seq_len, fixed_row, rope):
    n, d = s.shape
    tm = min(PRE_TM, seq_len)
    tps = seq_len // tm
    n_seq = n // seq_len
    nq = N_HEADS * HEAD_DIM
    return pl.pallas_call(
        functools.partial(_attn_pre_kernel, rope),
        grid=(n // tm,),
        in_specs=[pl.BlockSpec((tm, d), lambda i: (i, 0)),
                  _mod_blockspec(d, tps, fixed_row),
                  _const_spec((1, d)),
                  _const_spec(w_qkv.shape),
                  _const_spec((1, nq)),
                  _const_spec((1, N_KV_HEADS * HEAD_DIM)),
                  _const_spec(p.shape),
                  _const_spec(pt.shape),
                  pl.BlockSpec((tm, LANES), lambda i: (i % tps, 0)),
                  pl.BlockSpec((tm, LANES), lambda i: (i % tps, 0))],
        out_specs=[pl.BlockSpec((tm, nq), lambda i: (i, 0)),
                   pl.BlockSpec((None, N_KV_HEADS, KT_ROWS, tm), lambda i: (i // tps, 0, 0, i % tps)),
                   pl.BlockSpec((None, N_KV_HEADS, tm, LANES), lambda i: (i // tps, 0, i % tps, 0))],
        out_shape=[jax.ShapeDtypeStruct((n, nq), BF16),
                   jax.ShapeDtypeStruct((n_seq, N_KV_HEADS, KT_ROWS, seq_len), BF16),
                   jax.ShapeDtypeStruct((n_seq, N_KV_HEADS, seq_len, LANES), BF16)],
        compiler_params=_cparams(("arbitrary",)),
        name="attn_pre",
    )(s, mod, g.reshape(1, d), w_qkv, qn, kn, p, pt, cos, sin)


def _rope_tables(seq_len):
    half = HEAD_DIM // 4
    t = np.arange(seq_len)
    pos = np.stack([t // GRID_W, t % GRID_W], axis=1).astype(np.float32)
    freqs = (ROPE_THETA ** (-np.arange(half, dtype=np.float32) / half)).astype(np.float32)
    lane = np.arange(LANES)
    axis = (lane % HEAD_DIM) // (HEAD_DIM // 2)
    e = lane % (HEAD_DIM // 2)
    ang = pos[:, axis] * freqs[e % half][None, :]
    sign = np.where(e < half, -1.0, 1.0).astype(np.float32)
    return jnp.asarray(np.cos(ang), F32), jnp.asarray(np.sin(ang) * sign[None, :], F32)


def _head_sum_matrices():
    lane = np.arange(N_HEADS * HEAD_DIM)
    p = (lane[:, None] // HEAD_DIM == np.arange(LANES)[None, :]).astype(np.float32)
    return jnp.asarray(p, BF16), jnp.asarray(p.T, BF16)


def _max_key_norm(kt_ref):
    k = kt_ref[:HEAD_DIM, :].astype(F32)
    return jnp.sqrt(jnp.max(jnp.sum(k * k, axis=0, keepdims=True), axis=1, keepdims=True))


def _attn_kernel(online, n_lat_chunks, tk, q_ref, ktc_ref, vc_ref, *rest):
    if n_lat_chunks:
        ktl_ref, vl_ref, o_ref, kmax_ref = rest
    else:
        o_ref, kmax_ref = rest
    if not online:
        @pl.when(pl.program_id(2) == 0)
        def _():
            kmax = _max_key_norm(ktc_ref)
            if n_lat_chunks:
                kmax = jnp.maximum(kmax, _max_key_norm(ktl_ref))
            kmax_ref[...] = kmax
    tq = ATT_TQ
    n_sub = q_ref.shape[0] // tq
    lane = lax.broadcasted_iota(jnp.int32, (tq, LANES), 1)

    def stacked_heads(qf):
        parts = []
        for g in range(KV_GROUP):
            blk = qf[:, (g // 2) * LANES:(g // 2 + 1) * LANES]
            if g % 2 == 1:
                blk = pltpu.roll(blk, HEAD_DIM, axis=1)
            qg = jnp.where(lane < HEAD_DIM, blk, 0.0)
            if not online:
                shift = jnp.sqrt(jnp.sum(qg * qg, axis=1, keepdims=True)) * kmax_ref[...]
                qg = jnp.where(lane == HEAD_DIM, -shift, qg)
            parts.append(qg.astype(BF16))
        return jnp.concatenate(parts, axis=0)

    q4s = [stacked_heads(q_ref[sub * tq:(sub + 1) * tq, :].astype(F32)) for sub in range(n_sub)]

    for sub in range(n_sub):
        q4 = q4s[sub]
        if online:
            def step(kt, v, carry, q4=q4):
                m, acc = carry
                s = jnp.dot(q4, kt, preferred_element_type=F32)
                m_new = jnp.maximum(m, jnp.max(s, axis=-1, keepdims=True))
                p = jnp.exp2(s - m_new)
                acc = jnp.exp2(m - m_new) * acc + jnp.dot(p.astype(BF16), v, preferred_element_type=F32)
                return m_new, acc
            carry = (jnp.full((KV_GROUP * tq, 1), -1e30, F32), jnp.zeros((KV_GROUP * tq, LANES), F32))
        else:
            def step(kt, v, acc, q4=q4):
                p = jnp.exp2(jnp.dot(q4, kt, preferred_element_type=F32))
                return acc + jnp.dot(p.astype(BF16), v, preferred_element_type=F32)
            carry = jnp.zeros((KV_GROUP * tq, LANES), F32)

        carry = step(ktc_ref[...], vc_ref[...], carry)
        if n_lat_chunks:
            def body(c, carry, step=step):
                off = pl.multiple_of(c * tk, tk)
                return step(ktl_ref[:, pl.ds(off, tk)], vl_ref[pl.ds(off, tk), :], carry)
            carry = lax.fori_loop(0, n_lat_chunks, body, carry, unroll=8 if online else True)
        acc = carry[1] if online else carry
        o = acc[:, :HEAD_DIM] / acc[:, HEAD_DIM:HEAD_DIM + 1]
        o_ref[sub * tq:(sub + 1) * tq, :] = jnp.concatenate(
            [o[g * tq:(g + 1) * tq, :] for g in range(KV_GROUP)], axis=1).astype(BF16)


def _attention(online, lq, q, ktc, vc, ktl=None, vl=None):
    n, nq = q.shape
    b = n // lq
    tq = min(lq, ATT_TQ * ATT_SUB)
    nqt = lq // tq
    gw = KV_GROUP * HEAD_DIM
    lc = ktc.shape[-1]
    in_specs = [pl.BlockSpec((tq, gw), lambda bb, j, i: (bb * nqt + i, j)),
                pl.BlockSpec((None, None, KT_ROWS, lc), lambda bb, j, i: (bb, j, 0, 0)),
                pl.BlockSpec((None, None, lc, LANES), lambda bb, j, i: (bb, j, 0, 0))]
    args = [q, ktc, vc]
    n_lat = 0
    if ktl is not None:
        ll = ktl.shape[-1]
        n_lat = ll // ATT_TK
        in_specs += [pl.BlockSpec((None, None, KT_ROWS, ll), lambda bb, j, i: (bb, j, 0, 0)),
                     pl.BlockSpec((None, None, ll, LANES), lambda bb, j, i: (bb, j, 0, 0))]
        args += [ktl, vl]
    return pl.pallas_call(
        functools.partial(_attn_kernel, online, n_lat, ATT_TK),
        grid=(b, N_KV_HEADS, nqt),
        in_specs=in_specs,
        out_specs=pl.BlockSpec((tq, gw), lambda bb, j, i: (bb * nqt + i, j)),
        out_shape=jax.ShapeDtypeStruct((n, nq), BF16),
        scratch_shapes=[pltpu.VMEM((1, 1), F32)],
        compiler_params=_cparams(("arbitrary", "arbitrary", "arbitrary")),
        name="attention",
    )(*args)


def _hyena_pre_kernel(tps, n_col, s_ref, prev_ref, next_ref, mod_ref, g_ref, w_ref, b_ref,
                      cw_ref, cb_ref, o_ref, h_ref, pre_ref):
    i = pl.program_id(0)
    tm = s_ref.shape[0]
    hb = HALO
    g = g_ref[...]
    shift = mod_ref[3:4, :]
    scale = mod_ref[4:5, :]
    h_ref[0:hb, :] = _modulated(prev_ref[...], g, shift, scale).astype(BF16)
    h_ref[hb:hb + tm, :] = _modulated(s_ref[...], g, shift, scale).astype(BF16)
    h_ref[hb + tm:, :] = _modulated(next_ref[...], g, shift, scale).astype(BF16)
    row = lax.broadcasted_iota(jnp.int32, (tm, 1), 0)
    drop_up = jnp.logical_and(row == 0, i % tps == 0)
    drop_dn = jnp.logical_and(row == tm - 1, i % tps == tps - 1)
    ct = w_ref.shape[1] // n_col
    for c in range(n_col):
        cols = slice(c * ct, (c + 1) * ct)
        pre_ref[...] = jnp.dot(h_ref[...], w_ref[:, cols], preferred_element_type=F32) + b_ref[:, cols]
        up = jnp.where(drop_up, 0.0, pre_ref[hb - 1:hb - 1 + tm, :])
        mid = pre_ref[hb:hb + tm, :]
        dn = jnp.where(drop_dn, 0.0, pre_ref[hb + 1:hb + 1 + tm, :])
        o_ref[:, cols] = (up * cw_ref[0:1, cols] + mid * cw_ref[1:2, cols]
                          + dn * cw_ref[2:3, cols] + cb_ref[:, cols])


def _hyena_pre(s, mod, g, w_in, b_in, conv_w, conv_b, seq_len, fixed_row):
    n, d = s.shape
    tm = min(PRE_TM, seq_len)
    tps = seq_len // tm
    n3 = w_in.shape[1]
    hb = HALO
    nblk = n // hb
    n_col = 6
    return pl.pallas_call(
        functools.partial(_hyena_pre_kernel, tps, n_col),
        grid=(n // tm,),
        in_specs=[pl.BlockSpec((tm, d), lambda i: (i, 0)),
                  pl.BlockSpec((hb, d), lambda i: (jnp.maximum(i * (tm // hb) - 1, 0), 0)),
                  pl.BlockSpec((hb, d), lambda i: (jnp.minimum((i + 1) * (tm // hb), nblk - 1), 0)),
                  _mod_blockspec(d, tps, fixed_row),
                  _const_spec((1, d)),
                  _const_spec(w_in.shape),
                  _const_spec((1, n3)),
                  _const_spec((3, n3)),
                  _const_spec((1, n3))],
        out_specs=pl.BlockSpec((tm, n3), lambda i: (i, 0)),
        out_shape=jax.ShapeDtypeStruct((n, n3), F32),
        scratch_shapes=[pltpu.VMEM((tm + 2 * hb, d), BF16),
                        pltpu.VMEM((tm + 2 * hb, n3 // n_col), F32)],
        compiler_params=_cparams(("arbitrary",)),
        name="hyena_pre",
    )(s, s, s, mod, g.reshape(1, d), w_in, b_in.reshape(1, n3), conv_w, conv_b.reshape(1, n3))


def _filter_kernel(feat_ref, w1_ref, b1_ref, w2_ref, b2_ref, w3_ref, b3_ref, a_ref, o_ref):
    hp = lax.Precision.HIGHEST
    a = a_ref[...]
    hid = jnp.sin(a * (jnp.dot(w1_ref[...], feat_ref[...], precision=hp, preferred_element_type=F32) + b1_ref[...]))
    hid = jnp.sin(a * (jnp.dot(w2_ref[...], hid, precision=hp, preferred_element_type=F32) + b2_ref[...]))
    hid = jnp.sin(a * (jnp.dot(w3_ref[...], hid, precision=hp, preferred_element_type=F32) + b3_ref[...]))
    o_ref[...] = hid.T


def _taps_from_hidden(hid, feats, w_fwd, w_bwd, delta):
    hb = hid.astype(BF16)
    t = feats[:, 0:1]
    use_fwd = feats[:, N_FILTER_FEAT:N_FILTER_FEAT + 1]
    use_bwd = feats[:, N_FILTER_FEAT + 1:N_FILTER_FEAT + 2]
    fwd = jnp.dot(hb, w_fwd, preferred_element_type=F32)
    bwd = jnp.dot(hb, w_bwd, preferred_element_type=F32)
    return (jnp.exp(-t * delta) * (use_fwd * fwd + use_bwd * bwd)).astype(BF16)


def _filter_features(seq_len):
    l = seq_len
    n_feat = N_FILTER_FEAT
    n = np.arange(2 * l)
    pos = np.where(n < l, n, 2 * l - n)
    pos = np.where(n == l, 0, pos)
    t = np.linspace(0.0, 1.0, l, dtype=np.float32)[pos]
    w = (2.0 * math.pi * pos.astype(np.float32) / l).astype(np.float32)
    bands = np.linspace(1e-4, FILTER_BANDS - 1, FILTER_BANDS, dtype=np.float32)
    bw = (bands[None, :] * w[:, None]).astype(np.float32)
    feats = np.zeros((2 * l, LANES), np.float32)
    feats[:, 0] = t
    feats[:, 1:1 + FILTER_BANDS] = np.cos(bw)
    feats[:, 1 + FILTER_BANDS:n_feat] = -np.sin(bw)
    feats[:, n_feat] = (n < l)
    feats[:, n_feat + 1] = np.logical_or(n > l, n == 0)
    return jnp.asarray(feats, F32), jnp.asarray(np.ascontiguousarray(feats.T), F32)


def _filter_hidden(feats_t, f_w1, f_b1, f_w2, f_b2, f_w3, f_b3, f_freq):
    n = feats_t.shape[1]
    fh = f_w1.shape[1]
    w1t = jnp.zeros((fh, LANES), F32).at[:, :N_FILTER_FEAT].set(f_w1.T)
    tm = min(512, n)
    return pl.pallas_call(
        _filter_kernel,
        grid=(n // tm,),
        in_specs=[pl.BlockSpec((LANES, tm), lambda i: (0, i)),
                  _const_spec((fh, LANES)), _const_spec((fh, 1)),
                  _const_spec((fh, fh)), _const_spec((fh, 1)),
                  _const_spec((fh, fh)), _const_spec((fh, 1)),
                  _const_spec((fh, 1))],
        out_specs=pl.BlockSpec((tm, fh), lambda i: (i, 0)),
        out_shape=jax.ShapeDtypeStruct((n, fh), F32),
        compiler_params=_cparams(("arbitrary",)),
        name="filter_hidden",
    )(feats_t, w1t, f_b1.reshape(fh, 1), f_w2.T, f_b2.reshape(fh, 1), f_w3.T, f_b3.reshape(fh, 1),
      f_freq.reshape(fh, 1))


def _decay_rates(d):
    min_decay = math.log(DECAY_TARGET) / SLOW_DECAY_PCT
    max_decay = math.log(DECAY_TARGET) / FAST_DECAY_PCT
    return jnp.abs(jnp.linspace(min_decay, max_decay, d, dtype=F32)).reshape(1, d)


def _dft_tables(seq_len):
    n_fft = 2 * seq_len
    n1 = n_fft // FFT_N2
    k1 = jnp.arange(n1, dtype=jnp.int32)[None, :, None]
    n2 = jnp.arange(FFT_N2, dtype=jnp.int32)[:, None, None]
    nn1 = jnp.arange(n1, dtype=jnp.int32)[None, None, :]
    ang = (2.0 * math.pi / n_fft) * ((k1 * (FFT_N2 * nn1 + n2)) % n_fft).astype(F32)
    c = jnp.cos(ang)
    s = jnp.sin(ang)
    h = n1 // 2
    g_first = jnp.concatenate([jnp.concatenate([c[:, :, :h], s[:, :, :h]], axis=2),
                               jnp.concatenate([-s[:, :, :h], c[:, :, :h]], axis=2)], axis=1)
    g_last = jnp.swapaxes(g_first, 1, 2) * (1.0 / n_fft)
    g_taps = jnp.concatenate([c, -s], axis=1)
    k2 = jnp.arange(FFT_N2, dtype=jnp.int32)
    ang2 = (2.0 * math.pi / FFT_N2) * ((k2[:, None] * k2[None, :]) % FFT_N2).astype(F32)
    c2, s2 = jnp.cos(ang2), jnp.sin(ang2)
    g_mid = jnp.concatenate([jnp.concatenate([c2, s2], axis=1),
                             jnp.concatenate([-s2, c2], axis=1)], axis=0)
    return (g_first.astype(BF16), g_last.astype(BF16), g_taps.astype(BF16),
            g_mid.astype(BF16), g_mid.T.astype(BF16))


def _stage_rows(x, rows_ref):
    for s in range(rows_ref.shape[0]):
        rows_ref[s] = x[:, s * LANES:(s + 1) * LANES]


def _column_group(rows_ref, j, r, nb):
    return jnp.concatenate([rows_ref[s, pl.ds(j, r, stride=nb), :] for s in range(rows_ref.shape[0])], axis=1)


def _scatter_column_group(stage_ref, j, y, nb):
    for s in range(stage_ref.shape[0]):
        stage_ref[s, pl.ds(j, y.shape[0], stride=nb), :] = y[:, s * LANES:(s + 1) * LANES]


def _staged_block(stage_ref, nb):
    n_slab, rows, _ = stage_ref.shape
    return jnp.concatenate([stage_ref[s].reshape(rows // nb, nb, LANES) for s in range(n_slab)], axis=2)


def _first_stage_dft(rows_ref, g_ref, stage_ref, r, nb):
    for j in range(nb):
        xj = _column_group(rows_ref, j, r, nb).astype(BF16)
        stage_ref[:, j, :] = jnp.dot(g_ref[j], xj, preferred_element_type=F32)
    return stage_ref[...].astype(BF16)


def _fft_first_kernel(x_ref, g_ref, o_ref, rows_ref, stage_ref):
    r, nb, ct = x_ref.shape
    _stage_rows(x_ref[...].reshape(r * nb, ct), rows_ref)
    o_ref[...] = _first_stage_dft(rows_ref, g_ref, stage_ref, r, nb)


def _fft_first(x3, col_block, d, g):
    r = x3.shape[0]
    rows_out = g.shape[1]
    nb, ct = FFT_NB, FFT_CT
    return pl.pallas_call(
        _fft_first_kernel,
        grid=(FFT_N2 // nb, d // ct),
        in_specs=[pl.BlockSpec((r, nb, ct), lambda i, c: (0, i, col_block * (d // ct) + c)),
                  pl.BlockSpec((nb, rows_out, r), lambda i, c: (i, 0, 0))],
        out_specs=pl.BlockSpec((rows_out, nb, ct), lambda i, c: (0, i, c)),
        out_shape=jax.ShapeDtypeStruct((rows_out, FFT_N2, d), BF16),
        scratch_shapes=[pltpu.VMEM((ct // LANES, r * nb, LANES), F32),
                        pltpu.VMEM((rows_out, nb, ct), F32)],
        compiler_params=_cparams(("arbitrary", "arbitrary")),
        name="fft_first",
    )(x3, g)


def _fft_taps_kernel(ct, h_ref, feat_ref, wo_ref, delta_ref, g_ref, o_ref, rows_ref, stage_ref):
    r, nb, fh = h_ref.shape
    d = wo_ref.shape[1] // 4
    col0 = pl.program_id(1) * ct
    hid = h_ref[...].reshape(r * nb, fh)
    feats = feat_ref[...].reshape(r * nb, LANES)
    for o in range(2):
        w_fwd = wo_ref[:, pl.ds(pl.multiple_of(2 * o * d + col0, ct), ct)]
        w_bwd = wo_ref[:, pl.ds(pl.multiple_of((2 * o + 1) * d + col0, ct), ct)]
        taps = _taps_from_hidden(hid, feats, w_fwd, w_bwd, delta_ref[...])
        _stage_rows(taps.astype(F32), rows_ref)
        o_ref[o] = _first_stage_dft(rows_ref, g_ref, stage_ref, r, nb)


def _fft_taps(hid, feats, w_out, deltas, g):
    n, fh = hid.shape
    n1 = n // FFT_N2
    d = deltas.shape[1]
    rows_out = g.shape[1]
    nb, ct = FFT_NB, FFT_CT
    return pl.pallas_call(
        functools.partial(_fft_taps_kernel, ct),
        grid=(FFT_N2 // nb, d // ct),
        in_specs=[pl.BlockSpec((n1, nb, fh), lambda i, c: (0, i, 0)),
                  pl.BlockSpec((n1, nb, LANES), lambda i, c: (0, i, 0)),
                  _const_spec(w_out.shape),
                  pl.BlockSpec((1, ct), lambda i, c: (0, c)),
                  pl.BlockSpec((nb, rows_out, n1), lambda i, c: (i, 0, 0))],
        out_specs=pl.BlockSpec((2, rows_out, nb, ct), lambda i, c: (0, 0, i, c)),
        out_shape=jax.ShapeDtypeStruct((2, rows_out, FFT_N2, d), BF16),
        scratch_shapes=[pltpu.VMEM((ct // LANES, n1 * nb, LANES), F32),
                        pltpu.VMEM((rows_out, nb, ct), F32)],
        compiler_params=_cparams(("arbitrary", "arbitrary")),
        name="fft_taps",
    )(hid.reshape(n1, FFT_N2, fh), feats.reshape(n1, FFT_N2, LANES), w_out, deltas, g)


def _fft_mid_kernel(t_ref, f_ref, g_ref, gi_ref, o_ref):
    half = FFT_N2
    for kk in range(FFT_KB):
        x = jnp.dot(g_ref[...], jnp.concatenate([t_ref[0, kk], t_ref[1, kk]], axis=0),
                    preferred_element_type=F32)
        h = jnp.dot(g_ref[...], jnp.concatenate([f_ref[0, kk], f_ref[1, kk]], axis=0),
                    preferred_element_type=F32)
        xr, xi = x[:half], x[half:]
        hr, hi = h[:half], h[half:]
        z = jnp.concatenate([xr * hr - xi * hi, xr * hi + xi * hr], axis=0).astype(BF16)
        y = jnp.dot(gi_ref[...], z, preferred_element_type=F32)
        o_ref[0, kk] = y[:half].astype(BF16)
        o_ref[1, kk] = y[half:].astype(BF16)


def _fft_mid(t, f, order, g_mid, g_mid_inv):
    rows, _, d = t.shape
    n1 = rows // 2
    blk = pl.BlockSpec((2, FFT_KB, FFT_N2, d), lambda i: (0, i, 0, 0))
    taps_blk = pl.BlockSpec((None, 2, FFT_KB, FFT_N2, d), lambda i: (order, 0, i, 0, 0))
    out = pl.pallas_call(
        _fft_mid_kernel,
        grid=(n1 // FFT_KB,),
        in_specs=[blk, taps_blk, _const_spec(g_mid.shape), _const_spec(g_mid_inv.shape)],
        out_specs=blk,
        out_shape=jax.ShapeDtypeStruct((2, n1, FFT_N2, d), BF16),
        compiler_params=_cparams(("arbitrary",)),
        name="fft_mid",
    )(t.reshape(2, n1, FFT_N2, d), f.reshape(2, 2, n1, FFT_N2, d), g_mid, g_mid_inv)
    return out.reshape(rows, FFT_N2, d)


def _fft_last_kernel(b_ref, g_ref, z_ref, gate_ref, bias_ref, o_ref, rows_ref, stage_ref):
    rows_in, nb, ct = b_ref.shape
    _stage_rows(b_ref[...].astype(F32).reshape(rows_in * nb, ct), rows_ref)
    for j in range(nb):
        bj = _column_group(rows_ref, j, rows_in, nb).astype(BF16)
        _scatter_column_group(stage_ref, j, jnp.dot(g_ref[j], bj, preferred_element_type=F32), nb)
    o_ref[...] = (gate_ref[...] * (_staged_block(stage_ref, nb) + z_ref[...] * bias_ref[...])).astype(o_ref.dtype)


def _fft_last(b, g, z3, z_col, gate3, gate_col, bias, out_dtype):
    rows_in, _, d = b.shape
    r = g.shape[1]
    nb, ct = FFT_NB, FFT_CT_LAST
    nc = d // ct
    return pl.pallas_call(
        _fft_last_kernel,
        grid=(FFT_N2 // nb, nc),
        in_specs=[pl.BlockSpec((rows_in, nb, ct), lambda i, c: (0, i, c)),
                  pl.BlockSpec((nb, r, rows_in), lambda i, c: (i, 0, 0)),
                  pl.BlockSpec((r, nb, ct), lambda i, c: (0, i, z_col * nc + c)),
                  pl.BlockSpec((r, nb, ct), lambda i, c: (0, i, gate_col * nc + c)),
                  pl.BlockSpec((1, 1, ct), lambda i, c: (0, 0, c))],
        out_specs=pl.BlockSpec((r, nb, ct), lambda i, c: (0, i, c)),
        out_shape=jax.ShapeDtypeStruct((r, FFT_N2, d), out_dtype),
        scratch_shapes=[pltpu.VMEM((ct // LANES, rows_in * nb, LANES), F32),
                        pltpu.VMEM((ct // LANES, r * nb, LANES), F32)],
        compiler_params=_cparams(("arbitrary", "arbitrary")),
        name="fft_last",
    )(b, g, z3, gate3, bias.reshape(1, 1, d))


def _hyena_long_convs(u, hid, feats, w_out, f_bias, seq_len, tables):
    n, d3 = u.shape
    d = d3 // 3
    g_first, g_last, g_taps, g_mid, g_mid_inv = tables
    n1 = 2 * seq_len // FFT_N2
    u3 = u.reshape(n1, FFT_N2, d3)
    f = _fft_taps(hid, feats, w_out, _decay_rates(d), g_taps)
    z3, z_col = u3, 0
    for o in range(2):
        a = _fft_first(z3, z_col, d, g_first)
        bq = _fft_mid(a, f, o, g_mid, g_mid_inv)
        z3 = _fft_last(bq, g_last, z3, z_col, u3, 1 + o, f_bias[o], F32 if o == 0 else BF16)
        z_col = 0
    return z3.reshape(n, d)


def _small_conv_kernel(ct, u_v_ref, u_x1_ref, u_x2_ref, h_ref, feat_ref, wo_ref, delta_ref,
                       gf_ref, gt_ref, gi_ref, bias_ref, o_ref):
    nf = gf_ref.shape[0] // 2
    d = wo_ref.shape[1] // 4
    col0 = pl.program_id(0) * ct
    z = u_v_ref[...]
    gates = (u_x1_ref, u_x2_ref)
    for o in range(2):
        w_fwd = wo_ref[:, pl.ds(pl.multiple_of(2 * o * d + col0, ct), ct)]
        w_bwd = wo_ref[:, pl.ds(pl.multiple_of((2 * o + 1) * d + col0, ct), ct)]
        taps = _taps_from_hidden(h_ref[...], feat_ref[...], w_fwd, w_bwd, delta_ref[...])
        spec = jnp.dot(gt_ref[...], taps, preferred_element_type=F32)
        zq = jnp.dot(gf_ref[...], z.astype(BF16), preferred_element_type=F32)
        zr, zi = zq[:nf], zq[nf:]
        hr, hi = spec[:nf], spec[nf:]
        prod = jnp.concatenate([zr * hr - zi * hi, zr * hi + zi * hr], axis=0).astype(BF16)
        y = jnp.dot(gi_ref[...], prod, preferred_element_type=F32)
        z = gates[o][...] * (y + z * bias_ref[o:o + 1, :])
    o_ref[...] = z


def _small_dft_tables(seq_len):
    n_fft = 2 * seq_len
    k = jnp.arange(n_fft, dtype=jnp.int32)
    ang = (2.0 * math.pi / n_fft) * ((k[:, None] * k[None, :]) % n_fft).astype(F32)
    c, s = jnp.cos(ang), jnp.sin(ang)
    cl, sl = c[:, :seq_len], s[:, :seq_len]
    g_fwd = jnp.concatenate([jnp.concatenate([cl, sl], axis=1),
                             jnp.concatenate([-sl, cl], axis=1)], axis=0)
    g_taps = jnp.concatenate([c, -s], axis=0)
    g_inv = g_fwd.T * (1.0 / n_fft)
    return g_fwd.astype(BF16), g_taps.astype(BF16), g_inv.astype(BF16)


def _hyena_small_convs(u, hid, feats, w_out, f_bias, seq_len):
    n, d3 = u.shape
    d = d3 // 3
    ct = 256
    nc = d // ct
    g_fwd, g_taps, g_inv = _small_dft_tables(seq_len)
    return pl.pallas_call(
        functools.partial(_small_conv_kernel, ct),
        grid=(nc,),
        in_specs=[pl.BlockSpec((n, ct), lambda c: (0, c)),
                  pl.BlockSpec((n, ct), lambda c: (0, nc + c)),
                  pl.BlockSpec((n, ct), lambda c: (0, 2 * nc + c)),
                  _const_spec(hid.shape), _const_spec(feats.shape), _const_spec(w_out.shape),
                  pl.BlockSpec((1, ct), lambda c: (0, c)),
                  _const_spec(g_fwd.shape), _const_spec(g_taps.shape), _const_spec(g_inv.shape),
                  pl.BlockSpec((2, ct), lambda c: (0, c))],
        out_specs=pl.BlockSpec((n, ct), lambda c: (0, c)),
        out_shape=jax.ShapeDtypeStruct((n, d), F32),
        compiler_params=_cparams(("arbitrary",)),
        name="small_conv",
    )(u, u, u, hid, feats, w_out, _decay_rates(d), g_fwd, g_taps, g_inv, f_bias)


def kernel(x, c, ctx, c_ctx, w_mod, b_mod, norm_w, ffn_w_gate_up, ffn_w_down, attn_w_qkv, attn_w_o,
           attn_q_norm, attn_k_norm, hy_w_in, hy_b_in, hy_conv_w, hy_conv_b, hy_f_w1, hy_f_b1,
           hy_f_w2, hy_f_b2, hy_f_w3, hy_f_b3, hy_f_wout, hy_f_freq, hy_f_bias, hy_w_out, hy_b_out):
    bsz, seq, d = x.shape
    ctx_len = ctx.shape[1]
    depth = w_mod.shape[0]
    assert bsz == 2, "the long convolution packs exactly two batches into one complex sequence"
    xs = x.reshape(bsz * seq, d)
    cs = ctx.reshape(bsz * ctx_len, d)

    cc = jnp.zeros((8, d), F32).at[:bsz].set(c).at[bsz].set(c_ctx)
    mod_all = _mod_vectors(cc, w_mod, b_mod).reshape(depth, 8, N_MOD, d)
    ctx_row = bsz

    c_rows = bsz * ctx_len
    p_sum, p_bcast = _head_sum_matrices()
    cos_t, sin_t = _rope_tables(seq)
    dft_tables = _dft_tables(seq)
    feats_x, feats_xt = _filter_features(seq)
    feats_c, feats_ct = _filter_features(ctx_len)
    wgu_all = ffn_w_gate_up.astype(BF16)
    wd_all = ffn_w_down.astype(BF16)
    zero_bias = jnp.zeros((d,), F32)

    for l in range(depth):
        mod = mod_all[l]
        is_attn = (l % 2) == 0
        ctx_out = l < depth - 1
        ctx_live = ctx_out or is_attn

        xs = _ffn(xs, mod, 0, norm_w[l, 0], wgu_all, wd_all, l, 0, seq, None)
        if ctx_live:
            cs = _ffn(cs, mod, 0, norm_w[l, 0], wgu_all, wd_all, l, 0, c_rows, ctx_row)

        if is_attn:
            a = l // 2
            w_qkv = attn_w_qkv[a].astype(BF16)
            w_o = attn_w_o[a].astype(BF16)
            qn = (jnp.tile(attn_q_norm[a], N_HEADS) * (HEAD_DIM ** -0.5 * math.log2(math.e))).reshape(1, -1)
            kn = jnp.tile(attn_k_norm[a], N_KV_HEADS).reshape(1, -1)
            q_l, kt_l, v_l = _attn_pre(xs, mod, norm_w[l, 1], w_qkv, qn, kn, p_sum, p_bcast,
                                       cos_t, sin_t, seq, None, True)
            q_c, kt_c, v_c = _attn_pre(cs, mod, norm_w[l, 1], w_qkv, qn, kn, p_sum, p_bcast,
                                       cos_t, sin_t, ctx_len, ctx_row, False)
            shift_ok = (HEAD_DIM * jnp.max(jnp.abs(qn)) * jnp.max(jnp.abs(kn))) < ATT_SHIFT_LIMIT
            o_l = lax.cond(shift_ok, functools.partial(_attention, False, seq),
                           functools.partial(_attention, True, seq), q_l, kt_c, v_c, kt_l, v_l)
            mix_x = (o_l, w_o, zero_bias)
            if ctx_out:
                o_c = lax.cond(shift_ok, functools.partial(_attention, False, ctx_len),
                               functools.partial(_attention, True, ctx_len), q_c, kt_c, v_c)
                mix_c = (o_c, w_o, zero_bias)
        else:
            j = l // 2
            w_in = hy_w_in[j].astype(BF16)
            w_out = hy_w_out[j].astype(BF16)
            fargs = (hy_f_w1[j], hy_f_b1[j], hy_f_w2[j], hy_f_b2[j], hy_f_w3[j], hy_f_b3[j], hy_f_freq[j])
            f_wout = hy_f_wout[j].astype(BF16)
            u_l = _hyena_pre(xs, mod, norm_w[l, 1], w_in, hy_b_in[j], hy_conv_w[j], hy_conv_b[j], seq, None)
            y_l = _hyena_long_convs(u_l, _filter_hidden(feats_xt, *fargs), feats_x, f_wout, hy_f_bias[j],
                                    seq, dft_tables)
            if ctx_out:
                u_c = _hyena_pre(cs, mod, norm_w[l, 1], w_in, hy_b_in[j], hy_conv_w[j], hy_conv_b[j],
                                 ctx_len, ctx_row)
                y_c = _hyena_small_convs(u_c, _filter_hidden(feats_ct, *fargs), feats_c, f_wout,
                                         hy_f_bias[j], ctx_len)
                mix_c = (y_c, w_out, hy_b_out[j])
            mix_x = (y_l, w_out, hy_b_out[j])

        xs = _ffn(xs, mod, 2, norm_w[l, 2], wgu_all, wd_all, l, 1, seq, None, mixer=mix_x)
        if ctx_out:
            cs = _ffn(cs, mod, 2, norm_w[l, 2], wgu_all, wd_all, l, 1, c_rows, ctx_row, mixer=mix_c)
    return xs.reshape(bsz, seq, d)
```

```python
import functools
import math

import jax
import jax.numpy as jnp
import numpy as np
from jax import lax
from jax.experimental import pallas as pl
from jax.experimental.pallas import tpu as pltpu

F32 = jnp.float32
BF16 = jnp.bfloat16

N_MOD = 9
N_HEADS = 16
N_KV_HEADS = 4
HEAD_DIM = 64
KV_GROUP = N_HEADS // N_KV_HEADS
KT_ROWS = 2 * HEAD_DIM
ATT_SHIFT_LIMIT = 60.0
GRID_W = 64
ROPE_THETA = 10000.0
EPS = 1e-6
FILTER_BANDS = 16
N_FILTER_FEAT = 1 + 2 * FILTER_BANDS
DECAY_TARGET = 1e-2
FAST_DECAY_PCT = 0.3
SLOW_DECAY_PCT = 1.5

LANES = 128
HALO = 16
FFT_N2 = 128
VMEM_LIMIT = 56 * 1024 * 1024

FFN_TM = 1024
PRE_TM = 256
FFN_FC = 256
ATT_TQ = 256
ATT_SUB = 2
ATT_TK = 512
FFT_NB = 16
FFT_KB = 8
FFT_CT = 512
FFT_CT_LAST = 256


def _cparams(sem):
    return pltpu.CompilerParams(dimension_semantics=sem, vmem_limit_bytes=VMEM_LIMIT)


def _const_spec(shape):
    nd = len(shape)
    return pl.BlockSpec(shape, lambda *_: (0,) * nd, pipeline_mode=pl.Buffered(1))


def _modulated(s, g, shift, scale):
    ms = jnp.mean(s * s, axis=-1, keepdims=True)
    return (s * lax.rsqrt(ms + EPS) * g) * (1.0 + scale) + shift


def _mod_kernel(c_ref, w_ref, b_ref, o_ref):
    c = c_ref[...]
    a = (c * jax.nn.sigmoid(c)).astype(BF16)
    o_ref[0] = jnp.dot(a, w_ref[0].astype(BF16), preferred_element_type=F32) + b_ref[0]


def _mod_vectors(cc, w_mod, b_mod):
    depth, d, n = w_mod.shape
    tn = n // 6
    return pl.pallas_call(
        _mod_kernel,
        grid=(depth, n // tn),
        in_specs=[pl.BlockSpec((8, d), lambda l, j: (0, 0)),
                  pl.BlockSpec((1, d, tn), lambda l, j: (l, 0, j)),
                  pl.BlockSpec((1, 1, tn), lambda l, j: (l, 0, j))],
        out_specs=pl.BlockSpec((1, 8, tn), lambda l, j: (l, 0, j)),
        out_shape=jax.ShapeDtypeStruct((depth, 8, n), F32),
        compiler_params=_cparams(("arbitrary", "arbitrary")),
        name="mod_vectors",
    )(cc, w_mod, b_mod.reshape(depth, 1, n))


def _mod_blockspec(d, tiles_per_row, fixed_row):
    if fixed_row is None:
        return pl.BlockSpec((None, N_MOD, d), lambda i: (i // tiles_per_row, 0, 0))
    return pl.BlockSpec((None, N_MOD, d), lambda i: (fixed_row, 0, 0))


def _ffn_kernel(k, with_mixer, s_ref, mod_ref, g_ref, wgu_ref, wd_ref, *rest):
    if with_mixer:
        a_ref, wp_ref, bp_ref, o_ref, h_ref, acc_ref = rest
        mixed = jnp.dot(a_ref[...].astype(BF16), wp_ref[...], preferred_element_type=F32) + bp_ref[...]
        s = s_ref[...] + mod_ref[5:6, :] * mixed
    else:
        o_ref, h_ref, acc_ref = rest
        s = s_ref[...]
    shift = mod_ref[3 * k:3 * k + 1, :]
    scale = mod_ref[3 * k + 1:3 * k + 2, :]
    gate = mod_ref[3 * k + 2:3 * k + 3, :]
    h_ref[...] = _modulated(s, g_ref[...], shift, scale).astype(BF16)
    f = wd_ref.shape[0]
    for c in range(f // FFN_FC):
        lo = c * FFN_FC
        g = jnp.dot(h_ref[...], wgu_ref[:, lo:lo + FFN_FC], preferred_element_type=F32)
        u = jnp.dot(h_ref[...], wgu_ref[:, f + lo:f + lo + FFN_FC], preferred_element_type=F32)
        a = (g * jax.nn.sigmoid(g) * u).astype(BF16)
        y = jnp.dot(a, wd_ref[lo:lo + FFN_FC, :], preferred_element_type=F32)
        if c == 0:
            acc_ref[...] = y
        else:
            acc_ref[...] += y
    o_ref[...] = s + 0.5 * gate * acc_ref[...]


def _ffn(s, mod, k, g, wgu_all, wd_all, layer, which, rows_per_cond, fixed_row, mixer=None):
    n, d = s.shape
    tm = min(FFN_TM, n)
    tiles_per_row = rows_per_cond // tm
    f = wd_all.shape[2]
    assert f % FFN_FC == 0

    def weight_spec(shape):
        return pl.BlockSpec((None, None) + shape, lambda i: (layer, which, 0, 0), pipeline_mode=pl.Buffered(1))

    in_specs = [pl.BlockSpec((tm, d), lambda i: (i, 0)),
                _mod_blockspec(d, tiles_per_row, fixed_row),
                _const_spec((1, d)),
                weight_spec((d, 2 * f)),
                weight_spec((f, d))]
    args = [s, mod, g.reshape(1, d), wgu_all, wd_all]
    if mixer is not None:
        a, w, b = mixer
        in_specs += [pl.BlockSpec((tm, a.shape[1]), lambda i: (i, 0)), _const_spec(w.shape), _const_spec((1, d))]
        args += [a, w, b.reshape(1, d)]
    return pl.pallas_call(
        functools.partial(_ffn_kernel, k, mixer is not None),
        grid=(n // tm,),
        in_specs=in_specs,
        out_specs=pl.BlockSpec((tm, d), lambda i: (i, 0)),
        out_shape=jax.ShapeDtypeStruct((n, d), F32),
        scratch_shapes=[pltpu.VMEM((tm, d), BF16), pltpu.VMEM((tm, d), F32)],
        compiler_params=_cparams(("arbitrary",)),
        name="ffn",
    )(*args)


def _head_norm(x, p, pt, w):
    sq = x * x
    hi = sq.astype(BF16)
    lo = (sq - hi.astype(F32)).astype(BF16)
    ss = jnp.dot(hi, p, preferred_element_type=F32) + jnp.dot(lo, p, preferred_element_type=F32)
    r = lax.rsqrt(ss * (1.0 / HEAD_DIM) + EPS)
    rh = r.astype(BF16)
    rl = (r - rh.astype(F32)).astype(BF16)
    rb = jnp.dot(rh, pt, preferred_element_type=F32) + jnp.dot(rl, pt, preferred_element_type=F32)
    return x * rb * w


def _rope_block(xb, cos, sin):
    lane = lax.broadcasted_iota(jnp.int32, xb.shape, 1)
    fwd = pltpu.roll(xb, 16, axis=1)
    bwd = pltpu.roll(xb, LANES - 16, axis=1)
    partner = jnp.where((lane % 32) < 16, bwd, fwd)
    return xb * cos + partner * sin


def _attn_pre_kernel(rope, s_ref, mod_ref, g_ref, w_ref, qn_ref, kn_ref, p_ref, pt_ref,
                     cos_ref, sin_ref, q_ref, kt_ref, v_ref):
    s = s_ref[...]
    tm = s.shape[0]
    nq = N_HEADS * HEAD_DIM
    nk = N_KV_HEADS * HEAD_DIM
    h = _modulated(s, g_ref[...], mod_ref[3:4, :], mod_ref[4:5, :]).astype(BF16)
    qkv = jnp.dot(h, w_ref[...], preferred_element_type=F32)
    q = _head_norm(qkv[:, :nq], p_ref[...], pt_ref[...], qn_ref[...])
    k = _head_norm(qkv[:, nq:nq + nk], p_ref[:nk, :], pt_ref[:, :nk], kn_ref[...])
    v = qkv[:, nq + nk:]
    cos = cos_ref[...]
    sin = sin_ref[...]
    for j in range(nq // LANES):
        qb = q[:, j * LANES:(j + 1) * LANES]
        if rope:
            qb = _rope_block(qb, cos, sin)
        q_ref[:, j * LANES:(j + 1) * LANES] = qb.astype(BF16)
    kblocks = []
    for j in range(nk // LANES):
        kb = k[:, j * LANES:(j + 1) * LANES]
        if rope:
            kb = _rope_block(kb, cos, sin)
        kblocks.append(kb)
    kt = jnp.concatenate(kblocks, axis=1).T.astype(BF16)
    tail_row = lax.broadcasted_iota(jnp.int32, (KT_ROWS - HEAD_DIM, tm), 0)
    tail = jnp.where(tail_row == 0, 1.0, 0.0).astype(BF16)
    for hh in range(N_KV_HEADS):
        kt_ref[hh] = jnp.concatenate([kt[hh * HEAD_DIM:(hh + 1) * HEAD_DIM, :], tail], axis=0)
    lane = lax.broadcasted_iota(jnp.int32, (tm, LANES), 1)
    ones_col = jnp.where(lane == HEAD_DIM, 1.0, 0.0)
    for hh in range(N_KV_HEADS):
        vb = v[:, (hh // 2) * LANES:(hh // 2 + 1) * LANES]
        if hh % 2 == 1:
            vb = pltpu.roll(vb, HEAD_DIM, axis=1)
        v_ref[hh] = jnp.where(lane < HEAD_DIM, vb, ones_col).astype(BF16)


def _attn_pre(s, mod, g, w_qkv, qn, kn, p, pt, cos, sin, seq_len, fixed_row, rope):
    n, d = s.shape
    tm = min(PRE_TM, seq_len)
    tps = seq_len // tm
    n_seq = n // seq_len
    nq = N_HEADS * HEAD_DIM
    return pl.pallas_call(
        functools.partial(_attn_pre_kernel, rope),
        grid=(n // tm,),
        in_specs=[pl.BlockSpec((tm, d), lambda i: (i, 0)),
                  _mod_blockspec(d, tps, fixed_row),
                  _const_spec((1, d)),
                  _const_spec(w_qkv.shape),
                  _const_spec((1, nq)),
                  _const_spec((1, N_KV_HEADS * HEAD_DIM)),
                  _const_spec(p.shape),
                  _const_spec(pt.shape),
                  pl.BlockSpec((tm, LANES), lambda i: (i % tps, 0)),
                  pl.BlockSpec((tm, LANES), lambda i: (i % tps, 0))],
        out_specs=[pl.BlockSpec((tm, nq), lambda i: (i, 0)),
                   pl.BlockSpec((None, N_KV_HEADS, KT_ROWS, tm), lambda i: (i // tps, 0, 0, i % tps)),
                   pl.BlockSpec((None, N_KV_HEADS, tm, LANES), lambda i: (i // tps, 0, i % tps, 0))],
        out_shape=[jax.ShapeDtypeStruct((n, nq), BF16),
                   jax.ShapeDtypeStruct((n_seq, N_KV_HEADS, KT_ROWS, seq_len), BF16),
                   jax.ShapeDtypeStruct((n_seq, N_KV_HEADS, seq_len, LANES), BF16)],
        compiler_params=_cparams(("arbitrary",)),
        name="attn_pre",
    )(s, mod, g.reshape(1, d), w_qkv, qn, kn, p, pt, cos, sin)


def _rope_tables(seq_len):
    half = HEAD_DIM // 4
    t = np.arange(seq_len)
    pos = np.stack([t // GRID_W, t % GRID_W], axis=1).astype(np.float32)
    freqs = (ROPE_THETA ** (-np.arange(half, dtype=np.float32) / half)).astype(np.float32)
    lane = np.arange(LANES)
    axis = (lane % HEAD_DIM) // (HEAD_DIM // 2)
    e = lane % (HEAD_DIM // 2)
    ang = pos[:, axis] * freqs[e % half][None, :]
    sign = np.where(e < half, -1.0, 1.0).astype(np.float32)
    return jnp.asarray(np.cos(ang), F32), jnp.asarray(np.sin(ang) * sign[None, :], F32)


def _head_sum_matrices():
    lane = np.arange(N_HEADS * HEAD_DIM)
    p = (lane[:, None] // HEAD_DIM == np.arange(LANES)[None, :]).astype(np.float32)
    return jnp.asarray(p, BF16), jnp.asarray(p.T, BF16)


def _max_key_norm(kt_ref):
    k = kt_ref[:HEAD_DIM, :].astype(F32)
    return jnp.sqrt(jnp.max(jnp.sum(k * k, axis=0, keepdims=True), axis=1, keepdims=True))


def _attn_kernel(online, n_lat_chunks, tk, q_ref, ktc_ref, vc_ref, *rest):
    if n_lat_chunks:
        ktl_ref, vl_ref, o_ref, kmax_ref = rest
    else:
        o_ref, kmax_ref = rest
    if not online:
        @pl.when(pl.program_id(2) == 0)
        def _():
            kmax = _max_key_norm(ktc_ref)
            if n_lat_chunks:
                kmax = jnp.maximum(kmax, _max_key_norm(ktl_ref))
            kmax_ref[...] = kmax
    tq = ATT_TQ
    n_sub = q_ref.shape[0] // tq
    lane = lax.broadcasted_iota(jnp.int32, (tq, LANES), 1)

    def stacked_heads(qf):
        parts = []
        for g in range(KV_GROUP):
            blk = qf[:, (g // 2) * LANES:(g // 2 + 1) * LANES]
            if g % 2 == 1:
                blk = pltpu.roll(blk, HEAD_DIM, axis=1)
            qg = jnp.where(lane < HEAD_DIM, blk, 0.0)
            if not online:
                shift = jnp.sqrt(jnp.sum(qg * qg, axis=1, keepdims=True)) * kmax_ref[...]
                qg = jnp.where(lane == HEAD_DIM, -shift, qg)
            parts.append(qg.astype(BF16))
        return jnp.concatenate(parts, axis=0)

    q4s = [stacked_heads(q_ref[sub * tq:(sub + 1) * tq, :].astype(F32)) for sub in range(n_sub)]

    for sub in range(n_sub):
        q4 = q4s[sub]
        if online:
            def step(kt, v, carry, q4=q4):
                m, acc = carry
                s = jnp.dot(q4, kt, preferred_element_type=F32)
                m_new = jnp.maximum(m, jnp.max(s, axis=-1, keepdims=True))
                p = jnp.exp2(s - m_new)
                acc = jnp.exp2(m - m_new) * acc + jnp.dot(p.astype(BF16), v, preferred_element_type=F32)
                return m_new, acc
            carry = (jnp.full((KV_GROUP * tq, 1), -1e30, F32), jnp.zeros((KV_GROUP * tq, LANES), F32))
        else:
            def step(kt, v, acc, q4=q4):
                p = jnp.exp2(jnp.dot(q4, kt, preferred_element_type=F32))
                return acc + jnp.dot(p.astype(BF16), v, preferred_element_type=F32)
            carry = jnp.zeros((KV_GROUP * tq, LANES), F32)

        carry = step(ktc_ref[...], vc_ref[...], carry)
        if n_lat_chunks:
            def body(c, carry, step=step):
                off = pl.multiple_of(c * tk, tk)
                return step(ktl_ref[:, pl.ds(off, tk)], vl_ref[pl.ds(off, tk), :], carry)
            carry = lax.fori_loop(0, n_lat_chunks, body, carry, unroll=8 if online else True)
        acc = carry[1] if online else carry
        o = acc[:, :HEAD_DIM] / acc[:, HEAD_DIM:HEAD_DIM + 1]
        o_ref[sub * tq:(sub + 1) * tq, :] = jnp.concatenate(
            [o[g * tq:(g + 1) * tq, :] for g in range(KV_GROUP)], axis=1).astype(BF16)


def _attention(online, lq, q, ktc, vc, ktl=None, vl=None):
    n, nq = q.shape
    b = n // lq
    tq = min(lq, ATT_TQ * ATT_SUB)
    nqt = lq // tq
    gw = KV_GROUP * HEAD_DIM
    lc = ktc.shape[-1]
    in_specs = [pl.BlockSpec((tq, gw), lambda bb, j, i: (bb * nqt + i, j)),
                pl.BlockSpec((None, None, KT_ROWS, lc), lambda bb, j, i: (bb, j, 0, 0)),
                pl.BlockSpec((None, None, lc, LANES), lambda bb, j, i: (bb, j, 0, 0))]
    args = [q, ktc, vc]
    n_lat = 0
    if ktl is not None:
        ll = ktl.shape[-1]
        n_lat = ll // ATT_TK
        in_specs += [pl.BlockSpec((None, None, KT_ROWS, ll), lambda bb, j, i: (bb, j, 0, 0)),
                     pl.BlockSpec((None, None, ll, LANES), lambda bb, j, i: (bb, j, 0, 0))]
        args += [ktl, vl]
    return pl.pallas_call(
        functools.partial(_attn_kernel, online, n_lat, ATT_TK),
        grid=(b, N_KV_HEADS, nqt),
        in_specs=in_specs,
        out_specs=pl.BlockSpec((tq, gw), lambda bb, j, i: (bb * nqt + i, j)),
        out_shape=jax.ShapeDtypeStruct((n, nq), BF16),
        scratch_shapes=[pltpu.VMEM((1, 1), F32)],
        compiler_params=_cparams(("arbitrary", "arbitrary", "arbitrary")),
        name="attention",
    )(*args)


def _hyena_pre_kernel(tps, n_col, s_ref, prev_ref, next_ref, mod_ref, g_ref, w_ref, b_ref,
                      cw_ref, cb_ref, o_ref, h_ref, pre_ref):
    i = pl.program_id(0)
    tm = s_ref.shape[0]
    hb = HALO
    g = g_ref[...]
    shift = mod_ref[3:4, :]
    scale = mod_ref[4:5, :]
    h_ref[0:hb, :] = _modulated(prev_ref[...], g, shift, scale).astype(BF16)
    h_ref[hb:hb + tm, :] = _modulated(s_ref[...], g, shift, scale).astype(BF16)
    h_ref[hb + tm:, :] = _modulated(next_ref[...], g, shift, scale).astype(BF16)
    row = lax.broadcasted_iota(jnp.int32, (tm, 1), 0)
    drop_up = jnp.logical_and(row == 0, i % tps == 0)
    drop_dn = jnp.logical_and(row == tm - 1, i % tps == tps - 1)
    ct = w_ref.shape[1] // n_col
    for c in range(n_col):
        cols = slice(c * ct, (c + 1) * ct)
        pre_ref[...] = jnp.dot(h_ref[...], w_ref[:, cols], preferred_element_type=F32) + b_ref[:, cols]
        up = jnp.where(drop_up, 0.0, pre_ref[hb - 1:hb - 1 + tm, :])
        mid = pre_ref[hb:hb + tm, :]
        dn = jnp.where(drop_dn, 0.0, pre_ref[hb + 1:hb + 1 + tm, :])
        o_ref[:, cols] = (up * cw_ref[0:1, cols] + mid * cw_ref[1:2, cols]
                          + dn * cw_ref[2:3, cols] + cb_ref[:, cols])


def _hyena_pre(s, mod, g, w_in, b_in, conv_w, conv_b, seq_len, fixed_row):
    n, d = s.shape
    tm = min(PRE_TM, seq_len)
    tps = seq_len // tm
    n3 = w_in.shape[1]
    hb = HALO
    nblk = n // hb
    n_col = 6
    return pl.pallas_call(
        functools.partial(_hyena_pre_kernel, tps, n_col),
        grid=(n // tm,),
        in_specs=[pl.BlockSpec((tm, d), lambda i: (i, 0)),
                  pl.BlockSpec((hb, d), lambda i: (jnp.maximum(i * (tm // hb) - 1, 0), 0)),
                  pl.BlockSpec((hb, d), lambda i: (jnp.minimum((i + 1) * (tm // hb), nblk - 1), 0)),
                  _mod_blockspec(d, tps, fixed_row),
                  _const_spec((1, d)),
                  _const_spec(w_in.shape),
                  _const_spec((1, n3)),
                  _const_spec((3, n3)),
                  _const_spec((1, n3))],
        out_specs=pl.BlockSpec((tm, n3), lambda i: (i, 0)),
        out_shape=jax.ShapeDtypeStruct((n, n3), F32),
        scratch_shapes=[pltpu.VMEM((tm + 2 * hb, d), BF16),
                        pltpu.VMEM((tm + 2 * hb, n3 // n_col), F32)],
        compiler_params=_cparams(("arbitrary",)),
        name="hyena_pre",
    )(s, s, s, mod, g.reshape(1, d), w_in, b_in.reshape(1, n3), conv_w, conv_b.reshape(1, n3))


def _filter_kernel(feat_ref, w1_ref, b1_ref, w2_ref, b2_ref, w3_ref, b3_ref, a_ref, o_ref):
    hp = lax.Precision.HIGHEST
    a = a_ref[...]
    hid = jnp.sin(a * (jnp.dot(w1_ref[...], feat_ref[...], precision=hp, preferred_element_type=F32) + b1_ref[...]))
    hid = jnp.sin(a * (jnp.dot(w2_ref[...], hid, precision=hp, preferred_element_type=F32) + b2_ref[...]))
    hid = jnp.sin(a * (jnp.dot(w3_ref[...], hid, precision=hp, preferred_element_type=F32) + b3_ref[...]))
    o_ref[...] = hid.T


def _taps_from_hidden(hid, feats, w_fwd, w_bwd, delta):
    hb = hid.astype(BF16)
    t = feats[:, 0:1]
    use_fwd = feats[:, N_FILTER_FEAT:N_FILTER_FEAT + 1]
    use_bwd = feats[:, N_FILTER_FEAT + 1:N_FILTER_FEAT + 2]
    fwd = jnp.dot(hb, w_fwd, preferred_element_type=F32)
    bwd = jnp.dot(hb, w_bwd, preferred_element_type=F32)
    return (jnp.exp(-t * delta) * (use_fwd * fwd + use_bwd * bwd)).astype(BF16)


def _filter_features(seq_len):
    l = seq_len
    n_feat = N_FILTER_FEAT
    n = np.arange(2 * l)
    pos = np.where(n < l, n, 2 * l - n)
    pos = np.where(n == l, 0, pos)
    t = np.linspace(0.0, 1.0, l, dtype=np.float32)[pos]
    w = (2.0 * math.pi * pos.astype(np.float32) / l).astype(np.float32)
    bands = np.linspace(1e-4, FILTER_BANDS - 1, FILTER_BANDS, dtype=np.float32)
    bw = (bands[None, :] * w[:, None]).astype(np.float32)
    feats = np.zeros((2 * l, LANES), np.float32)
    feats[:, 0] = t
    feats[:, 1:1 + FILTER_BANDS] = np.cos(bw)
    feats[:, 1 + FILTER_BANDS:n_feat] = -np.sin(bw)
    feats[:, n_feat] = (n < l)
    feats[:, n_feat + 1] = np.logical_or(n > l, n == 0)
    return jnp.asarray(feats, F32), jnp.asarray(np.ascontiguousarray(feats.T), F32)


def _filter_hidden(feats_t, f_w1, f_b1, f_w2, f_b2, f_w3, f_b3, f_freq):
    n = feats_t.shape[1]
    fh = f_w1.shape[1]
    w1t = jnp.zeros((fh, LANES), F32).at[:, :N_FILTER_FEAT].set(f_w1.T)
    tm = min(512, n)
    return pl.pallas_call(
        _filter_kernel,
        grid=(n // tm,),
        in_specs=[pl.BlockSpec((LANES, tm), lambda i: (0, i)),
                  _const_spec((fh, LANES)), _const_spec((fh, 1)),
                  _const_spec((fh, fh)), _const_spec((fh, 1)),
                  _const_spec((fh, fh)), _const_spec((fh, 1)),
                  _const_spec((fh, 1))],
        out_specs=pl.BlockSpec((tm, fh), lambda i: (i, 0)),
        out_shape=jax.ShapeDtypeStruct((n, fh), F32),
        compiler_params=_cparams(("arbitrary",)),
        name="filter_hidden",
    )(feats_t, w1t, f_b1.reshape(fh, 1), f_w2.T, f_b2.reshape(fh, 1), f_w3.T, f_b3.reshape(fh, 1),
      f_freq.reshape(fh, 1))


def _decay_rates(d):
    min_decay = math.log(DECAY_TARGET) / SLOW_DECAY_PCT
    max_decay = math.log(DECAY_TARGET) / FAST_DECAY_PCT
    return jnp.abs(jnp.linspace(min_decay, max_decay, d, dtype=F32)).reshape(1, d)


def _dft_tables(seq_len):
    n_fft = 2 * seq_len
    n1 = n_fft // FFT_N2
    k1 = jnp.arange(n1, dtype=jnp.int32)[None, :, None]
    n2 = jnp.arange(FFT_N2, dtype=jnp.int32)[:, None, None]
    nn1 = jnp.arange(n1, dtype=jnp.int32)[None, None, :]
    ang = (2.0 * math.pi / n_fft) * ((k1 * (FFT_N2 * nn1 + n2)) % n_fft).astype(F32)
    c = jnp.cos(ang)
    s = jnp.sin(ang)
    h = n1 // 2
    g_first = jnp.concatenate([jnp.concatenate([c[:, :, :h], s[:, :, :h]], axis=2),
                               jnp.concatenate([-s[:, :, :h], c[:, :, :h]], axis=2)], axis=1)
    g_last = jnp.swapaxes(g_first, 1, 2) * (1.0 / n_fft)
    g_taps = jnp.concatenate([c, -s], axis=1)
    k2 = jnp.arange(FFT_N2, dtype=jnp.int32)
    ang2 = (2.0 * math.pi / FFT_N2) * ((k2[:, None] * k2[None, :]) % FFT_N2).astype(F32)
    c2, s2 = jnp.cos(ang2), jnp.sin(ang2)
    g_mid = jnp.concatenate([jnp.concatenate([c2, s2], axis=1),
                             jnp.concatenate([-s2, c2], axis=1)], axis=0)
    return (g_first.astype(BF16), g_last.astype(BF16), g_taps.astype(BF16),
            g_mid.astype(BF16), g_mid.T.astype(BF16))


def _stage_rows(x, rows_ref):
    for s in range(rows_ref.shape[0]):
        rows_ref[s] = x[:, s * LANES:(s + 1) * LANES]


def _column_group(rows_ref, j, r, nb):
    return jnp.concatenate([rows_ref[s, pl.ds(j, r, stride=nb), :] for s in range(rows_ref.shape[0])], axis=1)


def _scatter_column_group(stage_ref, j, y, nb):
    for s in range(stage_ref.shape[0]):
        stage_ref[s, pl.ds(j, y.shape[0], stride=nb), :] = y[:, s * LANES:(s + 1) * LANES]


def _staged_block(stage_ref, nb):
    n_slab, rows, _ = stage_ref.shape
    return jnp.concatenate([stage_ref[s].reshape(rows // nb, nb, LANES) for s in range(n_slab)], axis=2)


def _first_stage_dft(rows_ref, g_ref, stage_ref, r, nb):
    for j in range(nb):
        xj = _column_group(rows_ref, j, r, nb).astype(BF16)
        stage_ref[:, j, :] = jnp.dot(g_ref[j], xj, preferred_element_type=F32)
    return stage_ref[...].astype(BF16)


def _fft_first_kernel(x_ref, g_ref, o_ref, rows_ref, stage_ref):
    r, nb, ct = x_ref.shape
    _stage_rows(x_ref[...].reshape(r * nb, ct), rows_ref)
    o_ref[...] = _first_stage_dft(rows_ref, g_ref, stage_ref, r, nb)


def _fft_first(x3, col_block, d, g):
    r = x3.shape[0]
    rows_out = g.shape[1]
    nb, ct = FFT_NB, FFT_CT
    return pl.pallas_call(
        _fft_first_kernel,
        grid=(FFT_N2 // nb, d // ct),
        in_specs=[pl.BlockSpec((r, nb, ct), lambda i, c: (0, i, col_block * (d // ct) + c)),
                  pl.BlockSpec((nb, rows_out, r), lambda i, c: (i, 0, 0))],
        out_specs=pl.BlockSpec((rows_out, nb, ct), lambda i, c: (0, i, c)),
        out_shape=jax.ShapeDtypeStruct((rows_out, FFT_N2, d), BF16),
        scratch_shapes=[pltpu.VMEM((ct // LANES, r * nb, LANES), F32),
                        pltpu.VMEM((rows_out, nb, ct), F32)],
        compiler_params=_cparams(("arbitrary", "arbitrary")),
        name="fft_first",
    )(x3, g)


def _fft_taps_kernel(ct, h_ref, feat_ref, wo_ref, delta_ref, g_ref, o_ref, rows_ref, stage_ref):
    r, nb, fh = h_ref.shape
    d = wo_ref.shape[1] // 4
    col0 = pl.program_id(1) * ct
    hid = h_ref[...].reshape(r * nb, fh)
    feats = feat_ref[...].reshape(r * nb, LANES)
    for o in range(2):
        w_fwd = wo_ref[:, pl.ds(pl.multiple_of(2 * o * d + col0, ct), ct)]
        w_bwd = wo_ref[:, pl.ds(pl.multiple_of((2 * o + 1) * d + col0, ct), ct)]
        taps = _taps_from_hidden(hid, feats, w_fwd, w_bwd, delta_ref[...])
        _stage_rows(taps.astype(F32), rows_ref)
        o_ref[o] = _first_stage_dft(rows_ref, g_ref, stage_ref, r, nb)


def _fft_taps(hid, feats, w_out, deltas, g):
    n, fh = hid.shape
    n1 = n // FFT_N2
    d = deltas.shape[1]
    rows_out = g.shape[1]
    nb, ct = FFT_NB, FFT_CT
    return pl.pallas_call(
        functools.partial(_fft_taps_kernel, ct),
        grid=(FFT_N2 // nb, d // ct),
        in_specs=[pl.BlockSpec((n1, nb, fh), lambda i, c: (0, i, 0)),
                  pl.BlockSpec((n1, nb, LANES), lambda i, c: (0, i, 0)),
                  _const_spec(w_out.shape),
                  pl.BlockSpec((1, ct), lambda i, c: (0, c)),
                  pl.BlockSpec((nb, rows_out, n1), lambda i, c: (i, 0, 0))],
        out_specs=pl.BlockSpec((2, rows_out, nb, ct), lambda i, c: (0, 0, i, c)),
        out_shape=jax.ShapeDtypeStruct((2, rows_out, FFT_N2, d), BF16),
        scratch_shapes=[pltpu.VMEM((ct // LANES, n1 * nb, LANES), F32),
                        pltpu.VMEM((rows_out, nb, ct), F32)],
        compiler_params=_cparams(("arbitrary", "arbitrary")),
        name="fft_taps",
    )(hid.reshape(n1, FFT_N2, fh), feats.reshape(n1, FFT_N2, LANES), w_out, deltas, g)


def _fft_mid_kernel(t_ref, f_ref, g_ref, gi_ref, o_ref):
    half = FFT_N2
    for kk in range(FFT_KB):
        x = jnp.dot(g_ref[...], jnp.concatenate([t_ref[0, kk], t_ref[1, kk]], axis=0),
                    preferred_element_type=F32)
        h = jnp.dot(g_ref[...], jnp.concatenate([f_ref[0, kk], f_ref[1, kk]], axis=0),
                    preferred_element_type=F32)
        xr, xi = x[:half], x[half:]
        hr, hi = h[:half], h[half:]
        z = jnp.concatenate([xr * hr - xi * hi, xr * hi + xi * hr], axis=0).astype(BF16)
        y = jnp.dot(gi_ref[...], z, preferred_element_type=F32)
        o_ref[0, kk] = y[:half].astype(BF16)
        o_ref[1, kk] = y[half:].astype(BF16)


def _fft_mid(t, f, order, g_mid, g_mid_inv):
    rows, _, d = t.shape
    n1 = rows // 2
    blk = pl.BlockSpec((2, FFT_KB, FFT_N2, d), lambda i: (0, i, 0, 0))
    taps_blk = pl.BlockSpec((None, 2, FFT_KB, FFT_N2, d), lambda i: (order, 0, i, 0, 0))
    out = pl.pallas_call(
        _fft_mid_kernel,
        grid=(n1 // FFT_KB,),
        in_specs=[blk, taps_blk, _const_spec(g_mid.shape), _const_spec(g_mid_inv.shape)],
        out_specs=blk,
        out_shape=jax.ShapeDtypeStruct((2, n1, FFT_N2, d), BF16),
        compiler_params=_cparams(("arbitrary",)),
        name="fft_mid",
    )(t.reshape(2, n1, FFT_N2, d), f.reshape(2, 2, n1, FFT_N2, d), g_mid, g_mid_inv)
    return out.reshape(rows, FFT_N2, d)


def _fft_last_kernel(b_ref, g_ref, z_ref, gate_ref, bias_ref, o_ref, rows_ref, stage_ref):
    rows_in, nb, ct = b_ref.shape
    _stage_rows(b_ref[...].astype(F32).reshape(rows_in * nb, ct), rows_ref)
    for j in range(nb):
        bj = _column_group(rows_ref, j, rows_in, nb).astype(BF16)
        _scatter_column_group(stage_ref, j, jnp.dot(g_ref[j], bj, preferred_element_type=F32), nb)
    o_ref[...] = (gate_ref[...] * (_staged_block(stage_ref, nb) + z_ref[...] * bias_ref[...])).astype(o_ref.dtype)


def _fft_last(b, g, z3, z_col, gate3, gate_col, bias, out_dtype):
    rows_in, _, d = b.shape
    r = g.shape[1]
    nb, ct = FFT_NB, FFT_CT_LAST
    nc = d // ct
    return pl.pallas_call(
        _fft_last_kernel,
        grid=(FFT_N2 // nb, nc),
        in_specs=[pl.BlockSpec((rows_in, nb, ct), lambda i, c: (0, i, c)),
                  pl.BlockSpec((nb, r, rows_in), lambda i, c: (i, 0, 0)),
                  pl.BlockSpec((r, nb, ct), lambda i, c: (0, i, z_col * nc + c)),
                  pl.BlockSpec((r, nb, ct), lambda i, c: (0, i, gate_col * nc + c)),
                  pl.BlockSpec((1, 1, ct), lambda i, c: (0, 0, c))],
        out_specs=pl.BlockSpec((r, nb, ct), lambda i, c: (0, i, c)),
        out_shape=jax.ShapeDtypeStruct((r, FFT_N2, d), out_dtype),
        scratch_shapes=[pltpu.VMEM((ct // LANES, rows_in * nb, LANES), F32),
                        pltpu.VMEM((ct // LANES, r * nb, LANES), F32)],
        compiler_params=_cparams(("arbitrary", "arbitrary")),
        name="fft_last",
    )(b, g, z3, gate3, bias.reshape(1, 1, d))


def _hyena_long_convs(u, hid, feats, w_out, f_bias, seq_len, tables):
    n, d3 = u.shape
    d = d3 // 3
    g_first, g_last, g_taps, g_mid, g_mid_inv = tables
    n1 = 2 * seq_len // FFT_N2
    u3 = u.reshape(n1, FFT_N2, d3)
    f = _fft_taps(hid, feats, w_out, _decay_rates(d), g_taps)
    z3, z_col = u3, 0
    for o in range(2):
        a = _fft_first(z3, z_col, d, g_first)
        bq = _fft_mid(a, f, o, g_mid, g_mid_inv)
        z3 = _fft_last(bq, g_last, z3, z_col, u3, 1 + o, f_bias[o], F32 if o == 0 else BF16)
        z_col = 0
    return z3.reshape(n, d)


def _small_conv_kernel(ct, u_v_ref, u_x1_ref, u_x2_ref, h_ref, feat_ref, wo_ref, delta_ref,
                       gf_ref, gt_ref, gi_ref, bias_ref, o_ref):
    nf = gf_ref.shape[0] // 2
    d = wo_ref.shape[1] // 4
    col0 = pl.program_id(0) * ct
    z = u_v_ref[...]
    gates = (u_x1_ref, u_x2_ref)
    for o in range(2):
        w_fwd = wo_ref[:, pl.ds(pl.multiple_of(2 * o * d + col0, ct), ct)]
        w_bwd = wo_ref[:, pl.ds(pl.multiple_of((2 * o + 1) * d + col0, ct), ct)]
        taps = _taps_from_hidden(h_ref[...], feat_ref[...], w_fwd, w_bwd, delta_ref[...])
        spec = jnp.dot(gt_ref[...], taps, preferred_element_type=F32)
        zq = jnp.dot(gf_ref[...], z.astype(BF16), preferred_element_type=F32)
        zr, zi = zq[:nf], zq[nf:]
        hr, hi = spec[:nf], spec[nf:]
        prod = jnp.concatenate([zr * hr - zi * hi, zr * hi + zi * hr], axis=0).astype(BF16)
        y = jnp.dot(gi_ref[...], prod, preferred_element_type=F32)
        z = gates[o][...] * (y + z * bias_ref[o:o + 1, :])
    o_ref[...] = z


def _small_dft_tables(seq_len):
    n_fft = 2 * seq_len
    k = jnp.arange(n_fft, dtype=jnp.int32)
    ang = (2.0 * math.pi / n_fft) * ((k[:, None] * k[None, :]) % n_fft).astype(F32)
    c, s = jnp.cos(ang), jnp.sin(ang)
    cl, sl = c[:, :seq_len], s[:, :seq_len]
    g_fwd = jnp.concatenate([jnp.concatenate([cl, sl], axis=1),
                             jnp.concatenate([-sl, cl], axis=1)], axis=0)
    g_taps = jnp.concatenate([c, -s], axis=0)
    g_inv = g_fwd.T * (1.0 / n_fft)
    return g_fwd.astype(BF16), g_taps.astype(BF16), g_inv.astype(BF16)


def _hyena_small_convs(u, hid, feats, w_out, f_bias, seq_len):
    n, d3 = u.shape
    d = d3 // 3
    ct = 256
    nc = d // ct
    g_fwd, g_taps, g_inv = _small_dft_tables(seq_len)
    return pl.pallas_call(
        functools.partial(_small_conv_kernel, ct),
        grid=(nc,),
        in_specs=[pl.BlockSpec((n, ct), lambda c: (0, c)),
                  pl.BlockSpec((n, ct), lambda c: (0, nc + c)),
                  pl.BlockSpec((n, ct), lambda c: (0, 2 * nc + c)),
                  _const_spec(hid.shape), _const_spec(feats.shape), _const_spec(w_out.shape),
                  pl.BlockSpec((1, ct), lambda c: (0, c)),
                  _const_spec(g_fwd.shape), _const_spec(g_taps.shape), _const_spec(g_inv.shape),
                  pl.BlockSpec((2, ct), lambda c: (0, c))],
        out_specs=pl.BlockSpec((n, ct), lambda c: (0, c)),
        out_shape=jax.ShapeDtypeStruct((n, d), F32),
        compiler_params=_cparams(("arbitrary",)),
        name="small_conv",
    )(u, u, u, hid, feats, w_out, _decay_rates(d), g_fwd, g_taps, g_inv, f_bias)


def kernel(x, c, ctx, c_ctx, w_mod, b_mod, norm_w, ffn_w_gate_up, ffn_w_down, attn_w_qkv, attn_w_o,
           attn_q_norm, attn_k_norm, hy_w_in, hy_b_in, hy_conv_w, hy_conv_b, hy_f_w1, hy_f_b1,
           hy_f_w2, hy_f_b2, hy_f_w3, hy_f_b3, hy_f_wout, hy_f_freq, hy_f_bias, hy_w_out, hy_b_out):
    bsz, seq, d = x.shape
    ctx_len = ctx.shape[1]
    depth = w_mod.shape[0]
    assert bsz == 2, "the long convolution packs exactly two batches into one complex sequence"
    xs = x.reshape(bsz * seq, d)
    cs = ctx.reshape(bsz * ctx_len, d)

    cc = jnp.zeros((8, d), F32).at[:bsz].set(c).at[bsz].set(c_ctx)
    mod_all = _mod_vectors(cc, w_mod, b_mod).reshape(depth, 8, N_MOD, d)
    ctx_row = bsz

    c_rows = bsz * ctx_len
    p_sum, p_bcast = _head_sum_matrices()
    cos_t, sin_t = _rope_tables(seq)
    dft_tables = _dft_tables(seq)
    feats_x, feats_xt = _filter_features(seq)
    feats_c, feats_ct = _filter_features(ctx_len)
    wgu_all = ffn_w_gate_up.astype(BF16)
    wd_all = ffn_w_down.astype(BF16)
    zero_bias = jnp.zeros((d,), F32)

    for l in range(depth):
        mod = mod_all[l]
        is_attn = (l % 2) == 0
        ctx_out = l < depth - 1
        ctx_live = ctx_out or is_attn

        xs = _ffn(xs, mod, 0, norm_w[l, 0], wgu_all, wd_all, l, 0, seq, None)
        if ctx_live:
            cs = _ffn(cs, mod, 0, norm_w[l, 0], wgu_all, wd_all, l, 0, c_rows, ctx_row)

        if is_attn:
            a = l // 2
            w_qkv = attn_w_qkv[a].astype(BF16)
            w_o = attn_w_o[a].astype(BF16)
            qn = (jnp.tile(attn_q_norm[a], N_HEADS) * (HEAD_DIM ** -0.5 * math.log2(math.e))).reshape(1, -1)
            kn = jnp.tile(attn_k_norm[a], N_KV_HEADS).reshape(1, -1)
            q_l, kt_l, v_l = _attn_pre(xs, mod, norm_w[l, 1], w_qkv, qn, kn, p_sum, p_bcast,
                                       cos_t, sin_t, seq, None, True)
            q_c, kt_c, v_c = _attn_pre(cs, mod, norm_w[l, 1], w_qkv, qn, kn, p_sum, p_bcast,
                                       cos_t, sin_t, ctx_len, ctx_row, False)
            shift_ok = (HEAD_DIM * jnp.max(jnp.abs(qn)) * jnp.max(jnp.abs(kn))) < ATT_SHIFT_LIMIT
            o_l = lax.cond(shift_ok, functools.partial(_attention, False, seq),
                           functools.partial(_attention, True, seq), q_l, kt_c, v_c, kt_l, v_l)
            mix_x = (o_l, w_o, zero_bias)
            if ctx_out:
                o_c = lax.cond(shift_ok, functools.partial(_attention, False, ctx_len),
                               functools.partial(_attention, True, ctx_len), q_c, kt_c, v_c)
                mix_c = (o_c, w_o, zero_bias)
        else:
            j = l // 2
            w_in = hy_w_in[j].astype(BF16)
            w_out = hy_w_out[j].astype(BF16)
            fargs = (hy_f_w1[j], hy_f_b1[j], hy_f_w2[j], hy_f_b2[j], hy_f_w3[j], hy_f_b3[j], hy_f_freq[j])
            f_wout = hy_f_wout[j].astype(BF16)
            u_l = _hyena_pre(xs, mod, norm_w[l, 1], w_in, hy_b_in[j], hy_conv_w[j], hy_conv_b[j], seq, None)
            y_l = _hyena_long_convs(u_l, _filter_hidden(feats_xt, *fargs), feats_x, f_wout, hy_f_bias[j],
                                    seq, dft_tables)
            if ctx_out:
                u_c = _hyena_pre(cs, mod, norm_w[l, 1], w_in, hy_b_in[j], hy_conv_w[j], hy_conv_b[j],
                                 ctx_len, ctx_row)
                y_c = _hyena_small_convs(u_c, _filter_hidden(feats_ct, *fargs), feats_c, f_wout,
                                         hy_f_bias[j], ctx_len)
                mix_c = (y_c, w_out, hy_b_out[j])
            mix_x = (y_l, w_out, hy_b_out[j])

        xs = _ffn(xs, mod, 2, norm_w[l, 2], wgu_all, wd_all, l, 1, seq, None, mixer=mix_x)
        if ctx_out:
            cs = _ffn(cs, mod, 2, norm_w[l, 2], wgu_all, wd_all, l, 1, c_rows, ctx_row, mixer=mix_c)
    return xs.reshape(bsz, seq, d)
```

```python
import functools
import math

import jax
import jax.numpy as jnp
import numpy as np
from jax import lax
from jax.experimental import pallas as pl
from jax.experimental.pallas import tpu as pltpu

F32 = jnp.float32
BF16 = jnp.bfloat16

N_MOD = 9
N_HEADS = 16
N_KV_HEADS = 4
HEAD_DIM = 64
KV_GROUP = N_HEADS // N_KV_HEADS
KT_ROWS = 2 * HEAD_DIM
ATT_SHIFT_LIMIT = 60.0
GRID_W = 64
ROPE_THETA = 10000.0
EPS = 1e-6
FILTER_BANDS = 16
N_FILTER_FEAT = 1 + 2 * FILTER_BANDS
DECAY_TARGET = 1e-2
FAST_DECAY_PCT = 0.3
SLOW_DECAY_PCT = 1.5

LANES = 128
HALO = 16
FFT_N2 = 128
VMEM_LIMIT = 56 * 1024 * 1024

FFN_TM = 1024
PRE_TM = 256
FFN_FC = 256
ATT_TQ = 128
ATT_SUB = 4
ATT_TK = 512
FFT_NB = 16
FFT_KB = 8
FFT_CT = 512
FFT_CT_LAST = 256


def _cparams(sem):
    return pltpu.CompilerParams(dimension_semantics=sem, vmem_limit_bytes=VMEM_LIMIT)


def _const_spec(shape):
    nd = len(shape)
    return pl.BlockSpec(shape, lambda *_: (0,) * nd, pipeline_mode=pl.Buffered(1))


def _modulated(s, g, shift, scale):
    ms = jnp.mean(s * s, axis=-1, keepdims=True)
    return (s * lax.rsqrt(ms + EPS) * g) * (1.0 + scale) + shift


def _mod_kernel(c_ref, w_ref, b_ref, o_ref):
    c = c_ref[...]
    a = (c * jax.nn.sigmoid(c)).astype(BF16)
    o_ref[0] = jnp.dot(a, w_ref[0].astype(BF16), preferred_element_type=F32) + b_ref[0]


def _mod_vectors(cc, w_mod, b_mod):
    depth, d, n = w_mod.shape
    tn = n // 6
    return pl.pallas_call(
        _mod_kernel,
        grid=(depth, n // tn),
        in_specs=[pl.BlockSpec((8, d), lambda l, j: (0, 0)),
                  pl.BlockSpec((1, d, tn), lambda l, j: (l, 0, j)),
                  pl.BlockSpec((1, 1, tn), lambda l, j: (l, 0, j))],
        out_specs=pl.BlockSpec((1, 8, tn), lambda l, j: (l, 0, j)),
        out_shape=jax.ShapeDtypeStruct((depth, 8, n), F32),
        compiler_params=_cparams(("arbitrary", "arbitrary")),
        name="mod_vectors",
    )(cc, w_mod, b_mod.reshape(depth, 1, n))


def _mod_blockspec(d, tiles_per_row, fixed_row):
    if fixed_row is None:
        return pl.BlockSpec((None, N_MOD, d), lambda i: (i // tiles_per_row, 0, 0))
    return pl.BlockSpec((None, N_MOD, d), lambda i: (fixed_row, 0, 0))


def _ffn_kernel(k, with_mixer, s_ref, mod_ref, g_ref, wgu_ref, wd_ref, *rest):
    if with_mixer:
        a_ref, wp_ref, bp_ref, o_ref, h_ref, acc_ref = rest
        mixed = jnp.dot(a_ref[...].astype(BF16), wp_ref[...], preferred_element_type=F32) + bp_ref[...]
        s = s_ref[...] + mod_ref[5:6, :] * mixed
    else:
        o_ref, h_ref, acc_ref = rest
        s = s_ref[...]
    shift = mod_ref[3 * k:3 * k + 1, :]
    scale = mod_ref[3 * k + 1:3 * k + 2, :]
    gate = mod_ref[3 * k + 2:3 * k + 3, :]
    h_ref[...] = _modulated(s, g_ref[...], shift, scale).astype(BF16)
    f = wd_ref.shape[0]
    for c in range(f // FFN_FC):
        lo = c * FFN_FC
        g = jnp.dot(h_ref[...], wgu_ref[:, lo:lo + FFN_FC], preferred_element_type=F32)
        u = jnp.dot(h_ref[...], wgu_ref[:, f + lo:f + lo + FFN_FC], preferred_element_type=F32)
        a = (g * jax.nn.sigmoid(g) * u).astype(BF16)
        y = jnp.dot(a, wd_ref[lo:lo + FFN_FC, :], preferred_element_type=F32)
        if c == 0:
            acc_ref[...] = y
        else:
            acc_ref[...] += y
    o_ref[...] = s + 0.5 * gate * acc_ref[...]


def _ffn(s, mod, k, g, wgu_all, wd_all, layer, which, rows_per_cond, fixed_row, mixer=None):
    n, d = s.shape
    tm = min(FFN_TM, n)
    tiles_per_row = rows_per_cond // tm
    f = wd_all.shape[2]
    assert f % FFN_FC == 0

    def weight_spec(shape):
        return pl.BlockSpec((None, None) + shape, lambda i: (layer, which, 0, 0), pipeline_mode=pl.Buffered(1))

    in_specs = [pl.BlockSpec((tm, d), lambda i: (i, 0)),
                _mod_blockspec(d, tiles_per_row, fixed_row),
                _const_spec((1, d)),
                weight_spec((d, 2 * f)),
                weight_spec((f, d))]
    args = [s, mod, g.reshape(1, d), wgu_all, wd_all]
    if mixer is not None:
        a, w, b = mixer
        in_specs += [pl.BlockSpec((tm, a.shape[1]), lambda i: (i, 0)), _const_spec(w.shape), _const_spec((1, d))]
        args += [a, w, b.reshape(1, d)]
    return pl.pallas_call(
        functools.partial(_ffn_kernel, k, mixer is not None),
        grid=(n // tm,),
        in_specs=in_specs,
        out_specs=pl.BlockSpec((tm, d), lambda i: (i, 0)),
        out_shape=jax.ShapeDtypeStruct((n, d), F32),
        scratch_shapes=[pltpu.VMEM((tm, d), BF16), pltpu.VMEM((tm, d), F32)],
        compiler_params=_cparams(("arbitrary",)),
        name="ffn",
    )(*args)


def _head_norm(x, p, pt, w):
    sq = x * x
    hi = sq.astype(BF16)
    lo = (sq - hi.astype(F32)).astype(BF16)
    ss = jnp.dot(hi, p, preferred_element_type=F32) + jnp.dot(lo, p, preferred_element_type=F32)
    r = lax.rsqrt(ss * (1.0 / HEAD_DIM) + EPS)
    rh = r.astype(BF16)
    rl = (r - rh.astype(F32)).astype(BF16)
    rb = jnp.dot(rh, pt, preferred_element_type=F32) + jnp.dot(rl, pt, preferred_element_type=F32)
    return x * rb * w


def _rope_block(xb, cos, sin):
    lane = lax.broadcasted_iota(jnp.int32, xb.shape, 1)
    fwd = pltpu.roll(xb, 16, axis=1)
    bwd = pltpu.roll(xb, LANES - 16, axis=1)
    partner = jnp.where((lane % 32) < 16, bwd, fwd)
    return xb * cos + partner * sin


def _attn_pre_kernel(rope, s_ref, mod_ref, g_ref, w_ref, qn_ref, kn_ref, p_ref, pt_ref,
                     cos_ref, sin_ref, q_ref, kt_ref, v_ref):
    s = s_ref[...]
    tm = s.shape[0]
    nq = N_HEADS * HEAD_DIM
    nk = N_KV_HEADS * HEAD_DIM
    h = _modulated(s, g_ref[...], mod_ref[3:4, :], mod_ref[4:5, :]).astype(BF16)
    qkv = jnp.dot(h, w_ref[...], preferred_element_type=F32)
    q = _head_norm(qkv[:, :nq], p_ref[...], pt_ref[...], qn_ref[...])
    k = _head_norm(qkv[:, nq:nq + nk], p_ref[:nk, :], pt_ref[:, :nk], kn_ref[...])
    v = qkv[:, nq + nk:]
    cos = cos_ref[...]
    sin = sin_ref[...]
    for j in range(nq // LANES):
        qb = q[:, j * LANES:(j + 1) * LANES]
        if rope:
            qb = _rope_block(qb, cos, sin)
        q_ref[:, j * LANES:(j + 1) * LANES] = qb.astype(BF16)
    kblocks = []
    for j in range(nk // LANES):
        kb = k[:, j * LANES:(j + 1) * LANES]
        if rope:
            kb = _rope_block(kb, cos, sin)
        kblocks.append(kb)
    kt = jnp.concatenate(kblocks, axis=1).T.astype(BF16)
    tail_row = lax.broadcasted_iota(jnp.int32, (KT_ROWS - HEAD_DIM, tm), 0)
    tail = jnp.where(tail_row == 0, 1.0, 0.0).astype(BF16)
    for hh in range(N_KV_HEADS):
        kt_ref[hh] = jnp.concatenate([kt[hh * HEAD_DIM:(hh + 1) * HEAD_DIM, :], tail], axis=0)
    lane = lax.broadcasted_iota(jnp.int32, (tm, LANES), 1)
    ones_col = jnp.where(lane == HEAD_DIM, 1.0, 0.0)
    for hh in range(N_KV_HEADS):
        vb = v[:, (hh // 2) * LANES:(hh // 2 + 1) * LANES]
        if hh % 2 == 1:
            vb = pltpu.roll(vb, HEAD_DIM, axis=1)
        v_ref[hh] = jnp.where(lane < HEAD_DIM, vb, ones_col).astype(BF16)


def _attn_pre(s, mod, g, w_qkv, qn, kn, p, pt, cos, sin, seq_len, fixed_row, rope):
    n, d = s.shape
    tm = min(PRE_TM, seq_len)
    tps = seq_len // tm
    n_seq = n // seq_len
    nq = N_HEADS * HEAD_DIM
    return pl.pallas_call(
        functools.partial(_attn_pre_kernel, rope),
        grid=(n // tm,),
        in_specs=[pl.BlockSpec((tm, d), lambda i: (i, 0)),
                  _mod_blockspec(d, tps, fixed_row),
                  _const_spec((1, d)),
                  _const_spec(w_qkv.shape),
                  _const_spec((1, nq)),
                  _const_spec((1, N_KV_HEADS * HEAD_DIM)),
                  _const_spec(p.shape),
                  _const_spec(pt.shape),
                  pl.BlockSpec((tm, LANES), lambda i: (i % tps, 0)),
                  pl.BlockSpec((tm, LANES), lambda i: (i % tps, 0))],
        out_specs=[pl.BlockSpec((tm, nq), lambda i: (i, 0)),
                   pl.BlockSpec((None, N_KV_HEADS, KT_ROWS, tm), lambda i: (i // tps, 0, 0, i % tps)),
                   pl.BlockSpec((None, N_KV_HEADS, tm, LANES), lambda i: (i // tps, 0, i % tps, 0))],
        out_shape=[jax.ShapeDtypeStruct((n, nq), BF16),
                   jax.ShapeDtypeStruct((n_seq, N_KV_HEADS, KT_ROWS, seq_len), BF16),
                   jax.ShapeDtypeStruct((n_seq, N_KV_HEADS, seq_len, LANES), BF16)],
        compiler_params=_cparams(("arbitrary",)),
        name="attn_pre",
    )(s, mod, g.reshape(1, d), w_qkv, qn, kn, p, pt, cos, sin)


def _rope_tables(seq_len):
    half = HEAD_DIM // 4
    t = np.arange(seq_len)
    pos = np.stack([t // GRID_W, t % GRID_W], axis=1).astype(np.float32)
    freqs = (ROPE_THETA ** (-np.arange(half, dtype=np.float32) / half)).astype(np.float32)
    lane = np.arange(LANES)
    axis = (lane % HEAD_DIM) // (HEAD_DIM // 2)
    e = lane % (HEAD_DIM // 2)
    ang = pos[:, axis] * freqs[e % half][None, :]
    sign = np.where(e < half, -1.0, 1.0).astype(np.float32)
    return jnp.asarray(np.cos(ang), F32), jnp.asarray(np.sin(ang) * sign[None, :], F32)


def _head_sum_matrices():
    lane = np.arange(N_HEADS * HEAD_DIM)
    p = (lane[:, None] // HEAD_DIM == np.arange(LANES)[None, :]).astype(np.float32)
    return jnp.asarray(p, BF16), jnp.asarray(p.T, BF16)


def _max_key_norm(kt_ref):
    k = kt_ref[:HEAD_DIM, :].astype(F32)
    return jnp.sqrt(jnp.max(jnp.sum(k * k, axis=0, keepdims=True), axis=1, keepdims=True))


def _attn_kernel(online, n_lat_chunks, tk, q_ref, ktc_ref, vc_ref, *rest):
    if n_lat_chunks:
        ktl_ref, vl_ref, o_ref, kmax_ref = rest
    else:
        o_ref, kmax_ref = rest
    if not online:
        @pl.when(pl.program_id(2) == 0)
        def _():
            kmax = _max_key_norm(ktc_ref)
            if n_lat_chunks:
                kmax = jnp.maximum(kmax, _max_key_norm(ktl_ref))
            kmax_ref[...] = kmax
    tq = min(ATT_TQ, q_ref.shape[0])
    n_sub = q_ref.shape[0] // tq
    lane = lax.broadcasted_iota(jnp.int32, (tq, LANES), 1)

    def stacked_heads(qf):
        parts = []
        for g in range(KV_GROUP):
            blk = qf[:, (g // 2) * LANES:(g // 2 + 1) * LANES]
            if g % 2 == 1:
                blk = pltpu.roll(blk, HEAD_DIM, axis=1)
            qg = jnp.where(lane < HEAD_DIM, blk, 0.0)
            if not online:
                shift = jnp.sqrt(jnp.sum(qg * qg, axis=1, keepdims=True)) * kmax_ref[...]
                qg = jnp.where(lane == HEAD_DIM, -shift, qg)
            parts.append(qg.astype(BF16))
        return jnp.concatenate(parts, axis=0)

    q4s = [stacked_heads(q_ref[sub * tq:(sub + 1) * tq, :].astype(F32)) for sub in range(n_sub)]

    for sub in range(n_sub):
        q4 = q4s[sub]
        if online:
            def step(kt, v, carry, q4=q4):
                m, acc = carry
                s = jnp.dot(q4, kt, preferred_element_type=F32)
                m_new = jnp.maximum(m, jnp.max(s, axis=-1, keepdims=True))
                p = jnp.exp2(s - m_new)
                acc = jnp.exp2(m - m_new) * acc + jnp.dot(p.astype(BF16), v, preferred_element_type=F32)
                return m_new, acc
            carry = (jnp.full((KV_GROUP * tq, 1), -1e30, F32), jnp.zeros((KV_GROUP * tq, LANES), F32))
        else:
            def step(kt, v, acc, q4=q4):
                p = jnp.exp2(jnp.dot(q4, kt, preferred_element_type=F32))
                return acc + jnp.dot(p.astype(BF16), v, preferred_element_type=F32)
            carry = jnp.zeros((KV_GROUP * tq, LANES), F32)

        carry = step(ktc_ref[...], vc_ref[...], carry)
        if n_lat_chunks:
            def body(c, carry, step=step):
                off = pl.multiple_of(c * tk, tk)
                return step(ktl_ref[:, pl.ds(off, tk)], vl_ref[pl.ds(off, tk), :], carry)
            carry = lax.fori_loop(0, n_lat_chunks, body, carry, unroll=8 if online else True)
        acc = carry[1] if online else carry
        o = acc[:, :HEAD_DIM] / acc[:, HEAD_DIM:HEAD_DIM + 1]
        o_ref[sub * tq:(sub + 1) * tq, :] = jnp.concatenate(
            [o[g * tq:(g + 1) * tq, :] for g in range(KV_GROUP)], axis=1).astype(BF16)


def _attention(online, lq, q, ktc, vc, ktl=None, vl=None):
    n, nq = q.shape
    b = n // lq
    tq = min(lq, ATT_TQ * ATT_SUB)
    nqt = lq // tq
    gw = KV_GROUP * HEAD_DIM
    lc = ktc.shape[-1]
    in_specs = [pl.BlockSpec((tq, gw), lambda bb, j, i: (bb * nqt + i, j)),
                pl.BlockSpec((None, None, KT_ROWS, lc), lambda bb, j, i: (bb, j, 0, 0)),
                pl.BlockSpec((None, None, lc, LANES), lambda bb, j, i: (bb, j, 0, 0))]
    args = [q, ktc, vc]
    n_lat = 0
    if ktl is not None:
        ll = ktl.shape[-1]
        n_lat = ll // ATT_TK
        in_specs += [pl.BlockSpec((None, None, KT_ROWS, ll), lambda bb, j, i: (bb, j, 0, 0)),
                     pl.BlockSpec((None, None, ll, LANES), lambda bb, j, i: (bb, j, 0, 0))]
        args += [ktl, vl]
    return pl.pallas_call(
        functools.partial(_attn_kernel, online, n_lat, ATT_TK),
        grid=(b, N_KV_HEADS, nqt),
        in_specs=in_specs,
        out_specs=pl.BlockSpec((tq, gw), lambda bb, j, i: (bb * nqt + i, j)),
        out_shape=jax.ShapeDtypeStruct((n, nq), BF16),
        scratch_shapes=[pltpu.VMEM((1, 1), F32)],
        compiler_params=_cparams(("arbitrary", "arbitrary", "arbitrary")),
        name="attention",
    )(*args)


def _hyena_pre_kernel(tps, n_col, s_ref, prev_ref, next_ref, mod_ref, g_ref, w_ref, b_ref,
                      cw_ref, cb_ref, o_ref, h_ref, pre_ref):
    i = pl.program_id(0)
    tm = s_ref.shape[0]
    hb = HALO
    g = g_ref[...]
    shift = mod_ref[3:4, :]
    scale = mod_ref[4:5, :]
    h_ref[0:hb, :] = _modulated(prev_ref[...], g, shift, scale).astype(BF16)
    h_ref[hb:hb + tm, :] = _modulated(s_ref[...], g, shift, scale).astype(BF16)
    h_ref[hb + tm:, :] = _modulated(next_ref[...], g, shift, scale).astype(BF16)
    row = lax.broadcasted_iota(jnp.int32, (tm, 1), 0)
    drop_up = jnp.logical_and(row == 0, i % tps == 0)
    drop_dn = jnp.logical_and(row == tm - 1, i % tps == tps - 1)
    ct = w_ref.shape[1] // n_col
    for c in range(n_col):
        cols = slice(c * ct, (c + 1) * ct)
        pre_ref[...] = jnp.dot(h_ref[...], w_ref[:, cols], preferred_element_type=F32) + b_ref[:, cols]
        up = jnp.where(drop_up, 0.0, pre_ref[hb - 1:hb - 1 + tm, :])
        mid = pre_ref[hb:hb + tm, :]
        dn = jnp.where(drop_dn, 0.0, pre_ref[hb + 1:hb + 1 + tm, :])
        o_ref[:, cols] = (up * cw_ref[0:1, cols] + mid * cw_ref[1:2, cols]
                          + dn * cw_ref[2:3, cols] + cb_ref[:, cols])


def _hyena_pre(s, mod, g, w_in, b_in, conv_w, conv_b, seq_len, fixed_row):
    n, d = s.shape
    tm = min(PRE_TM, seq_len)
    tps = seq_len // tm
    n3 = w_in.shape[1]
    hb = HALO
    nblk = n // hb
    n_col = 6
    return pl.pallas_call(
        functools.partial(_hyena_pre_kernel, tps, n_col),
        grid=(n // tm,),
        in_specs=[pl.BlockSpec((tm, d), lambda i: (i, 0)),
                  pl.BlockSpec((hb, d), lambda i: (jnp.maximum(i * (tm // hb) - 1, 0), 0)),
                  pl.BlockSpec((hb, d), lambda i: (jnp.minimum((i + 1) * (tm // hb), nblk - 1), 0)),
                  _mod_blockspec(d, tps, fixed_row),
                  _const_spec((1, d)),
                  _const_spec(w_in.shape),
                  _const_spec((1, n3)),
                  _const_spec((3, n3)),
                  _const_spec((1, n3))],
        out_specs=pl.BlockSpec((tm, n3), lambda i: (i, 0)),
        out_shape=jax.ShapeDtypeStruct((n, n3), F32),
        scratch_shapes=[pltpu.VMEM((tm + 2 * hb, d), BF16),
                        pltpu.VMEM((tm + 2 * hb, n3 // n_col), F32)],
        compiler_params=_cparams(("arbitrary",)),
        name="hyena_pre",
    )(s, s, s, mod, g.reshape(1, d), w_in, b_in.reshape(1, n3), conv_w, conv_b.reshape(1, n3))


def _filter_kernel(feat_ref, w1_ref, b1_ref, w2_ref, b2_ref, w3_ref, b3_ref, a_ref, o_ref):
    hp = lax.Precision.HIGHEST
    a = a_ref[...]
    hid = jnp.sin(a * (jnp.dot(w1_ref[...], feat_ref[...], precision=hp, preferred_element_type=F32) + b1_ref[...]))
    hid = jnp.sin(a * (jnp.dot(w2_ref[...], hid, precision=hp, preferred_element_type=F32) + b2_ref[...]))
    hid = jnp.sin(a * (jnp.dot(w3_ref[...], hid, precision=hp, preferred_element_type=F32) + b3_ref[...]))
    o_ref[...] = hid.T


def _tap_scales(feats, delta):
    decay = jnp.exp(-feats[:, 0:1] * delta)
    return (decay * feats[:, N_FILTER_FEAT:N_FILTER_FEAT + 1],
            decay * feats[:, N_FILTER_FEAT + 1:N_FILTER_FEAT + 2])


def _taps_from_hidden(hb, scales, w_fwd, w_bwd):
    fwd = jnp.dot(hb, w_fwd, preferred_element_type=F32)
    bwd = jnp.dot(hb, w_bwd, preferred_element_type=F32)
    return (scales[0] * fwd + scales[1] * bwd).astype(BF16)


def _filter_features(seq_len):
    l = seq_len
    n_feat = N_FILTER_FEAT
    n = np.arange(2 * l)
    pos = np.where(n < l, n, 2 * l - n)
    pos = np.where(n == l, 0, pos)
    t = np.linspace(0.0, 1.0, l, dtype=np.float32)[pos]
    w = (2.0 * math.pi * pos.astype(np.float32) / l).astype(np.float32)
    bands = np.linspace(1e-4, FILTER_BANDS - 1, FILTER_BANDS, dtype=np.float32)
    bw = (bands[None, :] * w[:, None]).astype(np.float32)
    feats = np.zeros((2 * l, LANES), np.float32)
    feats[:, 0] = t
    feats[:, 1:1 + FILTER_BANDS] = np.cos(bw)
    feats[:, 1 + FILTER_BANDS:n_feat] = -np.sin(bw)
    feats[:, n_feat] = (n < l)
    feats[:, n_feat + 1] = np.logical_or(n > l, n == 0)
    return jnp.asarray(feats, F32), jnp.asarray(np.ascontiguousarray(feats.T), F32)


def _filter_hidden(feats_t, f_w1, f_b1, f_w2, f_b2, f_w3, f_b3, f_freq):
    n = feats_t.shape[1]
    fh = f_w1.shape[1]
    w1t = jnp.zeros((fh, LANES), F32).at[:, :N_FILTER_FEAT].set(f_w1.T)
    tm = min(512, n)
    return pl.pallas_call(
        _filter_kernel,
        grid=(n // tm,),
        in_specs=[pl.BlockSpec((LANES, tm), lambda i: (0, i)),
                  _const_spec((fh, LANES)), _const_spec((fh, 1)),
                  _const_spec((fh, fh)), _const_spec((fh, 1)),
                  _const_spec((fh, fh)), _const_spec((fh, 1)),
                  _const_spec((fh, 1))],
        out_specs=pl.BlockSpec((tm, fh), lambda i: (i, 0)),
        out_shape=jax.ShapeDtypeStruct((n, fh), F32),
        compiler_params=_cparams(("arbitrary",)),
        name="filter_hidden",
    )(feats_t, w1t, f_b1.reshape(fh, 1), f_w2.T, f_b2.reshape(fh, 1), f_w3.T, f_b3.reshape(fh, 1),
      f_freq.reshape(fh, 1))


def _decay_rates(d):
    min_decay = math.log(DECAY_TARGET) / SLOW_DECAY_PCT
    max_decay = math.log(DECAY_TARGET) / FAST_DECAY_PCT
    return jnp.abs(jnp.linspace(min_decay, max_decay, d, dtype=F32)).reshape(1, d)


def _dft_tables(seq_len):
    n_fft = 2 * seq_len
    n1 = n_fft // FFT_N2
    k1 = jnp.arange(n1, dtype=jnp.int32)[None, :, None]
    n2 = jnp.arange(FFT_N2, dtype=jnp.int32)[:, None, None]
    nn1 = jnp.arange(n1, dtype=jnp.int32)[None, None, :]
    ang = (2.0 * math.pi / n_fft) * ((k1 * (FFT_N2 * nn1 + n2)) % n_fft).astype(F32)
    c = jnp.cos(ang)
    s = jnp.sin(ang)
    h = n1 // 2
    g_first = jnp.concatenate([jnp.concatenate([c[:, :, :h], s[:, :, :h]], axis=2),
                               jnp.concatenate([-s[:, :, :h], c[:, :, :h]], axis=2)], axis=1)
    g_last = jnp.swapaxes(g_first, 1, 2) * (1.0 / n_fft)
    g_taps = jnp.concatenate([c, -s], axis=1)
    k2 = jnp.arange(FFT_N2, dtype=jnp.int32)
    ang2 = (2.0 * math.pi / FFT_N2) * ((k2[:, None] * k2[None, :]) % FFT_N2).astype(F32)
    c2, s2 = jnp.cos(ang2), jnp.sin(ang2)
    g_mid = jnp.concatenate([jnp.concatenate([c2, s2], axis=1),
                             jnp.concatenate([-s2, c2], axis=1)], axis=0)
    return (g_first.astype(BF16), g_last.astype(BF16), g_taps.astype(BF16),
            g_mid.astype(BF16), g_mid.T.astype(BF16))


def _stage_rows(x, rows_ref):
    for s in range(rows_ref.shape[0]):
        rows_ref[s] = x[:, s * LANES:(s + 1) * LANES]


def _column_group(rows_ref, j, r, nb):
    return jnp.concatenate([rows_ref[s, pl.ds(j, r, stride=nb), :] for s in range(rows_ref.shape[0])], axis=1)


def _scatter_column_group(stage_ref, j, y, nb):
    for s in range(stage_ref.shape[0]):
        stage_ref[s, pl.ds(j, y.shape[0], stride=nb), :] = y[:, s * LANES:(s + 1) * LANES]


def _staged_block(stage_ref, nb):
    n_slab, rows, _ = stage_ref.shape
    return jnp.concatenate([stage_ref[s].reshape(rows // nb, nb, LANES) for s in range(n_slab)], axis=2)


def _first_stage_dft(rows_ref, g_ref, stage_ref, r, nb):
    for j in range(nb):
        xj = _column_group(rows_ref, j, r, nb).astype(BF16)
        stage_ref[:, j, :] = jnp.dot(g_ref[j], xj, preferred_element_type=F32)
    return stage_ref[...].astype(BF16)


def _fft_first_kernel(x_ref, g_ref, o_ref, rows_ref, stage_ref):
    r, nb, ct = x_ref.shape
    _stage_rows(x_ref[...].reshape(r * nb, ct), rows_ref)
    o_ref[...] = _first_stage_dft(rows_ref, g_ref, stage_ref, r, nb)


def _fft_first(x3, col_block, d, g):
    r = x3.shape[0]
    rows_out = g.shape[1]
    nb, ct = FFT_NB, FFT_CT
    return pl.pallas_call(
        _fft_first_kernel,
        grid=(FFT_N2 // nb, d // ct),
        in_specs=[pl.BlockSpec((r, nb, ct), lambda i, c: (0, i, col_block * (d // ct) + c)),
                  pl.BlockSpec((nb, rows_out, r), lambda i, c: (i, 0, 0))],
        out_specs=pl.BlockSpec((rows_out, nb, ct), lambda i, c: (0, i, c)),
        out_shape=jax.ShapeDtypeStruct((rows_out, FFT_N2, d), BF16),
        scratch_shapes=[pltpu.VMEM((ct // LANES, r * nb, LANES), F32),
                        pltpu.VMEM((rows_out, nb, ct), F32)],
        compiler_params=_cparams(("arbitrary", "arbitrary")),
        name="fft_first",
    )(x3, g)


def _fft_taps_kernel(ct, h_ref, feat_ref, wo_ref, delta_ref, g_ref, o_ref, rows_ref, stage_ref):
    r, nb, fh = h_ref.shape
    d = wo_ref.shape[1] // 4
    col0 = pl.program_id(1) * ct
    hb = h_ref[...].reshape(r * nb, fh).astype(BF16)
    scales = _tap_scales(feat_ref[...].reshape(r * nb, LANES), delta_ref[...])
    for o in range(2):
        w_fwd = wo_ref[:, pl.ds(pl.multiple_of(2 * o * d + col0, ct), ct)]
        w_bwd = wo_ref[:, pl.ds(pl.multiple_of((2 * o + 1) * d + col0, ct), ct)]
        taps = _taps_from_hidden(hb, scales, w_fwd, w_bwd)
        _stage_rows(taps.astype(F32), rows_ref)
        o_ref[o] = _first_stage_dft(rows_ref, g_ref, stage_ref, r, nb)


def _fft_taps(hid, feats, w_out, deltas, g):
    n, fh = hid.shape
    n1 = n // FFT_N2
    d = deltas.shape[1]
    rows_out = g.shape[1]
    nb, ct = FFT_NB, FFT_CT
    return pl.pallas_call(
        functools.partial(_fft_taps_kernel, ct),
        grid=(FFT_N2 // nb, d // ct),
        in_specs=[pl.BlockSpec((n1, nb, fh), lambda i, c: (0, i, 0)),
                  pl.BlockSpec((n1, nb, LANES), lambda i, c: (0, i, 0)),
                  _const_spec(w_out.shape),
                  pl.BlockSpec((1, ct), lambda i, c: (0, c)),
                  pl.BlockSpec((nb, rows_out, n1), lambda i, c: (i, 0, 0))],
        out_specs=pl.BlockSpec((2, rows_out, nb, ct), lambda i, c: (0, 0, i, c)),
        out_shape=jax.ShapeDtypeStruct((2, rows_out, FFT_N2, d), BF16),
        scratch_shapes=[pltpu.VMEM((ct // LANES, n1 * nb, LANES), F32),
                        pltpu.VMEM((rows_out, nb, ct), F32)],
        compiler_params=_cparams(("arbitrary", "arbitrary")),
        name="fft_taps",
    )(hid.reshape(n1, FFT_N2, fh), feats.reshape(n1, FFT_N2, LANES), w_out, deltas, g)


def _fft_mid_kernel(t_ref, f_ref, g_ref, gi_ref, o_ref):
    half = FFT_N2
    for kk in range(FFT_KB):
        x = jnp.dot(g_ref[...], jnp.concatenate([t_ref[0, kk], t_ref[1, kk]], axis=0),
                    preferred_element_type=F32)
        h = jnp.dot(g_ref[...], jnp.concatenate([f_ref[0, kk], f_ref[1, kk]], axis=0),
                    preferred_element_type=F32)
        xr, xi = x[:half], x[half:]
        hr, hi = h[:half], h[half:]
        z = jnp.concatenate([xr * hr - xi * hi, xr * hi + xi * hr], axis=0).astype(BF16)
        y = jnp.dot(gi_ref[...], z, preferred_element_type=F32)
        o_ref[0, kk] = y[:half].astype(BF16)
        o_ref[1, kk] = y[half:].astype(BF16)


def _fft_mid(t, f, order, g_mid, g_mid_inv):
    rows, _, d = t.shape
    n1 = rows // 2
    blk = pl.BlockSpec((2, FFT_KB, FFT_N2, d), lambda i: (0, i, 0, 0))
    taps_blk = pl.BlockSpec((None, 2, FFT_KB, FFT_N2, d), lambda i: (order, 0, i, 0, 0))
    out = pl.pallas_call(
        _fft_mid_kernel,
        grid=(n1 // FFT_KB,),
        in_specs=[blk, taps_blk, _const_spec(g_mid.shape), _const_spec(g_mid_inv.shape)],
        out_specs=blk,
        out_shape=jax.ShapeDtypeStruct((2, n1, FFT_N2, d), BF16),
        compiler_params=_cparams(("arbitrary",)),
        name="fft_mid",
    )(t.reshape(2, n1, FFT_N2, d), f.reshape(2, 2, n1, FFT_N2, d), g_mid, g_mid_inv)
    return out.reshape(rows, FFT_N2, d)


def _fft_last_kernel(b_ref, g_ref, z_ref, gate_ref, bias_ref, o_ref, rows_ref, stage_ref):
    rows_in, nb, ct = b_ref.shape
    _stage_rows(b_ref[...].astype(F32).reshape(rows_in * nb, ct), rows_ref)
    for j in range(nb):
        bj = _column_group(rows_ref, j, rows_in, nb).astype(BF16)
        _scatter_column_group(stage_ref, j, jnp.dot(g_ref[j], bj, preferred_element_type=F32), nb)
    o_ref[...] = (gate_ref[...] * (_staged_block(stage_ref, nb) + z_ref[...] * bias_ref[...])).astype(o_ref.dtype)


def _fft_last(b, g, z3, z_col, gate3, gate_col, bias, out_dtype):
    rows_in, _, d = b.shape
    r = g.shape[1]
    nb, ct = FFT_NB, FFT_CT_LAST
    nc = d // ct
    return pl.pallas_call(
        _fft_last_kernel,
        grid=(FFT_N2 // nb, nc),
        in_specs=[pl.BlockSpec((rows_in, nb, ct), lambda i, c: (0, i, c)),
                  pl.BlockSpec((nb, r, rows_in), lambda i, c: (i, 0, 0)),
                  pl.BlockSpec((r, nb, ct), lambda i, c: (0, i, z_col * nc + c)),
                  pl.BlockSpec((r, nb, ct), lambda i, c: (0, i, gate_col * nc + c)),
                  pl.BlockSpec((1, 1, ct), lambda i, c: (0, 0, c))],
        out_specs=pl.BlockSpec((r, nb, ct), lambda i, c: (0, i, c)),
        out_shape=jax.ShapeDtypeStruct((r, FFT_N2, d), out_dtype),
        scratch_shapes=[pltpu.VMEM((ct // LANES, rows_in * nb, LANES), F32),
                        pltpu.VMEM((ct // LANES, r * nb, LANES), F32)],
        compiler_params=_cparams(("arbitrary", "arbitrary")),
        name="fft_last",
    )(b, g, z3, gate3, bias.reshape(1, 1, d))


def _hyena_long_convs(u, hid, feats, w_out, f_bias, seq_len, tables):
    n, d3 = u.shape
    d = d3 // 3
    g_first, g_last, g_taps, g_mid, g_mid_inv = tables
    n1 = 2 * seq_len // FFT_N2
    u3 = u.reshape(n1, FFT_N2, d3)
    f = _fft_taps(hid, feats, w_out, _decay_rates(d), g_taps)
    z3, z_col = u3, 0
    for o in range(2):
        a = _fft_first(z3, z_col, d, g_first)
        bq = _fft_mid(a, f, o, g_mid, g_mid_inv)
        z3 = _fft_last(bq, g_last, z3, z_col, u3, 1 + o, f_bias[o], F32 if o == 0 else BF16)
        z_col = 0
    return z3.reshape(n, d)


def _small_conv_kernel(ct, u_v_ref, u_x1_ref, u_x2_ref, h_ref, feat_ref, wo_ref, delta_ref,
                       gf_ref, gt_ref, gi_ref, bias_ref, o_ref):
    nf = gf_ref.shape[0] // 2
    d = wo_ref.shape[1] // 4
    col0 = pl.program_id(0) * ct
    z = u_v_ref[...]
    gates = (u_x1_ref, u_x2_ref)
    hb = h_ref[...].astype(BF16)
    scales = _tap_scales(feat_ref[...], delta_ref[...])
    for o in range(2):
        w_fwd = wo_ref[:, pl.ds(pl.multiple_of(2 * o * d + col0, ct), ct)]
        w_bwd = wo_ref[:, pl.ds(pl.multiple_of((2 * o + 1) * d + col0, ct), ct)]
        taps = _taps_from_hidden(hb, scales, w_fwd, w_bwd)
        spec = jnp.dot(gt_ref[...], taps, preferred_element_type=F32)
        zq = jnp.dot(gf_ref[...], z.astype(BF16), preferred_element_type=F32)
        zr, zi = zq[:nf], zq[nf:]
        hr, hi = spec[:nf], spec[nf:]
        prod = jnp.concatenate([zr * hr - zi * hi, zr * hi + zi * hr], axis=0).astype(BF16)
        y = jnp.dot(gi_ref[...], prod, preferred_element_type=F32)
        z = gates[o][...] * (y + z * bias_ref[o:o + 1, :])
    o_ref[...] = z


def _small_dft_tables(seq_len):
    n_fft = 2 * seq_len
    k = jnp.arange(n_fft, dtype=jnp.int32)
    ang = (2.0 * math.pi / n_fft) * ((k[:, None] * k[None, :]) % n_fft).astype(F32)
    c, s = jnp.cos(ang), jnp.sin(ang)
    cl, sl = c[:, :seq_len], s[:, :seq_len]
    g_fwd = jnp.concatenate([jnp.concatenate([cl, sl], axis=1),
                             jnp.concatenate([-sl, cl], axis=1)], axis=0)
    g_taps = jnp.concatenate([c, -s], axis=0)
    g_inv = g_fwd.T * (1.0 / n_fft)
    return g_fwd.astype(BF16), g_taps.astype(BF16), g_inv.astype(BF16)


def _hyena_small_convs(u, hid, feats, w_out, f_bias, seq_len):
    n, d3 = u.shape
    d = d3 // 3
    ct = 256
    nc = d // ct
    g_fwd, g_taps, g_inv = _small_dft_tables(seq_len)
    return pl.pallas_call(
        functools.partial(_small_conv_kernel, ct),
        grid=(nc,),
        in_specs=[pl.BlockSpec((n, ct), lambda c: (0, c)),
                  pl.BlockSpec((n, ct), lambda c: (0, nc + c)),
                  pl.BlockSpec((n, ct), lambda c: (0, 2 * nc + c)),
                  _const_spec(hid.shape), _const_spec(feats.shape), _const_spec(w_out.shape),
                  pl.BlockSpec((1, ct), lambda c: (0, c)),
                  _const_spec(g_fwd.shape), _const_spec(g_taps.shape), _const_spec(g_inv.shape),
                  pl.BlockSpec((2, ct), lambda c: (0, c))],
        out_specs=pl.BlockSpec((n, ct), lambda c: (0, c)),
        out_shape=jax.ShapeDtypeStruct((n, d), F32),
        compiler_params=_cparams(("arbitrary",)),
        name="small_conv",
    )(u, u, u, hid, feats, w_out, _decay_rates(d), g_fwd, g_taps, g_inv, f_bias)


def kernel(x, c, ctx, c_ctx, w_mod, b_mod, norm_w, ffn_w_gate_up, ffn_w_down, attn_w_qkv, attn_w_o,
           attn_q_norm, attn_k_norm, hy_w_in, hy_b_in, hy_conv_w, hy_conv_b, hy_f_w1, hy_f_b1,
           hy_f_w2, hy_f_b2, hy_f_w3, hy_f_b3, hy_f_wout, hy_f_freq, hy_f_bias, hy_w_out, hy_b_out):
    bsz, seq, d = x.shape
    ctx_len = ctx.shape[1]
    depth = w_mod.shape[0]
    assert bsz == 2, "the long convolution packs exactly two batches into one complex sequence"
    xs = x.reshape(bsz * seq, d)
    cs = ctx.reshape(bsz * ctx_len, d)

    cc = jnp.zeros((8, d), F32).at[:bsz].set(c).at[bsz].set(c_ctx)
    mod_all = _mod_vectors(cc, w_mod, b_mod).reshape(depth, 8, N_MOD, d)
    ctx_row = bsz

    c_rows = bsz * ctx_len
    p_sum, p_bcast = _head_sum_matrices()
    cos_t, sin_t = _rope_tables(seq)
    dft_tables = _dft_tables(seq)
    feats_x, feats_xt = _filter_features(seq)
    feats_c, feats_ct = _filter_features(ctx_len)
    wgu_all = ffn_w_gate_up.astype(BF16)
    wd_all = ffn_w_down.astype(BF16)
    zero_bias = jnp.zeros((d,), F32)

    for l in range(depth):
        mod = mod_all[l]
        is_attn = (l % 2) == 0
        ctx_out = l < depth - 1
        ctx_live = ctx_out or is_attn

        xs = _ffn(xs, mod, 0, norm_w[l, 0], wgu_all, wd_all, l, 0, seq, None)
        if ctx_live:
            cs = _ffn(cs, mod, 0, norm_w[l, 0], wgu_all, wd_all, l, 0, c_rows, ctx_row)

        if is_attn:
            a = l // 2
            w_qkv = attn_w_qkv[a].astype(BF16)
            w_o = attn_w_o[a].astype(BF16)
            qn = (jnp.tile(attn_q_norm[a], N_HEADS) * (HEAD_DIM ** -0.5 * math.log2(math.e))).reshape(1, -1)
            kn = jnp.tile(attn_k_norm[a], N_KV_HEADS).reshape(1, -1)
            q_l, kt_l, v_l = _attn_pre(xs, mod, norm_w[l, 1], w_qkv, qn, kn, p_sum, p_bcast,
                                       cos_t, sin_t, seq, None, True)
            q_c, kt_c, v_c = _attn_pre(cs, mod, norm_w[l, 1], w_qkv, qn, kn, p_sum, p_bcast,
                                       cos_t, sin_t, ctx_len, ctx_row, False)
            shift_ok = (HEAD_DIM * jnp.max(jnp.abs(qn)) * jnp.max(jnp.abs(kn))) < ATT_SHIFT_LIMIT
            o_l = lax.cond(shift_ok, functools.partial(_attention, False, seq),
                           functools.partial(_attention, True, seq), q_l, kt_c, v_c, kt_l, v_l)
            mix_x = (o_l, w_o, zero_bias)
            if ctx_out:
                o_c = lax.cond(shift_ok, functools.partial(_attention, False, ctx_len),
                               functools.partial(_attention, True, ctx_len), q_c, kt_c, v_c)
                mix_c = (o_c, w_o, zero_bias)
        else:
            j = l // 2
            w_in = hy_w_in[j].astype(BF16)
            w_out = hy_w_out[j].astype(BF16)
            fargs = (hy_f_w1[j], hy_f_b1[j], hy_f_w2[j], hy_f_b2[j], hy_f_w3[j], hy_f_b3[j], hy_f_freq[j])
            f_wout = hy_f_wout[j].astype(BF16)
            u_l = _hyena_pre(xs, mod, norm_w[l, 1], w_in, hy_b_in[j], hy_conv_w[j], hy_conv_b[j], seq, None)
            y_l = _hyena_long_convs(u_l, _filter_hidden(feats_xt, *fargs), feats_x, f_wout, hy_f_bias[j],
                                    seq, dft_tables)
            if ctx_out:
                u_c = _hyena_pre(cs, mod, norm_w[l, 1], w_in, hy_b_in[j], hy_conv_w[j], hy_conv_b[j],
                                 ctx_len, ctx_row)
                y_c = _hyena_small_convs(u_c, _filter_hidden(feats_ct, *fargs), feats_c, f_wout,
                                         hy_f_bias[j], ctx_len)
                mix_c = (y_c, w_out, hy_b_out[j])
            mix_x = (y_l, w_out, hy_b_out[j])

        xs = _ffn(xs, mod, 2, norm_w[l, 2], wgu_all, wd_all, l, 1, seq, None, mixer=mix_x)
        if ctx_out:
            cs = _ffn(cs, mod, 2, norm_w[l, 2], wgu_all, wd_all, l, 1, c_rows, ctx_row, mixer=mix_c)
    return xs.reshape(bsz, seq, d)
```
